```python
import jax, jax.numpy as jnp
from jax import lax
import numpy as np

D_MODEL = 1024
BATCH = 4
SEQ = 8192
DEPTH = 1
DEC_BATCH = 128
DEC_SEQ = 1
PAST_LEN = 8192
PAGE_SIZE = 128

HEAD_DIM = 64
N_HEADS_A = D_MODEL // 128
N_HEADS_B = D_MODEL // 128
WIDTH_A = N_HEADS_A * HEAD_DIM
WIDTH_B = N_HEADS_B * HEAD_DIM
MIX_WIDTH = WIDTH_A + WIDTH_B
CHUNK = 128
DILATED_CONFIGS = ((128, 1), (512, 4), (2048, 16))
MAX_WINDOW = 2048
N_GROUPS = 4
EXPERTS_PER_GROUP = 8
TOP_K_INNER = 2
D_EXPERT = D_MODEL // 4
EPS = 1e-6

kernel_name = 'hymba_gmlp_dilated_hiermoe_step'


def rmsnorm(x, g):
    xf = x.astype(jnp.float32)
    r = lax.rsqrt(jnp.mean(xf * xf, axis=-1, keepdims=True) + EPS)
    return (xf * r * g.astype(jnp.float32)).astype(x.dtype)


def mixer_inputs(x, norm_g, w_in, q_gain, k_gain):
    B, T, _ = x.shape
    z = jnp.einsum('btd,de->bte', rmsnorm(x, norm_g), w_in)
    q, k, v, ua, va = jnp.split(z, [WIDTH_B, 2 * WIDTH_B, 3 * WIDTH_B, 3 * WIDTH_B + WIDTH_A], axis=-1)
    hb = lambda t: t.reshape(B, T, N_HEADS_B, HEAD_DIM)
    ha = lambda t: t.reshape(B, T, N_HEADS_A, HEAD_DIM)
    q = rmsnorm(hb(q), q_gain)
    k = rmsnorm(hb(k), k_gain)
    return q, k, hb(v), ha(ua), ha(va)


def chunk_gating(ua, va, v_gain, w_s, b_s):
    B, T = ua.shape[:2]
    u = jax.nn.gelu(ua)
    v = rmsnorm(jax.nn.gelu(va), v_gain)
    nc = -(-T // CHUNK)
    vp = jnp.pad(v, ((0, 0), (0, nc * CHUNK - T), (0, 0), (0, 0)))
    vp = vp.reshape(B, nc, CHUNK, N_HEADS_A, HEAD_DIM)
    mixed = jnp.einsum('hqk,bnkhd->bnqhd', jnp.tril(w_s), vp) + b_s.T[None, None, :, :, None]
    mixed = mixed.reshape(B, nc * CHUNK, N_HEADS_A, HEAD_DIM)[:, :T]
    return u * mixed, v


def band_attention(q, k, v, dil, n):
    B, T, H, Dh = q.shape
    L = T // dil
    nb = -(-L // n)
    pad = nb * n - L

    def blocks(x):
        x = x.astype(jnp.float32).reshape(B, L, dil, H, Dh)
        x = jnp.pad(x, ((0, 0), (0, pad), (0, 0), (0, 0), (0, 0)))
        return x.reshape(B, nb, n, dil, H, Dh)

    def band(xb):
        prev = jnp.pad(xb, ((0, 0), (1, 0), (0, 0), (0, 0), (0, 0), (0, 0)))[:, :nb]
        return jnp.concatenate([prev, xb], axis=2)

    qb = blocks(q)
    kk, vv = band(blocks(k)), band(blocks(v))
    s = jnp.einsum('bnqrhd,bnkrhd->bnrhqk', qb, kk) * HEAD_DIM ** -0.5
    qi = jnp.arange(n)[:, None]
    ki = jnp.arange(2 * n)[None, :]
    dist = qi + n - ki
    valid = (dist >= 0) & (dist <= n) & ((jnp.arange(nb)[:, None, None] > 0) | (ki >= n)[None])
    s = jnp.where(valid[None, :, None, None], s, -jnp.inf)
    m = jnp.max(s, axis=-1, keepdims=True)
    e = jnp.exp(s - m)
    den = jnp.sum(e, axis=-1, keepdims=True)
    o = jnp.einsum('bnrhqk,bnkrhd->bnqrhd', e / den, vv)
    lse = (m + jnp.log(den))[..., 0]
    o = o.reshape(B, nb * n, dil, H, Dh)[:, :L].reshape(B, T, H, Dh)
    lse = lse.transpose(0, 1, 4, 2, 3).reshape(B, nb * n, dil, H)[:, :L].reshape(B, T, H)
    return o, lse


def gather_attention(q, k_all, v_all, offset, dil, n):
    S = q.shape[1]
    idx = offset + jnp.arange(S)[:, None] - dil * jnp.arange(n + 1)[None, :]
    valid = idx >= 0
    idx = jnp.maximum(idx, 0)
    kg = k_all[:, idx].astype(jnp.float32)
    vg = v_all[:, idx].astype(jnp.float32)
    s = jnp.einsum('bshd,bsjhd->bshj', q.astype(jnp.float32), kg) * HEAD_DIM ** -0.5
    s = jnp.where(valid[None, :, None, :], s, -jnp.inf)
    m = jnp.max(s, axis=-1, keepdims=True)
    e = jnp.exp(s - m)
    den = jnp.sum(e, axis=-1, keepdims=True)
    o = jnp.einsum('bshj,bsjhd->bshd', e / den, vg)
    return o, (m + jnp.log(den))[..., 0]


def merge_dilations(outs, lses, dtype):
    w = jax.nn.softmax(jnp.stack(lses, 0), axis=0)
    return jnp.einsum('cbth,cbthd->bthd', w, jnp.stack(outs, 0)).astype(dtype)


def merge_out(x, o_a, o_b, g_a, g_b, w_out):
    B, T, _ = x.shape
    ya = rmsnorm(o_a.reshape(B, T, WIDTH_A), g_a)
    yb = rmsnorm(o_b.reshape(B, T, WIDTH_B), g_b)
    return x + jnp.einsum('bte,ed->btd', jnp.concatenate([ya, yb], axis=-1), w_out)


def hier_moe(h, norm_g, w_r1, b_r1, w_r2, b_r2, w_up, w_gate, w_down):
    hn = rmsnorm(h, norm_g)
    lg = jnp.einsum('btd,dg->btg', hn, w_r1).astype(jnp.float32) + b_r1.astype(jnp.float32)
    g_star = jnp.argmax(lg, axis=-1)
    p_star = jnp.take_along_axis(jax.nn.softmax(lg, axis=-1), g_star[..., None], axis=-1)
    lf = jnp.einsum('btd,gde->btge', hn, w_r2).astype(jnp.float32) + b_r2.astype(jnp.float32)
    lf_sel = jnp.take_along_axis(lf, g_star[..., None, None], axis=2)[:, :, 0]
    top_v, top_i = lax.top_k(lf_sel, TOP_K_INNER)
    w_top = jax.nn.softmax(top_v, axis=-1) * p_star
    gate_e = jnp.sum(jax.nn.one_hot(top_i, EXPERTS_PER_GROUP, dtype=jnp.float32) * w_top[..., None], axis=-2)
    gate = (jax.nn.one_hot(g_star, N_GROUPS, dtype=jnp.float32)[..., None] * gate_e[..., None, :]).astype(hn.dtype)
    out = jnp.zeros_like(h)
    for g in range(N_GROUPS):
        a = jnp.einsum('btd,edf->btef', hn, w_gate[g])
        b = jnp.einsum('btd,edf->btef', hn, w_up[g])
        hid = jax.nn.silu(a) * b * gate[:, :, g, :, None]
        out = out + jnp.einsum('btef,efd->btd', hid, w_down[g])
    return h + out


def setup_inputs(seed: int = 0) -> dict:
    key = jax.random.key(seed)
    ks = jax.random.split(key, 24)
    f32 = jnp.float32
    nrm = lambda k, shape, scale: jax.random.normal(k, shape, f32) * scale
    gain = lambda k, shape: 1.0 + 0.1 * jax.random.normal(k, shape, f32)
    wb = min(MAX_WINDOW, PAST_LEN)
    return {
        'x_prompt': nrm(ks[0], (BATCH, SEQ, D_MODEL), 1.0),
        'x_sample': nrm(ks[1], (DEC_BATCH, DEC_SEQ, D_MODEL), 1.0),
        'cache_k': nrm(ks[2], (DEPTH, DEC_BATCH, wb, N_HEADS_B, HEAD_DIM), 1.0),
        'cache_v': nrm(ks[3], (DEPTH, DEC_BATCH, wb, N_HEADS_B, HEAD_DIM), 1.0),
        'norm1_g': gain(ks[4], (DEPTH, D_MODEL)),
        'w_in': nrm(ks[5], (DEPTH, D_MODEL, 3 * WIDTH_B + 2 * WIDTH_A), D_MODEL ** -0.5),
        'q_gain': gain(ks[6], (DEPTH, N_HEADS_B, HEAD_DIM)),
        'k_gain': gain(ks[7], (DEPTH, N_HEADS_B, HEAD_DIM)),
        'v_gain': gain(ks[8], (DEPTH, N_HEADS_A, HEAD_DIM)),
        'w_spatial': nrm(ks[9], (DEPTH, N_HEADS_A, CHUNK, CHUNK), CHUNK ** -0.5),
        'b_spatial': gain(ks[10], (DEPTH, N_HEADS_A, CHUNK)),
        'out_gain_a': gain(ks[11], (DEPTH, WIDTH_A)),
        'out_gain_b': gain(ks[12], (DEPTH, WIDTH_B)),
        'w_out': nrm(ks[13], (DEPTH, MIX_WIDTH, D_MODEL), MIX_WIDTH ** -0.5),
        'norm2_g': gain(ks[14], (DEPTH, D_MODEL)),
        'w_router1': nrm(ks[15], (DEPTH, D_MODEL, N_GROUPS), D_MODEL ** -0.5),
        'b_router1': nrm(ks[16], (DEPTH, N_GROUPS), 0.01),
        'w_router2': nrm(ks[17], (DEPTH, N_GROUPS, D_MODEL, EXPERTS_PER_GROUP), D_MODEL ** -0.5),
        'b_router2': nrm(ks[18], (DEPTH, N_GROUPS, EXPERTS_PER_GROUP), 0.01),
        'w_up': nrm(ks[19], (DEPTH, N_GROUPS, EXPERTS_PER_GROUP, D_MODEL, D_EXPERT), D_MODEL ** -0.5),
        'w_gate': nrm(ks[20], (DEPTH, N_GROUPS, EXPERTS_PER_GROUP, D_MODEL, D_EXPERT), D_MODEL ** -0.5),
        'w_down': nrm(ks[21], (DEPTH, N_GROUPS, EXPERTS_PER_GROUP, D_EXPERT, D_MODEL), D_EXPERT ** -0.5),
    }


def reference(x_prompt, x_sample, cache_k, cache_v, norm1_g, w_in, q_gain, k_gain, v_gain,
              w_spatial, b_spatial, out_gain_a, out_gain_b, w_out, norm2_g, w_router1, b_router1,
              w_router2, b_router2, w_up, w_gate, w_down):
    yp, ys = x_prompt, x_sample
    wb = cache_k.shape[2]
    nkp, nvp, nks, nvs, nva = [], [], [], [], []
    for l in range(DEPTH):
        moe = lambda h: hier_moe(h, norm2_g[l], w_router1[l], b_router1[l], w_router2[l], b_router2[l],
                                 w_up[l], w_gate[l], w_down[l])
        q, k, v, ua, va = mixer_inputs(yp, norm1_g[l], w_in[l], q_gain[l], k_gain[l])
        res = [band_attention(q, k, v, d, w // d) for (w, d) in DILATED_CONFIGS]
        o_b = merge_dilations([r[0] for r in res], [r[1] for r in res], q.dtype)
        o_a, _ = chunk_gating(ua, va, v_gain[l], w_spatial[l], b_spatial[l])
        nkp.append(k[:, -MAX_WINDOW:])
        nvp.append(v[:, -MAX_WINDOW:])
        yp = moe(merge_out(yp, o_a, o_b, out_gain_a[l], out_gain_b[l], w_out[l]))
        q, k, v, ua, va = mixer_inputs(ys, norm1_g[l], w_in[l], q_gain[l], k_gain[l])
        k_all = jnp.concatenate([cache_k[l].astype(k.dtype), k], axis=1)
        v_all = jnp.concatenate([cache_v[l].astype(v.dtype), v], axis=1)
        res = [gather_attention(q, k_all, v_all, wb, d, w // d) for (w, d) in DILATED_CONFIGS]
        o_b = merge_dilations([r[0] for r in res], [r[1] for r in res], q.dtype)
        o_a, v_rows = chunk_gating(ua, va, v_gain[l], w_spatial[l], b_spatial[l])
        nks.append(k)
        nvs.append(v)
        nva.append(v_rows)
        ys = moe(merge_out(ys, o_a, o_b, out_gain_a[l], out_gain_b[l], w_out[l]))
    new_k_prompt = jnp.stack(nkp, 0)
    new_v_prompt = jnp.stack(nvp, 0)
    new_k_sample = jnp.stack(nks, 0)
    new_v_sample = jnp.stack(nvs, 0)
    new_chunk_v_sample = jnp.stack(nva, 0)
    return (yp, ys, new_k_prompt, new_v_prompt, new_k_sample, new_v_sample, new_chunk_v_sample)
```

```python
import functools

import jax
import jax.numpy as jnp
from jax import lax
from jax.experimental import pallas as pl
from jax.experimental.pallas import tpu as pltpu

F32 = jnp.float32
BF16 = jnp.bfloat16

D_MODEL = 1024
HEAD_DIM = 64
N_HEADS = 8
WIDTH = N_HEADS * HEAD_DIM
N_PAIRS = N_HEADS // 2
CHUNK = 128
N_WIN = 128
DILATIONS = (1, 4, 16)
MAX_WINDOW = 2048
N_GROUPS = 4
EXPERTS_PER_GROUP = 8
N_EXPERTS = N_GROUPS * EXPERTS_PER_GROUP
D_EXPERT = D_MODEL // 4
EPS = 1e-6

LANES = 128
SUBLANES = 8
ROW_TILES = D_MODEL // LANES
VMEM_LIMIT = 48 * 1024 * 1024

PROJ_TILE = 512
ATTN_TILE = 512
MIX_TILE = 256
EXPERT_TILE = 256
COMBINE_TILE = 256


def _dot(a, b):
    return jnp.dot(a, b, preferred_element_type=F32)


def _gelu(x):
    return 0.5 * x * (1.0 + jnp.tanh(0.7978845608028654 * (x + 0.044715 * (x * x * x))))


def _head_rms(t, gain, bd):
    ss = _dot((t * t).astype(BF16), bd)
    return t * lax.rsqrt(ss * (1.0 / HEAD_DIM) + EPS) * gain


def _proj_common(x_ref, g1_ref, win_ref, gains_ref, bd_ref):
    x = x_ref[...]
    r = lax.rsqrt(jnp.mean(x * x, axis=-1, keepdims=True) + EPS)
    xn = (x * r * g1_ref[...]).astype(BF16)
    bd = bd_ref[...]
    q = _head_rms(_dot(xn, win_ref[:, 0:WIDTH]), gains_ref[0:1, :], bd) * (HEAD_DIM ** -0.5)
    k = _head_rms(_dot(xn, win_ref[:, WIDTH:2 * WIDTH]), gains_ref[1:2, :], bd)
    v = _dot(xn, win_ref[:, 2 * WIDTH:3 * WIDTH])
    u = _gelu(_dot(xn, win_ref[:, 3 * WIDTH:4 * WIDTH]))
    vg = _head_rms(_gelu(_dot(xn, win_ref[:, 4 * WIDTH:5 * WIDTH])), gains_ref[2:3, :], bd)
    return q, k, v, u, vg


def _proj_prompt_kernel(x_ref, g1_ref, win_ref, gains_ref, bd_ref, wcat_ref, bias_ref,
                        q_ref, k_ref, v_ref, oa_ref, kl_ref, vl_ref, *, tiles_per_seq, first_kept_tile):
    q, k, v, u, vg = _proj_common(x_ref, g1_ref, win_ref, gains_ref, bd_ref)
    q_ref[...] = q.astype(BF16)
    k_ref[...] = k.astype(BF16)
    v_ref[...] = v.astype(BF16)

    @pl.when(pl.program_id(0) % tiles_per_seq >= first_kept_tile)
    def _():
        kl_ref[...] = k
        vl_ref[...] = v

    vgb = vg.astype(BF16)
    lane = lax.broadcasted_iota(jnp.int32, (CHUNK, LANES), 1)
    tm = x_ref.shape[0]
    for c in range(tm // CHUNK):
        rows = slice(c * CHUNK, (c + 1) * CHUNK)
        for p in range(N_PAIRS):
            cols = slice(p * LANES, (p + 1) * LANES)
            vp = vgb[rows, cols]
            zero = jnp.zeros_like(vp)
            rhs = jnp.concatenate([jnp.where(lane < HEAD_DIM, vp, zero),
                                   jnp.where(lane >= HEAD_DIM, vp, zero)], axis=0)
            mixed = _dot(wcat_ref[p], rhs) + bias_ref[p]
            oa_ref[rows, cols] = (u[rows, cols] * mixed).astype(BF16)


def _proj_sample_kernel(x_ref, g1_ref, win_ref, gains_ref, bd_ref, avec_ref, bvec_ref,
                        q_ref, k_ref, v_ref, vg_ref, oa_ref):
    q, k, v, u, vg = _proj_common(x_ref, g1_ref, win_ref, gains_ref, bd_ref)
    q_ref[...] = q
    k_ref[...] = k
    v_ref[...] = v
    vg_ref[...] = vg
    oa_ref[...] = (u * (avec_ref[...] * vg + bvec_ref[...])).astype(BF16)


def _const_spec(shape):
    return pl.BlockSpec(shape, lambda *_: (0,) * len(shape))


def _proj_prompt(x2, g1, win, gains, bd, wcat, bias, *, seq):
    n = x2.shape[0]
    tm = PROJ_TILE
    tiles_per_seq = seq // tm
    kept = min(MAX_WINDOW, seq)
    kept_tiles = kept // tm
    first_kept_tile = tiles_per_seq - kept_tiles
    n_kept = (n // seq) * kept

    def kept_map(i):
        b = i // tiles_per_seq
        t = i % tiles_per_seq
        return (b * kept_tiles + jnp.maximum(t - first_kept_tile, 0), 0)

    row_spec = pl.BlockSpec((tm, WIDTH), lambda i: (i, 0))
    kept_spec = pl.BlockSpec((tm, WIDTH), kept_map)
    return pl.pallas_call(
        functools.partial(_proj_prompt_kernel, tiles_per_seq=tiles_per_seq, first_kept_tile=first_kept_tile),
        grid=(n // tm,),
        in_specs=[pl.BlockSpec((tm, D_MODEL), lambda i: (i, 0)),
                  _const_spec((1, D_MODEL)), _const_spec((D_MODEL, 5 * WIDTH)), _const_spec((3, WIDTH)),
                  _const_spec((WIDTH, WIDTH)), _const_spec((N_PAIRS, CHUNK, 2 * CHUNK)),
                  _const_spec((N_PAIRS, CHUNK, LANES))],
        out_specs=[row_spec, row_spec, row_spec, row_spec, kept_spec, kept_spec],
        out_shape=[jax.ShapeDtypeStruct((n, WIDTH), BF16)] * 4 + [jax.ShapeDtypeStruct((n_kept, WIDTH), F32)] * 2,
        compiler_params=pltpu.CompilerParams(dimension_semantics=("arbitrary",), vmem_limit_bytes=VMEM_LIMIT),
        name="proj_prompt",
    )(x2, g1, win, gains, bd, wcat, bias)


def _proj_sample(x2, g1, win, gains, bd, avec, bvec):
    n = x2.shape[0]
    full = _const_spec((n, WIDTH))
    return pl.pallas_call(
        _proj_sample_kernel,
        grid=(1,),
        in_specs=[_const_spec((n, D_MODEL)), _const_spec((1, D_MODEL)), _const_spec((D_MODEL, 5 * WIDTH)),
                  _const_spec((3, WIDTH)), _const_spec((WIDTH, WIDTH)), _const_spec((1, WIDTH)),
                  _const_spec((1, WIDTH))],
        out_specs=[full] * 5,
        out_shape=[jax.ShapeDtypeStruct((n, WIDTH), F32)] * 4 + [jax.ShapeDtypeStruct((n, WIDTH), BF16)],
        compiler_params=pltpu.CompilerParams(dimension_semantics=("arbitrary",), vmem_limit_bytes=VMEM_LIMIT),
        name="proj_sample",
    )(x2, g1, win, gains, bd, avec, bvec)


def _band_attn_kernel(q_ref, k_ref, v_ref, o_ref, lse_ref, kbuf, vbuf, *, tq):
    j = pl.program_id(2)
    nblk = tq // N_WIN

    @pl.when(j == 0)
    def _():
        kbuf[0:N_WIN, :] = jnp.zeros((N_WIN, WIDTH), BF16)
        vbuf[...] = jnp.ones(vbuf.shape, BF16)

    @pl.when(j > 0)
    def _():
        kbuf[0:N_WIN, :] = kbuf[tq:tq + N_WIN, :]
        vbuf[0:N_WIN, :] = vbuf[tq:tq + N_WIN, :]

    kbuf[N_WIN:N_WIN + tq, :] = k_ref[...]
    for p in range(N_PAIRS):
        vbuf[N_WIN:N_WIN + tq, 2 * p * LANES:(2 * p + 1) * LANES] = v_ref[:, p * LANES:(p + 1) * LANES]

    qi = lax.broadcasted_iota(jnp.int32, (N_WIN, 2 * N_WIN), 0)
    ki = lax.broadcasted_iota(jnp.int32, (N_WIN, 2 * N_WIN), 1)
    band = (ki >= qi) & (ki <= qi + N_WIN)
    lane = lax.broadcasted_iota(jnp.int32, (N_WIN, LANES), 1)
    low_half = lane < HEAD_DIM

    def block(jb, carry):
        q0 = pl.multiple_of(jb * N_WIN, N_WIN)
        valid = band & ((ki >= N_WIN) | (j * nblk + jb > 0))
        lse_tile = jnp.zeros((N_WIN, LANES), F32)
        for p in range(N_PAIRS):
            cols = slice(p * LANES, (p + 1) * LANES)
            qp = q_ref[pl.ds(q0, N_WIN), cols]
            kp = kbuf[pl.ds(q0, 2 * N_WIN), cols]
            vp = vbuf[pl.ds(q0, 2 * N_WIN), 2 * p * LANES:(2 * p + 2) * LANES]
            halves = []
            for hh in range(2):
                mine = low_half if hh == 0 else ~low_half
                qm = jnp.where(mine, qp, jnp.zeros_like(qp))
                s = lax.dot_general(qm, kp, (((1,), (1,)), ((), ())), preferred_element_type=F32)
                s = jnp.where(valid, s, -jnp.inf)
                m = jnp.max(s, axis=-1, keepdims=True)
                e = jnp.exp(s - m)
                ov = _dot(e.astype(BF16), vp)
                den = ov[:, LANES:2 * LANES]
                halves.append(ov[:, 0:LANES] / den)
                lse_tile = jnp.where(lane == 2 * p + hh, m + jnp.log(den), lse_tile)
            o_ref[pl.ds(q0, N_WIN), cols] = jnp.where(low_half, halves[0], halves[1]).astype(BF16)
        lse_ref[pl.ds(q0, N_WIN), :] = lse_tile
        return carry

    lax.fori_loop(0, nblk, block, 0)


def _band_attention(q2, k2, v2, *, batch, seq, dil):
    length = seq // dil
    tq = min(ATTN_TILE, length)
    view = lambda a: a.reshape(batch, length, dil * a.shape[-1])
    spec = pl.BlockSpec((None, tq, WIDTH), lambda b, r, j: (b, j, r))
    lse_spec = pl.BlockSpec((None, tq, LANES), lambda b, r, j: (b, j, r))
    o, lse = pl.pallas_call(
        functools.partial(_band_attn_kernel, tq=tq),
        grid=(batch, dil, length // tq),
        in_specs=[spec, spec, spec],
        out_specs=[spec, lse_spec],
        out_shape=[jax.ShapeDtypeStruct((batch, length, dil * WIDTH), BF16),
                   jax.ShapeDtypeStruct((batch, length, dil * LANES), F32)],
        scratch_shapes=[pltpu.VMEM((N_WIN + tq, WIDTH), BF16), pltpu.VMEM((N_WIN + tq, 2 * WIDTH), BF16)],
        compiler_params=pltpu.CompilerParams(dimension_semantics=("arbitrary", "arbitrary", "arbitrary"),
                                             vmem_limit_bytes=VMEM_LIMIT),
        name=f"band_attn_d{dil}",
    )(view(q2), view(k2), view(v2))
    return o.reshape(batch * seq, WIDTH), lse.reshape(batch * seq, LANES)


def _sample_attn_kernel(q_ref, kn_ref, vn_ref, k1_ref, k4_ref, k16_ref, v1_ref, v4_ref, v16_ref, o_ref):
    q = q_ref[...]
    kn = kn_ref[...]
    vn = vn_ref[...]
    s0 = jnp.sum(q * kn, axis=-1, keepdims=True)
    outs, lses = [], []
    for kc_ref, vc_ref in ((k1_ref, v1_ref), (k4_ref, v4_ref), (k16_ref, v16_ref)):
        s = jnp.sum(kc_ref[...] * q[None], axis=-1, keepdims=True)
        m = jnp.maximum(jnp.max(s, axis=0), s0)
        e = jnp.exp(s - m[None])
        e0 = jnp.exp(s0 - m)
        den = jnp.sum(e, axis=0) + e0
        outs.append((jnp.sum(e * vc_ref[...], axis=0) + e0 * vn) / den)
        lses.append(m + jnp.log(den))
    mm = jnp.maximum(jnp.maximum(lses[0], lses[1]), lses[2])
    ws = [jnp.exp(l - mm) for l in lses]
    tot = ws[0] + ws[1] + ws[2]
    o_ref[...] = (ws[0] * outs[0] + ws[1] * outs[1] + ws[2] * outs[2]) / tot


def _sample_attention(q3, kn3, vn3, cache_k, cache_v):
    nb, wb = cache_k.shape[0], cache_k.shape[1]
    assert wb == MAX_WINDOW, "cache window must cover every dilated key"
    tok = pl.BlockSpec((None, N_HEADS, HEAD_DIM), lambda b: (b, 0, 0))
    specs, args = [], []
    for cache in (cache_k, cache_v):
        for dil in DILATIONS:
            groups = wb // dil
            args.append(cache.reshape(nb, groups, dil, N_HEADS, HEAD_DIM))
            specs.append(pl.BlockSpec((None, N_WIN, None, N_HEADS, HEAD_DIM),
                                      functools.partial(lambda b, g: (b, g, 0, 0, 0), g=groups // N_WIN - 1)))
    return pl.pallas_call(
        _sample_attn_kernel,
        grid=(nb,),
        in_specs=[tok, tok, tok] + specs,
        out_specs=tok,
        out_shape=jax.ShapeDtypeStruct((nb, N_HEADS, HEAD_DIM), F32),
        compiler_params=pltpu.CompilerParams(dimension_semantics=("arbitrary",), vmem_limit_bytes=VMEM_LIMIT),
        name="sample_attn",
    )(q3, kn3, vn3, *args)


def _split_hi_lo(w):
    hi = w.astype(BF16)
    return jnp.concatenate([hi, (w - hi.astype(F32)).astype(BF16)], axis=-1)


def _mix_kernel(*refs, n_cfg):
    x_ref, oa_ref = refs[0], refs[1]
    o_refs = refs[2:2 + n_cfg]
    n_lse = n_cfg if n_cfg > 1 else 0
    lse_refs = refs[2 + n_cfg:2 + n_cfg + n_lse]
    (ga_ref, gb_ref, wout_ref, g2_ref, wr_ref, br_ref, expand_ref, tri_ref,
     h_ref, hn_ref, route_ref, cnt_ref, carry) = refs[2 + n_cfg + n_lse:]
    tm = x_ref.shape[0]
    i = pl.program_id(0)

    @pl.when(i == 0)
    def _():
        carry[...] = jnp.zeros_like(carry)

    if n_cfg == 1:
        ob = o_refs[0][...].astype(F32)
    else:
        lses = [r[...] for r in lse_refs]
        mm = functools.reduce(jnp.maximum, lses)
        ws = [jnp.exp(l - mm) for l in lses]
        tot = functools.reduce(lambda a, b: a + b, ws)
        ob = jnp.zeros((tm, WIDTH), F32)
        for w, o_ref in zip(ws, o_refs):
            ob = ob + _dot(_split_hi_lo(w / tot), expand_ref[...]) * o_ref[...].astype(F32)

    oa = oa_ref[...].astype(F32)
    ya = oa * lax.rsqrt(jnp.mean(oa * oa, axis=-1, keepdims=True) + EPS) * ga_ref[...]
    yb = ob * lax.rsqrt(jnp.mean(ob * ob, axis=-1, keepdims=True) + EPS) * gb_ref[...]
    cat = jnp.concatenate([ya, yb], axis=-1).astype(BF16)
    h = x_ref[...] + _dot(cat, wout_ref[...])
    h_ref[...] = h
    hn = h * lax.rsqrt(jnp.mean(h * h, axis=-1, keepdims=True) + EPS) * g2_ref[...]
    for c in range(ROW_TILES):
        hn_ref[:, c, :] = hn[:, c * LANES:(c + 1) * LANES]

    logits = _dot(hn.astype(BF16), wr_ref[...]) + br_ref[...]
    lane = lax.broadcasted_iota(jnp.int32, (tm, LANES), 1)
    lane_f = lane.astype(F32)
    neg = -jnp.inf
    big = float(LANES)
    coarse = (lane >= N_EXPERTS) & (lane < N_EXPERTS + N_GROUPS)
    lg = jnp.where(coarse, logits, neg)
    mx = jnp.max(lg, axis=-1, keepdims=True)
    g_lane = jnp.min(jnp.where(lg == mx, lane_f, big), axis=-1, keepdims=True)
    p_star = 1.0 / jnp.sum(jnp.exp(lg - mx), axis=-1, keepdims=True)
    lo = (g_lane - float(N_EXPERTS)) * float(EXPERTS_PER_GROUP)
    lf = jnp.where((lane_f >= lo) & (lane_f < lo + float(EXPERTS_PER_GROUP)), logits, neg)
    v1 = jnp.max(lf, axis=-1, keepdims=True)
    i1 = jnp.min(jnp.where(lf == v1, lane_f, big), axis=-1, keepdims=True)
    lf2 = jnp.where(lane_f == i1, neg, lf)
    v2 = jnp.max(lf2, axis=-1, keepdims=True)
    i2 = jnp.min(jnp.where(lf2 == v2, lane_f, big), axis=-1, keepdims=True)
    e21 = jnp.exp(v2 - v1)
    w1 = p_star / (1.0 + e21)
    w2 = p_star * e21 / (1.0 + e21)

    sel1 = lane_f == i1
    sel2 = lane_f == i2
    onehot = jnp.where(sel1 | sel2, 1.0, 0.0)
    before = _dot(tri_ref[...], onehot.astype(BF16)) + carry[...]
    r1 = jnp.sum(jnp.where(sel1, before, 0.0), axis=-1, keepdims=True)
    r2 = jnp.sum(jnp.where(sel2, before, 0.0), axis=-1, keepdims=True)
    carry[...] = carry[...] + jnp.sum(onehot, axis=0, keepdims=True)
    cnt_ref[...] = carry[...]

    col = lax.broadcasted_iota(jnp.int32, (tm, SUBLANES), 1)
    route = jnp.zeros((tm, SUBLANES), F32)
    for idx, val in enumerate((i1, i2, r1, r2, w1, w2)):
        route = jnp.where(col == idx, val, route)
    route_ref[...] = route


def _mix(x2, oa, obs, lses, ga, gb, wout, g2, wr, br, expand, tri):
    n = x2.shape[0]
    tm = min(MIX_TILE, n)
    n_cfg = len(obs)
    row = lambda w: pl.BlockSpec((tm, w), lambda i: (i, 0))
    in_specs = ([row(D_MODEL), row(WIDTH)] + [row(WIDTH)] * n_cfg + [row(LANES)] * len(lses) +
                [_const_spec((1, WIDTH)), _const_spec((1, WIDTH)), _const_spec((D_MODEL, D_MODEL)),
                 _const_spec((1, D_MODEL)), _const_spec((D_MODEL, LANES)), _const_spec((1, LANES)),
                 _const_spec((2 * LANES, WIDTH)), _const_spec((tm, tm))])
    return pl.pallas_call(
        functools.partial(_mix_kernel, n_cfg=n_cfg),
        grid=(n // tm,),
        in_specs=in_specs,
        out_specs=[row(D_MODEL), pl.BlockSpec((tm, ROW_TILES, LANES), lambda i: (i, 0, 0)),
                   row(SUBLANES), _const_spec((1, LANES))],
        out_shape=[jax.ShapeDtypeStruct((n, D_MODEL), F32), jax.ShapeDtypeStruct((n, ROW_TILES, LANES), F32),
                   jax.ShapeDtypeStruct((n, SUBLANES), F32), jax.ShapeDtypeStruct((1, LANES), F32)],
        scratch_shapes=[pltpu.VMEM((1, LANES), F32)],
        compiler_params=pltpu.CompilerParams(dimension_semantics=("arbitrary",), vmem_limit_bytes=VMEM_LIMIT),
        name="mix",
    )(x2, oa, *obs, *lses, ga, gb, wout, g2, wr, br, expand, tri)


def _row_copy(src_hbm, row, buf, slot, sem):
    return pltpu.make_async_copy(src_hbm.at[row], buf.at[pl.ds(slot * ROW_TILES, ROW_TILES), :], sem)


def _gather_rows(idx_ref, n_rows, src_hbm, buf, sem):
    def start(r, c):
        _row_copy(src_hbm, idx_ref[0, 0, r], buf, r, sem).start()
        return c

    lax.fori_loop(0, n_rows, start, 0)

    def wait(r, c):
        _row_copy(src_hbm, 0, buf, r, sem).wait()
        return c

    lax.fori_loop(0, n_rows, wait, 0)


def _gathered(buf, n_rows):
    return jnp.concatenate([buf[pl.ds(c, n_rows, stride=ROW_TILES), :] for c in range(ROW_TILES)], axis=1)


def _expert_kernel(te_ref, src_ref, hn_hbm, wg_ref, wu_ref, wd_ref, out_ref, xbuf, sem):
    tr = out_ref.shape[0]
    _gather_rows(src_ref, tr, hn_hbm, xbuf, sem)
    x = _gathered(xbuf, tr).astype(BF16)
    a = _dot(x, wg_ref[...])
    b = _dot(x, wu_ref[...])
    hid = (a * jax.nn.sigmoid(a)) * b
    out = _dot(hid.astype(BF16), wd_ref[...])
    for c in range(ROW_TILES):
        out_ref[:, c, :] = out[:, c * LANES:(c + 1) * LANES]


def _experts(tile_expert, src, hn3, wg, wu, wd):
    n_tiles = src.shape[0]
    tr = EXPERT_TILE
    grid_spec = pltpu.PrefetchScalarGridSpec(
        num_scalar_prefetch=1,
        grid=(n_tiles,),
        in_specs=[pl.BlockSpec((1, 1, tr), lambda i, te: (i, 0, 0), memory_space=pltpu.SMEM),
                  pl.BlockSpec(memory_space=pl.ANY),
                  pl.BlockSpec((None, D_MODEL, D_EXPERT), lambda i, te: (te[i], 0, 0)),
                  pl.BlockSpec((None, D_MODEL, D_EXPERT), lambda i, te: (te[i], 0, 0)),
                  pl.BlockSpec((None, D_EXPERT, D_MODEL), lambda i, te: (te[i], 0, 0))],
        out_specs=pl.BlockSpec((tr, ROW_TILES, LANES), lambda i, te: (i, 0, 0)),
        scratch_shapes=[pltpu.VMEM((tr * ROW_TILES, LANES), F32), pltpu.SemaphoreType.DMA],
    )
    return pl.pallas_call(
        _expert_kernel,
        grid_spec=grid_spec,
        out_shape=jax.ShapeDtypeStruct((n_tiles * tr, ROW_TILES, LANES), F32),
        compiler_params=pltpu.CompilerParams(dimension_semantics=("arbitrary",), vmem_limit_bytes=VMEM_LIMIT),
        name="experts",
    )(tile_expert, src, hn3, wg, wu, wd)


def _combine_kernel(pos_ref, h_ref, route_ref, out_hbm, y_ref, gbuf, sem):
    tm = h_ref.shape[0]
    _gather_rows(pos_ref, 2 * tm, out_hbm, gbuf, sem)
    g = _gathered(gbuf, 2 * tm)
    route = route_ref[...]
    y_ref[...] = h_ref[...] + route[:, 4:5] * g[0:tm] + route[:, 5:6] * g[tm:2 * tm]


def _combine(pos, h, route, out3):
    n = h.shape[0]
    tm = min(COMBINE_TILE, n)
    return pl.pallas_call(
        _combine_kernel,
        grid=(n // tm,),
        in_specs=[pl.BlockSpec((1, 1, 2 * tm), lambda i: (i, 0, 0), memory_space=pltpu.SMEM),
                  pl.BlockSpec((tm, D_MODEL), lambda i: (i, 0)),
                  pl.BlockSpec((tm, SUBLANES), lambda i: (i, 0)),
                  pl.BlockSpec(memory_space=pl.ANY)],
        out_specs=pl.BlockSpec((tm, D_MODEL), lambda i: (i, 0)),
        out_shape=jax.ShapeDtypeStruct((n, D_MODEL), F32),
        scratch_shapes=[pltpu.VMEM((2 * tm * ROW_TILES, LANES), F32), pltpu.SemaphoreType.DMA],
        compiler_params=pltpu.CompilerParams(dimension_semantics=("arbitrary",), vmem_limit_bytes=VMEM_LIMIT),
        name="combine",
    )(pos, h, route, out3)


def _moe(h, hn3, route, counts, wg, wu, wd):
    n = h.shape[0]
    tr = EXPERT_TILE
    tm = min(COMBINE_TILE, n)
    n_slots = (pl.cdiv(2 * n, tr) + N_EXPERTS) * tr
    cnt = counts[0, :N_EXPERTS].astype(jnp.int32)
    padded = ((cnt + tr - 1) // tr) * tr
    ends = jnp.cumsum(padded)
    starts = ends - padded
    e1 = route[:, 0].astype(jnp.int32)
    e2 = route[:, 1].astype(jnp.int32)
    pos1 = starts[e1] + route[:, 2].astype(jnp.int32)
    pos2 = starts[e2] + route[:, 3].astype(jnp.int32)
    tok = jnp.arange(n, dtype=jnp.int32)
    src = jnp.zeros((n_slots,), jnp.int32).at[pos1].set(tok).at[pos2].set(tok)
    tile_start = jnp.arange(n_slots // tr, dtype=jnp.int32) * tr
    tile_expert = jnp.minimum(jnp.searchsorted(ends, tile_start, side="right"), N_EXPERTS - 1).astype(jnp.int32)
    out3 = _experts(tile_expert, src.reshape(n_slots // tr, 1, tr), hn3, wg, wu, wd)
    pos = jnp.concatenate([pos1.reshape(n // tm, 1, tm), pos2.reshape(n // tm, 1, tm)], axis=-1)
    return _combine(pos, h, route, out3)


def kernel(x_prompt, x_sample, cache_k, cache_v, norm1_g, w_in, q_gain, k_gain, v_gain, w_spatial, b_spatial,
           out_gain_a, out_gain_b, w_out, norm2_g, w_router1, b_router1, w_router2, b_router2, w_up, w_gate,
           w_down):
    depth = norm1_g.shape[0]
    assert depth == 1
    l = 0
    batch, seq, _ = x_prompt.shape
    nb, dec_seq, _ = x_sample.shape
    assert dec_seq == 1

    g1 = norm1_g[l].reshape(1, D_MODEL)
    win = w_in[l].astype(BF16)
    gains = jnp.stack([q_gain[l].reshape(WIDTH), k_gain[l].reshape(WIDTH), v_gain[l].reshape(WIDTH)])
    head_of_lane = jnp.arange(WIDTH) // HEAD_DIM
    bd = (head_of_lane[:, None] == head_of_lane[None, :]).astype(BF16)
    ws_tril = jnp.tril(w_spatial[l])
    wcat = jnp.concatenate([ws_tril[0::2], ws_tril[1::2]], axis=-1).astype(BF16)
    bs = b_spatial[l]
    bias = jnp.where(jnp.arange(LANES)[None, None, :] < HEAD_DIM, bs[0::2][:, :, None], bs[1::2][:, :, None])
    avec = jnp.repeat(w_spatial[l][:, 0, 0], HEAD_DIM).reshape(1, WIDTH)
    bvec = jnp.repeat(bs[:, 0], HEAD_DIM).reshape(1, WIDTH)
    ga = out_gain_a[l].reshape(1, WIDTH)
    gb = out_gain_b[l].reshape(1, WIDTH)
    wout = w_out[l].astype(BF16)
    g2 = norm2_g[l].reshape(1, D_MODEL)
    wr = jnp.zeros((D_MODEL, LANES), F32)
    wr = wr.at[:, :N_EXPERTS].set(jnp.transpose(w_router2[l], (1, 0, 2)).reshape(D_MODEL, N_EXPERTS))
    wr = wr.at[:, N_EXPERTS:N_EXPERTS + N_GROUPS].set(w_router1[l]).astype(BF16)
    br = jnp.zeros((1, LANES), F32)
    br = br.at[0, :N_EXPERTS].set(b_router2[l].reshape(N_EXPERTS))
    br = br.at[0, N_EXPERTS:N_EXPERTS + N_GROUPS].set(b_router1[l])
    lane_head = (jnp.arange(LANES)[:, None] == head_of_lane[None, :]).astype(BF16)
    expand = jnp.concatenate([lane_head, lane_head], axis=0)
    wg = w_gate[l].reshape(N_EXPERTS, D_MODEL, D_EXPERT).astype(BF16)
    wu = w_up[l].reshape(N_EXPERTS, D_MODEL, D_EXPERT).astype(BF16)
    wd = w_down[l].reshape(N_EXPERTS, D_EXPERT, D_MODEL).astype(BF16)

    def tri(t):
        return (jnp.arange(t)[:, None] > jnp.arange(t)[None, :]).astype(BF16)

    xp = x_prompt.reshape(batch * seq, D_MODEL)
    q, k, v, oa, k_last, v_last = _proj_prompt(xp, g1, win, gains, bd, wcat, bias, seq=seq)
    obs, lses = [], []
    for dil in DILATIONS:
        o, lse = _band_attention(q, k, v, batch=batch, seq=seq, dil=dil)
        obs.append(o)
        lses.append(lse)
    tmix = min(MIX_TILE, batch * seq)
    h, hn3, route, counts = _mix(xp, oa, obs, lses, ga, gb, wout, g2, wr, br, expand, tri(tmix))
    y_prompt = _moe(h, hn3, route, counts, wg, wu, wd).reshape(batch, seq, D_MODEL)
    kept = min(MAX_WINDOW, seq)
    new_k_prompt = k_last.reshape(1, batch, kept, N_HEADS, HEAD_DIM)
    new_v_prompt = v_last.reshape(1, batch, kept, N_HEADS, HEAD_DIM)

    xs = x_sample.reshape(nb, D_MODEL)
    qs, ks, vs, vgs, oas = _proj_sample(xs, g1, win, gains, bd, avec, bvec)
    heads = lambda t: t.reshape(nb, N_HEADS, HEAD_DIM)
    obs_s = _sample_attention(heads(qs), heads(ks), heads(vs), cache_k[l], cache_v[l]).reshape(nb, WIDTH)
    tmix_s = min(MIX_TILE, nb)
    hs, hn3s, route_s, counts_s = _mix(xs, oas, [obs_s], [], ga, gb, wout, g2, wr, br, expand, tri(tmix_s))
    y_sample = _moe(hs, hn3s, route_s, counts_s, wg, wu, wd).reshape(nb, 1, D_MODEL)
    to5 = lambda t: t.reshape(1, nb, 1, N_HEADS, HEAD_DIM)
    return (y_prompt, y_sample, new_k_prompt, new_v_prompt, to5(ks), to5(vs), to5(vgs))
```

```python
import functools

import jax
import jax.numpy as jnp
from jax import lax
from jax.experimental import pallas as pl
from jax.experimental.pallas import tpu as pltpu

F32 = jnp.float32
BF16 = jnp.bfloat16
I32 = jnp.int32

D_MODEL = 1024
HEAD_DIM = 64
N_HEADS = 8
WIDTH = N_HEADS * HEAD_DIM
N_PAIRS = N_HEADS // 2
CHUNK = 128
N_WIN = 128
DILATIONS = (1, 4, 16)
MAX_WINDOW = 2048
N_GROUPS = 4
EXPERTS_PER_GROUP = 8
N_EXPERTS = N_GROUPS * EXPERTS_PER_GROUP
D_EXPERT = D_MODEL // 4
EPS = 1e-6

LANES = 128
SUBLANES = 8
ROW_TILES = D_MODEL // LANES
VMEM_LIMIT = 48 * 1024 * 1024

PROJ_TILE = 512
ATTN_TILE = 512
MIX_TILE = 256
EXPERT_TILE = 256
TOKEN_TILE = 128


def _dot(a, b):
    return jnp.dot(a, b, preferred_element_type=F32)


def _gelu(x):
    return 0.5 * x * (1.0 + jnp.tanh(0.7978845608028654 * (x + 0.044715 * (x * x * x))))


def _head_rms(t, gain, bd):
    ss = _dot((t * t).astype(BF16), bd)
    return t * lax.rsqrt(ss * (1.0 / HEAD_DIM) + EPS) * gain


def _proj_common(x_ref, g1_ref, win_ref, gains_ref, bd_ref):
    x = x_ref[...]
    r = lax.rsqrt(jnp.mean(x * x, axis=-1, keepdims=True) + EPS)
    xn = (x * r * g1_ref[...]).astype(BF16)
    bd = bd_ref[...]
    q = _head_rms(_dot(xn, win_ref[:, 0:WIDTH]), gains_ref[0:1, :], bd) * (HEAD_DIM ** -0.5)
    k = _head_rms(_dot(xn, win_ref[:, WIDTH:2 * WIDTH]), gains_ref[1:2, :], bd)
    v = _dot(xn, win_ref[:, 2 * WIDTH:3 * WIDTH])
    u = _gelu(_dot(xn, win_ref[:, 3 * WIDTH:4 * WIDTH]))
    vg = _head_rms(_gelu(_dot(xn, win_ref[:, 4 * WIDTH:5 * WIDTH])), gains_ref[2:3, :], bd)
    return q, k, v, u, vg


def _proj_prompt_kernel(x_ref, g1_ref, win_ref, gains_ref, bd_ref, wcat_ref, bias_ref,
                        q_ref, k_ref, v_ref, oa_ref, kl_ref, vl_ref, *, tiles_per_seq, first_kept_tile):
    q, k, v, u, vg = _proj_common(x_ref, g1_ref, win_ref, gains_ref, bd_ref)
    q_ref[...] = q.astype(BF16)
    k_ref[...] = k.astype(BF16)
    v_ref[...] = v.astype(BF16)

    @pl.when(pl.program_id(0) % tiles_per_seq >= first_kept_tile)
    def _():
        kl_ref[...] = k.T
        vl_ref[...] = v.T

    vgb = vg.astype(BF16)
    lane = lax.broadcasted_iota(I32, (CHUNK, LANES), 1)
    tm = x_ref.shape[0]
    for c in range(tm // CHUNK):
        rows = slice(c * CHUNK, (c + 1) * CHUNK)
        for p in range(N_PAIRS):
            cols = slice(p * LANES, (p + 1) * LANES)
            vp = vgb[rows, cols]
            zero = jnp.zeros_like(vp)
            rhs = jnp.concatenate([jnp.where(lane < HEAD_DIM, vp, zero),
                                   jnp.where(lane >= HEAD_DIM, vp, zero)], axis=0)
            mixed = _dot(wcat_ref[p], rhs) + bias_ref[p]
            oa_ref[rows, cols] = (u[rows, cols] * mixed).astype(BF16)


def _proj_sample_kernel(x_ref, g1_ref, win_ref, gains_ref, bd_ref, avec_ref, bvec_ref,
                        q_ref, k_ref, v_ref, vg_ref, oa_ref):
    q, k, v, u, vg = _proj_common(x_ref, g1_ref, win_ref, gains_ref, bd_ref)
    q_ref[...] = q
    k_ref[...] = k
    v_ref[...] = v
    vg_ref[...] = vg
    oa_ref[...] = (u * (avec_ref[...] * vg + bvec_ref[...])).astype(BF16)


def _const_spec(shape):
    return pl.BlockSpec(shape, lambda *_: (0,) * len(shape))


def _proj_prompt(x2, g1, win, gains, bd, wcat, bias, *, seq):
    n = x2.shape[0]
    tm = PROJ_TILE
    tiles_per_seq = seq // tm
    kept = min(MAX_WINDOW, seq)
    first_kept_tile = tiles_per_seq - kept // tm

    def kept_map(i):
        return (i // tiles_per_seq, 0, jnp.maximum(i % tiles_per_seq - first_kept_tile, 0))

    row_spec = pl.BlockSpec((tm, WIDTH), lambda i: (i, 0))
    kept_spec = pl.BlockSpec((None, WIDTH, tm), kept_map)
    return pl.pallas_call(
        functools.partial(_proj_prompt_kernel, tiles_per_seq=tiles_per_seq, first_kept_tile=first_kept_tile),
        grid=(n // tm,),
        in_specs=[pl.BlockSpec((tm, D_MODEL), lambda i: (i, 0)),
                  _const_spec((1, D_MODEL)), _const_spec((D_MODEL, 5 * WIDTH)), _const_spec((3, WIDTH)),
                  _const_spec((WIDTH, WIDTH)), _const_spec((N_PAIRS, CHUNK, 2 * CHUNK)),
                  _const_spec((N_PAIRS, CHUNK, LANES))],
        out_specs=[row_spec, row_spec, row_spec, row_spec, kept_spec, kept_spec],
        out_shape=[jax.ShapeDtypeStruct((n, WIDTH), BF16)] * 4 +
                  [jax.ShapeDtypeStruct((n // seq, WIDTH, kept), F32)] * 2,
        compiler_params=pltpu.CompilerParams(dimension_semantics=("arbitrary",), vmem_limit_bytes=VMEM_LIMIT),
        name="proj_prompt",
    )(x2, g1, win, gains, bd, wcat, bias)


def _proj_sample(x2, g1, win, gains, bd, avec, bvec):
    n = x2.shape[0]
    full = _const_spec((n, WIDTH))
    return pl.pallas_call(
        _proj_sample_kernel,
        grid=(1,),
        in_specs=[_const_spec((n, D_MODEL)), _const_spec((1, D_MODEL)), _const_spec((D_MODEL, 5 * WIDTH)),
                  _const_spec((3, WIDTH)), _const_spec((WIDTH, WIDTH)), _const_spec((1, WIDTH)),
                  _const_spec((1, WIDTH))],
        out_specs=[full] * 5,
        out_shape=[jax.ShapeDtypeStruct((n, WIDTH), F32)] * 4 + [jax.ShapeDtypeStruct((n, WIDTH), BF16)],
        compiler_params=pltpu.CompilerParams(dimension_semantics=("arbitrary",), vmem_limit_bytes=VMEM_LIMIT),
        name="proj_sample",
    )(x2, g1, win, gains, bd, avec, bvec)


def _band_attn_kernel(q_ref, k_ref, v_ref, o_ref, lse_ref, kbuf, vbuf, *, tq):
    j = pl.program_id(2)
    nblk = tq // N_WIN

    @pl.when(j == 0)
    def _():
        kbuf[0:N_WIN, :] = jnp.zeros((N_WIN, WIDTH), BF16)
        vbuf[...] = jnp.ones(vbuf.shape, BF16)

    @pl.when(j > 0)
    def _():
        kbuf[0:N_WIN, :] = kbuf[tq:tq + N_WIN, :]
        vbuf[0:N_WIN, :] = vbuf[tq:tq + N_WIN, :]

    kbuf[N_WIN:N_WIN + tq, :] = k_ref[...]
    for p in range(N_PAIRS):
        vbuf[N_WIN:N_WIN + tq, 2 * p * LANES:(2 * p + 1) * LANES] = v_ref[:, p * LANES:(p + 1) * LANES]

    qi = lax.broadcasted_iota(I32, (N_WIN, 2 * N_WIN), 0)
    ki = lax.broadcasted_iota(I32, (N_WIN, 2 * N_WIN), 1)
    band = (ki >= qi) & (ki <= qi + N_WIN)
    lane = lax.broadcasted_iota(I32, (N_WIN, LANES), 1)
    low_half = lane < HEAD_DIM

    def block(jb, carry):
        q0 = pl.multiple_of(jb * N_WIN, N_WIN)
        valid = band & ((ki >= N_WIN) | (j * nblk + jb > 0))
        lse_tile = jnp.zeros((N_WIN, LANES), F32)
        for p in range(N_PAIRS):
            cols = slice(p * LANES, (p + 1) * LANES)
            qp = q_ref[pl.ds(q0, N_WIN), cols]
            kp = kbuf[pl.ds(q0, 2 * N_WIN), cols]
            vp = vbuf[pl.ds(q0, 2 * N_WIN), 2 * p * LANES:(2 * p + 2) * LANES]
            halves = []
            for hh in range(2):
                mine = low_half if hh == 0 else ~low_half
                qm = jnp.where(mine, qp, jnp.zeros_like(qp))
                s = lax.dot_general(qm, kp, (((1,), (1,)), ((), ())), preferred_element_type=F32)
                s = jnp.where(valid, s, -jnp.inf)
                m = jnp.max(s, axis=-1, keepdims=True)
                e = jnp.exp(s - m)
                ov = _dot(e.astype(BF16), vp)
                den = ov[:, LANES:2 * LANES]
                halves.append(ov[:, 0:LANES] / den)
                lse_tile = jnp.where(lane == 2 * p + hh, m + jnp.log(den), lse_tile)
            o_ref[pl.ds(q0, N_WIN), cols] = jnp.where(low_half, halves[0], halves[1]).astype(BF16)
        lse_ref[pl.ds(q0, N_WIN), :] = lse_tile
        return carry

    lax.fori_loop(0, nblk, block, 0)


def _band_attention(q2, k2, v2, *, batch, seq, dil):
    length = seq // dil
    tq = min(ATTN_TILE, length)
    view = lambda a: a.reshape(batch, length, dil * a.shape[-1])
    spec = pl.BlockSpec((None, tq, WIDTH), lambda b, r, j: (b, j, r))
    lse_spec = pl.BlockSpec((None, tq, LANES), lambda b, r, j: (b, j, r))
    o, lse = pl.pallas_call(
        functools.partial(_band_attn_kernel, tq=tq),
        grid=(batch, dil, length // tq),
        in_specs=[spec, spec, spec],
        out_specs=[spec, lse_spec],
        out_shape=[jax.ShapeDtypeStruct((batch, length, dil * WIDTH), BF16),
                   jax.ShapeDtypeStruct((batch, length, dil * LANES), F32)],
        scratch_shapes=[pltpu.VMEM((N_WIN + tq, WIDTH), BF16), pltpu.VMEM((N_WIN + tq, 2 * WIDTH), BF16)],
        compiler_params=pltpu.CompilerParams(dimension_semantics=("arbitrary", "arbitrary", "arbitrary"),
                                             vmem_limit_bytes=VMEM_LIMIT),
        name=f"band_attn_d{dil}",
    )(view(q2), view(k2), view(v2))
    return o.reshape(batch * seq, WIDTH), lse.reshape(batch * seq, LANES)


def _sample_attn_kernel(qt_ref, knt_ref, vnt_ref, kt_ref, vt_ref, ot_ref):
    qt = qt_ref[...]
    vnt = vnt_ref[...]
    s_new = jnp.sum(qt * knt_ref[...], axis=0, keepdims=True)
    wb = kt_ref.shape[1]
    pos = lax.broadcasted_iota(I32, (1, wb), 1)
    head_lane = lax.broadcasted_iota(I32, (HEAD_DIM, N_HEADS), 1)
    ot = jnp.zeros((HEAD_DIM, N_HEADS), F32)
    for h in range(N_HEADS):
        rows = slice(h * HEAD_DIM, (h + 1) * HEAD_DIM)
        s = jnp.sum(kt_ref[rows, :] * qt[:, h:h + 1], axis=0, keepdims=True)
        s0 = s_new[:, h:h + 1]
        vn = vnt[:, h:h + 1]
        outs, lses = [], []
        for dil in DILATIONS:
            lo = wb - N_WIN * dil
            sc = s[:, lo:]
            valid = (pos[:, lo:] & (dil - 1)) == 0
            m = jnp.maximum(jnp.max(jnp.where(valid, sc, -jnp.inf), axis=-1, keepdims=True), s0)
            e = jnp.where(valid, jnp.exp(sc - m), 0.0)
            e0 = jnp.exp(s0 - m)
            den = jnp.sum(e, axis=-1, keepdims=True) + e0
            outs.append((jnp.sum(vt_ref[rows, lo:] * e, axis=-1, keepdims=True) + e0 * vn) / den)
            lses.append(m + jnp.log(den))
        mm = jnp.maximum(jnp.maximum(lses[0], lses[1]), lses[2])
        ws = [jnp.exp(l - mm) for l in lses]
        col = (ws[0] * outs[0] + ws[1] * outs[1] + ws[2] * outs[2]) / (ws[0] + ws[1] + ws[2])
        ot = jnp.where(head_lane == h, col, ot)
    ot_ref[...] = ot


def _sample_attention(qt, knt, vnt, kt, vt):
    nb, _, wb = kt.shape
    assert wb == MAX_WINDOW, "cache window must cover every dilated key"
    tok = pl.BlockSpec((None, HEAD_DIM, N_HEADS), lambda b: (b, 0, 0))
    cache = pl.BlockSpec((None, WIDTH, wb), lambda b: (b, 0, 0))
    return pl.pallas_call(
        _sample_attn_kernel,
        grid=(nb,),
        in_specs=[tok, tok, tok, cache, cache],
        out_specs=tok,
        out_shape=jax.ShapeDtypeStruct((nb, HEAD_DIM, N_HEADS), F32),
        compiler_params=pltpu.CompilerParams(dimension_semantics=("arbitrary",), vmem_limit_bytes=VMEM_LIMIT),
        name="sample_attn",
    )(qt, knt, vnt, kt, vt)


def _split_hi_lo(w):
    hi = w.astype(BF16)
    return jnp.concatenate([hi, (w - hi.astype(F32)).astype(BF16)], axis=-1)


def _mix_kernel(*refs, n_cfg):
    n_lse = n_cfg if n_cfg > 1 else 0
    x_ref, oa_ref = refs[0], refs[1]
    o_refs = refs[2:2 + n_cfg]
    lse_refs = refs[2 + n_cfg:2 + n_cfg + n_lse]
    (ga_ref, gb_ref, wout_ref, g2_ref, wr_ref, br_ref, expand_ref, tri_ref, cnt0_ref,
     h_ref, hn_ref, route_ref, cnt_ref, carry) = refs[2 + n_cfg + n_lse:]
    tm = x_ref.shape[0]

    @pl.when(pl.program_id(0) == 0)
    def _():
        carry[...] = cnt0_ref[...]

    if n_cfg == 1:
        ob = o_refs[0][...].astype(F32)
    else:
        lses = [r[...] for r in lse_refs]
        mm = functools.reduce(jnp.maximum, lses)
        ws = [jnp.exp(l - mm) for l in lses]
        tot = functools.reduce(lambda a, b: a + b, ws)
        ob = jnp.zeros((tm, WIDTH), F32)
        for w, o_ref in zip(ws, o_refs):
            ob = ob + _dot(_split_hi_lo(w / tot), expand_ref[...]) * o_ref[...].astype(F32)

    oa = oa_ref[...].astype(F32)
    ya = oa * lax.rsqrt(jnp.mean(oa * oa, axis=-1, keepdims=True) + EPS) * ga_ref[...]
    yb = ob * lax.rsqrt(jnp.mean(ob * ob, axis=-1, keepdims=True) + EPS) * gb_ref[...]
    cat = jnp.concatenate([ya, yb], axis=-1).astype(BF16)
    h = x_ref[...] + _dot(cat, wout_ref[...])
    h_ref[...] = h
    hn = h * lax.rsqrt(jnp.mean(h * h, axis=-1, keepdims=True) + EPS) * g2_ref[...]
    for c in range(ROW_TILES):
        hn_ref[:, c, :] = hn[:, c * LANES:(c + 1) * LANES]

    logits = _dot(hn.astype(BF16), wr_ref[...]) + br_ref[...]
    lane = lax.broadcasted_iota(I32, (tm, LANES), 1)
    lane_f = lane.astype(F32)
    neg = -jnp.inf
    big = float(LANES)
    coarse = (lane >= N_EXPERTS) & (lane < N_EXPERTS + N_GROUPS)
    lg = jnp.where(coarse, logits, neg)
    mx = jnp.max(lg, axis=-1, keepdims=True)
    g_lane = jnp.min(jnp.where(lg == mx, lane_f, big), axis=-1, keepdims=True)
    p_star = 1.0 / jnp.sum(jnp.exp(lg - mx), axis=-1, keepdims=True)
    lo = (g_lane - float(N_EXPERTS)) * float(EXPERTS_PER_GROUP)
    lf = jnp.where((lane_f >= lo) & (lane_f < lo + float(EXPERTS_PER_GROUP)), logits, neg)
    v1 = jnp.max(lf, axis=-1, keepdims=True)
    i1 = jnp.min(jnp.where(lf == v1, lane_f, big), axis=-1, keepdims=True)
    lf2 = jnp.where(lane_f == i1, neg, lf)
    v2 = jnp.max(lf2, axis=-1, keepdims=True)
    i2 = jnp.min(jnp.where(lf2 == v2, lane_f, big), axis=-1, keepdims=True)
    e21 = jnp.exp(v2 - v1)
    w1 = p_star / (1.0 + e21)
    w2 = p_star * e21 / (1.0 + e21)

    sel1 = lane_f == i1
    sel2 = lane_f == i2
    onehot = jnp.where(sel1 | sel2, 1.0, 0.0)
    before = _dot(tri_ref[...], onehot.astype(BF16)) + carry[...]
    r1 = jnp.sum(jnp.where(sel1, before, 0.0), axis=-1, keepdims=True)
    r2 = jnp.sum(jnp.where(sel2, before, 0.0), axis=-1, keepdims=True)
    carry[...] = carry[...] + jnp.sum(onehot, axis=0, keepdims=True)
    cnt_ref[...] = carry[...]

    col = lax.broadcasted_iota(I32, (tm, SUBLANES), 1)
    route = jnp.zeros((tm, SUBLANES), F32)
    for idx, val in enumerate((i1, i2, r1, r2, w1, w2)):
        route = jnp.where(col == idx, val, route)
    route_ref[...] = route


def _mix(x2, oa, obs, lses, ga, gb, wout, g2, wr, br, expand, tri, cnt0):
    n = x2.shape[0]
    tm = min(MIX_TILE, n)
    n_cfg = len(obs)
    row = lambda w: pl.BlockSpec((tm, w), lambda i: (i, 0))
    in_specs = ([row(D_MODEL), row(WIDTH)] + [row(WIDTH)] * n_cfg + [row(LANES)] * len(lses) +
                [_const_spec((1, WIDTH)), _const_spec((1, WIDTH)), _const_spec((D_MODEL, D_MODEL)),
                 _const_spec((1, D_MODEL)), _const_spec((D_MODEL, LANES)), _const_spec((1, LANES)),
                 _const_spec((2 * LANES, WIDTH)), _const_spec((tm, tm)), _const_spec((1, LANES))])
    return pl.pallas_call(
        functools.partial(_mix_kernel, n_cfg=n_cfg),
        grid=(n // tm,),
        in_specs=in_specs,
        out_specs=[row(D_MODEL), pl.BlockSpec((tm, ROW_TILES, LANES), lambda i: (i, 0, 0)),
                   row(SUBLANES), _const_spec((1, LANES))],
        out_shape=[jax.ShapeDtypeStruct((n, D_MODEL), F32), jax.ShapeDtypeStruct((n, ROW_TILES, LANES), F32),
                   jax.ShapeDtypeStruct((n, SUBLANES), F32), jax.ShapeDtypeStruct((1, LANES), F32)],
        scratch_shapes=[pltpu.VMEM((1, LANES), F32)],
        compiler_params=pltpu.CompilerParams(dimension_semantics=("arbitrary",), vmem_limit_bytes=VMEM_LIMIT),
        name="mix",
    )(x2, oa, *obs, *lses, ga, gb, wout, g2, wr, br, expand, tri, cnt0)


def _slot(meta_ref, ri_ref, choice, t, tm):
    return meta_ref[ri_ref[0, 0, choice * tm + t]] + ri_ref[0, 0, (2 + choice) * tm + t]


def _dispatch_kernel(meta_ref, ri_ref, hn_a_ref, hn_b_ref, xs_hbm, zbuf, sem, *, tiles_a):
    tm = hn_a_ref.shape[0]
    n_slots = xs_hbm.shape[0]
    zrows = zbuf.shape[0]
    i = pl.program_id(0)

    def fill(off, rows):
        cp = pltpu.make_async_copy(zbuf.at[pl.ds(0, rows)], xs_hbm.at[pl.ds(off, rows)], sem)
        cp.start()
        cp.wait()

    @pl.when(i == 0)
    def _():
        zbuf[...] = jnp.zeros_like(zbuf)

        def pad(e, c):
            lo = meta_ref[e] + meta_ref[N_EXPERTS + e]
            n_pad = meta_ref[2 * N_EXPERTS + e] - meta_ref[N_EXPERTS + e]
            bit = zrows
            while bit >= 1:
                pl.when((n_pad & bit) != 0)(functools.partial(fill, lo + (n_pad & ~(2 * bit - 1)), bit))
                bit //= 2
            return c

        lax.fori_loop(0, N_EXPERTS, pad, 0)

        def tail(j, c):
            fill(meta_ref[3 * N_EXPERTS] + j * zrows, zrows)
            return c

        lax.fori_loop(0, (n_slots - meta_ref[3 * N_EXPERTS]) // zrows, tail, 0)

    def scatter(hn_ref):
        def row_copy(t, slot):
            return pltpu.make_async_copy(hn_ref.at[t], xs_hbm.at[slot], sem)

        def start(t, c):
            row_copy(t, _slot(meta_ref, ri_ref, 0, t, tm)).start()
            row_copy(t, _slot(meta_ref, ri_ref, 1, t, tm)).start()
            return c

        lax.fori_loop(0, tm, start, 0)

        def wait(t, c):
            row_copy(0, 0).wait()
            row_copy(0, 0).wait()
            return c

        lax.fori_loop(0, tm, wait, 0)

    pl.when(i < tiles_a)(functools.partial(scatter, hn_a_ref))
    pl.when(i >= tiles_a)(functools.partial(scatter, hn_b_ref))


def _dispatch(meta, ri, hn_a, hn_b, n_slots):
    tm = TOKEN_TILE
    tiles_a, tiles_b = hn_a.shape[0] // tm, hn_b.shape[0] // tm
    rows = lambda f: pl.BlockSpec((tm, ROW_TILES, LANES), f)
    return pl.pallas_call(
        functools.partial(_dispatch_kernel, tiles_a=tiles_a),
        grid_spec=pltpu.PrefetchScalarGridSpec(
            num_scalar_prefetch=1, grid=(tiles_a + tiles_b,),
            in_specs=[pl.BlockSpec((1, 1, 4 * tm), lambda i, m: (i, 0, 0), memory_space=pltpu.SMEM),
                      rows(lambda i, m: (jnp.minimum(i, tiles_a - 1), 0, 0)),
                      rows(lambda i, m: (jnp.maximum(i - tiles_a, 0), 0, 0))],
            out_specs=pl.BlockSpec(memory_space=pl.ANY),
            scratch_shapes=[pltpu.VMEM((EXPERT_TILE // 2, ROW_TILES, LANES), F32), pltpu.SemaphoreType.DMA]),
        out_shape=jax.ShapeDtypeStruct((n_slots, ROW_TILES, LANES), F32),
        compiler_params=pltpu.CompilerParams(dimension_semantics=("arbitrary",), vmem_limit_bytes=VMEM_LIMIT),
        name="dispatch",
    )(meta, ri, hn_a, hn_b)


def _expert_kernel(te_ref, nv_ref, xs_ref, wg_ref, wu_ref, wd_ref, out_ref):
    @pl.when(pl.program_id(0) < nv_ref[0])
    def _():
        x = jnp.concatenate([xs_ref[:, c, :] for c in range(ROW_TILES)], axis=1).astype(BF16)
        a = _dot(x, wg_ref[...])
        b = _dot(x, wu_ref[...])
        hid = (a * jax.nn.sigmoid(a)) * b
        out = _dot(hid.astype(BF16), wd_ref[...])
        for c in range(ROW_TILES):
            out_ref[:, c, :] = out[:, c * LANES:(c + 1) * LANES]

    @pl.when(pl.program_id(0) >= nv_ref[0])
    def _():
        out_ref[...] = jnp.zeros_like(out_ref)


def _experts(tile_expert, n_valid, xs, wg, wu, wd):
    tr = EXPERT_TILE
    n_tiles = xs.shape[0] // tr
    grid_spec = pltpu.PrefetchScalarGridSpec(
        num_scalar_prefetch=2,
        grid=(n_tiles,),
        in_specs=[pl.BlockSpec((tr, ROW_TILES, LANES), lambda i, te, nv: (jnp.minimum(i, nv[0] - 1), 0, 0)),
                  pl.BlockSpec((None, D_MODEL, D_EXPERT), lambda i, te, nv: (te[i], 0, 0)),
                  pl.BlockSpec((None, D_MODEL, D_EXPERT), lambda i, te, nv: (te[i], 0, 0)),
                  pl.BlockSpec((None, D_EXPERT, D_MODEL), lambda i, te, nv: (te[i], 0, 0))],
        out_specs=pl.BlockSpec((tr, ROW_TILES, LANES), lambda i, te, nv: (i, 0, 0)),
    )
    return pl.pallas_call(
        _expert_kernel,
        grid_spec=grid_spec,
        out_shape=jax.ShapeDtypeStruct((n_tiles * tr, ROW_TILES, LANES), F32),
        compiler_params=pltpu.CompilerParams(dimension_semantics=("arbitrary",), vmem_limit_bytes=VMEM_LIMIT),
        name="experts",
    )(tile_expert, n_valid, xs, wg, wu, wd)


def _combine_kernel(meta_ref, ri_ref, h_ref, route_ref, out_hbm, y_ref, gbuf, sem):
    tm = h_ref.shape[0]

    def row_copy(slot, r):
        return pltpu.make_async_copy(out_hbm.at[slot], gbuf.at[pl.ds(r * ROW_TILES, ROW_TILES), :], sem)

    def start(t, c):
        row_copy(_slot(meta_ref, ri_ref, 0, t, tm), t).start()
        row_copy(_slot(meta_ref, ri_ref, 1, t, tm), tm + t).start()
        return c

    lax.fori_loop(0, tm, start, 0)

    def wait(t, c):
        row_copy(0, 0).wait()
        row_copy(0, 0).wait()
        return c

    lax.fori_loop(0, tm, wait, 0)
    g = jnp.concatenate([gbuf[pl.ds(c, 2 * tm, stride=ROW_TILES), :] for c in range(ROW_TILES)], axis=1)
    route = route_ref[...]
    y_ref[...] = h_ref[...] + route[:, 4:5] * g[0:tm] + route[:, 5:6] * g[tm:2 * tm]


def _combine(meta, ri, h, route, out3):
    n = h.shape[0]
    tm = min(TOKEN_TILE, n)
    grid_spec = pltpu.PrefetchScalarGridSpec(
        num_scalar_prefetch=1,
        grid=(n // tm,),
        in_specs=[pl.BlockSpec((1, 1, 4 * tm), lambda i, m: (i, 0, 0), memory_space=pltpu.SMEM),
                  pl.BlockSpec((tm, D_MODEL), lambda i, m: (i, 0)),
                  pl.BlockSpec((tm, SUBLANES), lambda i, m: (i, 0)),
                  pl.BlockSpec(memory_space=pl.ANY)],
        out_specs=pl.BlockSpec((tm, D_MODEL), lambda i, m: (i, 0)),
        scratch_shapes=[pltpu.VMEM((2 * tm * ROW_TILES, LANES), F32), pltpu.SemaphoreType.DMA],
    )
    return pl.pallas_call(
        _combine_kernel,
        grid_spec=grid_spec,
        out_shape=jax.ShapeDtypeStruct((n, D_MODEL), F32),
        compiler_params=pltpu.CompilerParams(dimension_semantics=("arbitrary",), vmem_limit_bytes=VMEM_LIMIT),
        name="combine",
    )(meta, ri, h, route, out3)


def _route_ints(route):
    n = route.shape[0]
    tm = min(TOKEN_TILE, n)
    ri = route[:, 0:4].astype(I32).reshape(n // tm, tm, 4)
    return jnp.transpose(ri, (0, 2, 1)).reshape(n // tm, 1, 4 * tm)


def _moe(part_a, part_b, counts, wg, wu, wd):
    tr = EXPERT_TILE
    n_tokens = part_a[0].shape[0] + part_b[0].shape[0]
    n_tiles = pl.cdiv(2 * n_tokens, tr) + N_EXPERTS
    cnt = counts[0, :N_EXPERTS].astype(I32)
    padded = ((cnt + tr - 1) // tr) * tr
    ends = jnp.cumsum(padded)
    meta = jnp.concatenate([ends - padded, cnt, padded, ends[N_EXPERTS - 1:]])
    tile_start = jnp.arange(n_tiles, dtype=I32) * tr
    tile_expert = jnp.minimum(jnp.sum((tile_start[:, None] >= ends[None, :]).astype(I32), axis=1), N_EXPERTS - 1)
    n_valid = (ends[N_EXPERTS - 1] // tr).reshape(1)
    ri_a, ri_b = _route_ints(part_a[2]), _route_ints(part_b[2])
    xs = _dispatch(meta, jnp.concatenate([ri_a, ri_b]), part_a[1], part_b[1], n_tiles * tr)
    out3 = _experts(tile_expert, n_valid, xs, wg, wu, wd)
    return [_combine(meta, ri, h, route, out3) for (h, _, route), ri in ((part_a, ri_a), (part_b, ri_b))]


def kernel(x_prompt, x_sample, cache_k, cache_v, norm1_g, w_in, q_gain, k_gain, v_gain, w_spatial, b_spatial,
           out_gain_a, out_gain_b, w_out, norm2_g, w_router1, b_router1, w_router2, b_router2, w_up, w_gate,
           w_down):
    depth = norm1_g.shape[0]
    assert depth == 1
    l = 0
    batch, seq, _ = x_prompt.shape
    nb, dec_seq, _ = x_sample.shape
    assert dec_seq == 1

    g1 = norm1_g[l].reshape(1, D_MODEL)
    win = w_in[l].astype(BF16)
    gains = jnp.stack([q_gain[l].reshape(WIDTH), k_gain[l].reshape(WIDTH), v_gain[l].reshape(WIDTH)])
    head_of_lane = jnp.arange(WIDTH) // HEAD_DIM
    bd = (head_of_lane[:, None] == head_of_lane[None, :]).astype(BF16)
    ws_tril = jnp.tril(w_spatial[l])
    wcat = jnp.concatenate([ws_tril[0::2], ws_tril[1::2]], axis=-1).astype(BF16)
    bs = b_spatial[l]
    bias = jnp.where(jnp.arange(LANES)[None, None, :] < HEAD_DIM, bs[0::2][:, :, None], bs[1::2][:, :, None])
    avec = jnp.repeat(w_spatial[l][:, 0, 0], HEAD_DIM).reshape(1, WIDTH)
    bvec = jnp.repeat(bs[:, 0], HEAD_DIM).reshape(1, WIDTH)
    ga = out_gain_a[l].reshape(1, WIDTH)
    gb = out_gain_b[l].reshape(1, WIDTH)
    wout = w_out[l].astype(BF16)
    g2 = norm2_g[l].reshape(1, D_MODEL)
    wr = jnp.zeros((D_MODEL, LANES), F32)
    wr = wr.at[:, :N_EXPERTS].set(jnp.transpose(w_router2[l], (1, 0, 2)).reshape(D_MODEL, N_EXPERTS))
    wr = wr.at[:, N_EXPERTS:N_EXPERTS + N_GROUPS].set(w_router1[l]).astype(BF16)
    br = jnp.zeros((1, LANES), F32)
    br = br.at[0, :N_EXPERTS].set(b_router2[l].reshape(N_EXPERTS))
    br = br.at[0, N_EXPERTS:N_EXPERTS + N_GROUPS].set(b_router1[l])
    lane_head = (jnp.arange(LANES)[:, None] == head_of_lane[None, :]).astype(BF16)
    expand = jnp.concatenate([lane_head, lane_head], axis=0)
    wg = w_gate[l].reshape(N_EXPERTS, D_MODEL, D_EXPERT).astype(BF16)
    wu = w_up[l].reshape(N_EXPERTS, D_MODEL, D_EXPERT).astype(BF16)
    wd = w_down[l].reshape(N_EXPERTS, D_EXPERT, D_MODEL).astype(BF16)

    def tri(t):
        return (jnp.arange(t)[:, None] > jnp.arange(t)[None, :]).astype(BF16)

    xp = x_prompt.reshape(batch * seq, D_MODEL)
    q, k, v, oa, kt_last, vt_last = _proj_prompt(xp, g1, win, gains, bd, wcat, bias, seq=seq)
    obs, lses = [], []
    for dil in DILATIONS:
        o, lse = _band_attention(q, k, v, batch=batch, seq=seq, dil=dil)
        obs.append(o)
        lses.append(lse)
    tmix = min(MIX_TILE, batch * seq)
    h, hn3, route, counts = _mix(xp, oa, obs, lses, ga, gb, wout, g2, wr, br, expand, tri(tmix),
                                 jnp.zeros((1, LANES), F32))
    kept = min(MAX_WINDOW, seq)
    to_cache = lambda t: jnp.transpose(t.reshape(1, batch, N_HEADS, HEAD_DIM, kept), (0, 1, 4, 2, 3))

    xs = x_sample.reshape(nb, D_MODEL)
    qs, ks, vs, vgs, oas = _proj_sample(xs, g1, win, gains, bd, avec, bvec)
    cols = lambda t: jnp.transpose(t.reshape(nb, N_HEADS, HEAD_DIM), (0, 2, 1))
    feature_major = lambda c: jnp.transpose(c, (0, 2, 3, 1)).reshape(nb, WIDTH, c.shape[1])
    ot = _sample_attention(cols(qs), cols(ks), cols(vs), feature_major(cache_k[l]), feature_major(cache_v[l]))
    obs_s = jnp.transpose(ot, (0, 2, 1)).reshape(nb, WIDTH)
    tmix_s = min(MIX_TILE, nb)
    hs, hn3s, route_s, counts_all = _mix(xs, oas, [obs_s], [], ga, gb, wout, g2, wr, br, expand, tri(tmix_s),
                                         counts)

    y_prompt, y_sample = _moe((h, hn3, route), (hs, hn3s, route_s), counts_all, wg, wu, wd)
    to5 = lambda t: t.reshape(1, nb, 1, N_HEADS, HEAD_DIM)
    return (y_prompt.reshape(batch, seq, D_MODEL), y_sample.reshape(nb, 1, D_MODEL),
            to_cache(kt_last), to_cache(vt_last), to5(ks), to5(vs), to5(vgs))
```

```python
import functools

import jax
import jax.numpy as jnp
from jax import lax
from jax.experimental import pallas as pl
from jax.experimental.pallas import tpu as pltpu

F32 = jnp.float32
BF16 = jnp.bfloat16
I32 = jnp.int32

D_MODEL = 1024
HEAD_DIM = 64
N_HEADS = 8
WIDTH = N_HEADS * HEAD_DIM
N_PAIRS = N_HEADS // 2
CHUNK = 128
N_WIN = 128
DILATIONS = (1, 4, 16)
MAX_WINDOW = 2048
N_GROUPS = 4
EXPERTS_PER_GROUP = 8
N_EXPERTS = N_GROUPS * EXPERTS_PER_GROUP
D_EXPERT = D_MODEL // 4
EPS = 1e-6

LANES = 128
SUBLANES = 8
VMEM_LIMIT = 48 * 1024 * 1024

PROJ_TILE = 512
ATTN_TILE = 512
MIX_TILE = 256
EXPERT_TILE = 256
TOKEN_TILE = 128
COMBINE_TILE = 256
ISSUE_UNROLL = 4


def _dot(a, b):
    return jnp.dot(a, b, preferred_element_type=F32)


def _gelu(x):
    return 0.5 * x * (1.0 + jnp.tanh(0.7978845608028654 * (x + 0.044715 * (x * x * x))))


def _head_rms(t, gain, bd):
    ss = _dot((t * t).astype(BF16), bd)
    return t * lax.rsqrt(ss * (1.0 / HEAD_DIM) + EPS) * gain


def _proj_common(x_ref, g1_ref, win_ref, gains_ref, bd_ref):
    x = x_ref[...]
    r = lax.rsqrt(jnp.mean(x * x, axis=-1, keepdims=True) + EPS)
    xn = (x * r * g1_ref[...]).astype(BF16)
    bd = bd_ref[...]
    q = _head_rms(_dot(xn, win_ref[:, 0:WIDTH]), gains_ref[0:1, :], bd) * (HEAD_DIM ** -0.5)
    k = _head_rms(_dot(xn, win_ref[:, WIDTH:2 * WIDTH]), gains_ref[1:2, :], bd)
    v = _dot(xn, win_ref[:, 2 * WIDTH:3 * WIDTH])
    u = _gelu(_dot(xn, win_ref[:, 3 * WIDTH:4 * WIDTH]))
    vg = _head_rms(_gelu(_dot(xn, win_ref[:, 4 * WIDTH:5 * WIDTH])), gains_ref[2:3, :], bd)
    return q, k, v, u, vg


def _proj_prompt_kernel(x_ref, g1_ref, win_ref, gains_ref, bd_ref, wcat_ref, bias_ref,
                        q_ref, k_ref, v_ref, oa_ref, kl_ref, vl_ref, *, tiles_per_seq, first_kept_tile):
    q, k, v, u, vg = _proj_common(x_ref, g1_ref, win_ref, gains_ref, bd_ref)
    q_ref[...] = q.astype(BF16)
    k_ref[...] = k.astype(BF16)
    v_ref[...] = v.astype(BF16)

    @pl.when(pl.program_id(0) % tiles_per_seq >= first_kept_tile)
    def _():
        kl_ref[...] = k.T
        vl_ref[...] = v.T

    vgb = vg.astype(BF16)
    lane = lax.broadcasted_iota(I32, (CHUNK, LANES), 1)
    tm = x_ref.shape[0]
    for c in range(tm // CHUNK):
        rows = slice(c * CHUNK, (c + 1) * CHUNK)
        for p in range(N_PAIRS):
            cols = slice(p * LANES, (p + 1) * LANES)
            vp = vgb[rows, cols]
            zero = jnp.zeros_like(vp)
            rhs = jnp.concatenate([jnp.where(lane < HEAD_DIM, vp, zero),
                                   jnp.where(lane >= HEAD_DIM, vp, zero)], axis=0)
            mixed = _dot(wcat_ref[p], rhs) + bias_ref[p]
            oa_ref[rows, cols] = (u[rows, cols] * mixed).astype(BF16)


def _proj_sample_kernel(x_ref, g1_ref, win_ref, gains_ref, bd_ref, avec_ref, bvec_ref,
                        q_ref, k_ref, v_ref, vg_ref, oa_ref):
    q, k, v, u, vg = _proj_common(x_ref, g1_ref, win_ref, gains_ref, bd_ref)
    q_ref[...] = q
    k_ref[...] = k
    v_ref[...] = v
    vg_ref[...] = vg
    oa_ref[...] = (u * (avec_ref[...] * vg + bvec_ref[...])).astype(BF16)


def _const_spec(shape):
    return pl.BlockSpec(shape, lambda *_: (0,) * len(shape))


def _proj_prompt(x2, g1, win, gains, bd, wcat, bias, *, seq):
    n = x2.shape[0]
    tm = PROJ_TILE
    tiles_per_seq = seq // tm
    kept = min(MAX_WINDOW, seq)
    first_kept_tile = tiles_per_seq - kept // tm

    def kept_map(i):
        return (i // tiles_per_seq, 0, jnp.maximum(i % tiles_per_seq - first_kept_tile, 0))

    row_spec = pl.BlockSpec((tm, WIDTH), lambda i: (i, 0))
    kept_spec = pl.BlockSpec((None, WIDTH, tm), kept_map)
    return pl.pallas_call(
        functools.partial(_proj_prompt_kernel, tiles_per_seq=tiles_per_seq, first_kept_tile=first_kept_tile),
        grid=(n // tm,),
        in_specs=[pl.BlockSpec((tm, D_MODEL), lambda i: (i, 0)),
                  _const_spec((1, D_MODEL)), _const_spec((D_MODEL, 5 * WIDTH)), _const_spec((3, WIDTH)),
                  _const_spec((WIDTH, WIDTH)), _const_spec((N_PAIRS, CHUNK, 2 * CHUNK)),
                  _const_spec((N_PAIRS, CHUNK, LANES))],
        out_specs=[row_spec, row_spec, row_spec, row_spec, kept_spec, kept_spec],
        out_shape=[jax.ShapeDtypeStruct((n, WIDTH), BF16)] * 4 +
                  [jax.ShapeDtypeStruct((n // seq, WIDTH, kept), F32)] * 2,
        compiler_params=pltpu.CompilerParams(dimension_semantics=("arbitrary",), vmem_limit_bytes=VMEM_LIMIT),
        name="proj_prompt",
    )(x2, g1, win, gains, bd, wcat, bias)


def _proj_sample(x2, g1, win, gains, bd, avec, bvec):
    n = x2.shape[0]
    full = _const_spec((n, WIDTH))
    return pl.pallas_call(
        _proj_sample_kernel,
        grid=(1,),
        in_specs=[_const_spec((n, D_MODEL)), _const_spec((1, D_MODEL)), _const_spec((D_MODEL, 5 * WIDTH)),
                  _const_spec((3, WIDTH)), _const_spec((WIDTH, WIDTH)), _const_spec((1, WIDTH)),
                  _const_spec((1, WIDTH))],
        out_specs=[full] * 5,
        out_shape=[jax.ShapeDtypeStruct((n, WIDTH), F32)] * 4 + [jax.ShapeDtypeStruct((n, WIDTH), BF16)],
        compiler_params=pltpu.CompilerParams(dimension_semantics=("arbitrary",), vmem_limit_bytes=VMEM_LIMIT),
        name="proj_sample",
    )(x2, g1, win, gains, bd, avec, bvec)


def _band_attn_kernel(q_ref, k_ref, v_ref, o_ref, lse_ref, kbuf, vbuf, *, tq):
    j = pl.program_id(2)
    nblk = tq // N_WIN

    @pl.when(j == 0)
    def _():
        kbuf[0:N_WIN, :] = jnp.zeros((N_WIN, WIDTH), BF16)
        vbuf[...] = jnp.ones(vbuf.shape, BF16)

    @pl.when(j > 0)
    def _():
        kbuf[0:N_WIN, :] = kbuf[tq:tq + N_WIN, :]
        vbuf[0:N_WIN, :] = vbuf[tq:tq + N_WIN, :]

    kbuf[N_WIN:N_WIN + tq, :] = k_ref[...]
    for p in range(N_PAIRS):
        vbuf[N_WIN:N_WIN + tq, 2 * p * LANES:(2 * p + 1) * LANES] = v_ref[:, p * LANES:(p + 1) * LANES]

    qi = lax.broadcasted_iota(I32, (N_WIN, 2 * N_WIN), 0)
    ki = lax.broadcasted_iota(I32, (N_WIN, 2 * N_WIN), 1)
    band = (ki >= qi) & (ki <= qi + N_WIN)
    lane = lax.broadcasted_iota(I32, (N_WIN, LANES), 1)
    low_half = lane < HEAD_DIM

    def block(jb, carry):
        q0 = pl.multiple_of(jb * N_WIN, N_WIN)
        valid = band & ((ki >= N_WIN) | (j * nblk + jb > 0))
        lse_tile = jnp.zeros((N_WIN, LANES), F32)
        for p in range(N_PAIRS):
            cols = slice(p * LANES, (p + 1) * LANES)
            qp = q_ref[pl.ds(q0, N_WIN), cols]
            kp = kbuf[pl.ds(q0, 2 * N_WIN), cols]
            vp = vbuf[pl.ds(q0, 2 * N_WIN), 2 * p * LANES:(2 * p + 2) * LANES]
            halves = []
            for hh in range(2):
                mine = low_half if hh == 0 else ~low_half
                qm = jnp.where(mine, qp, jnp.zeros_like(qp))
                s = lax.dot_general(qm, kp, (((1,), (1,)), ((), ())), preferred_element_type=F32)
                s = jnp.where(valid, s, -jnp.inf)
                m = jnp.max(s, axis=-1, keepdims=True)
                e = jnp.exp(s - m)
                ov = _dot(e.astype(BF16), vp)
                den = ov[:, LANES:2 * LANES]
                halves.append(ov[:, 0:LANES] / den)
                lse_tile = jnp.where(lane == 2 * p + hh, m + jnp.log(den), lse_tile)
            o_ref[pl.ds(q0, N_WIN), cols] = jnp.where(low_half, halves[0], halves[1]).astype(BF16)
        lse_ref[pl.ds(q0, N_WIN), :] = lse_tile
        return carry

    lax.fori_loop(0, nblk, block, 0)


def _band_attention(q2, k2, v2, *, batch, seq, dil):
    length = seq // dil
    tq = min(ATTN_TILE, length)
    view = lambda a: a.reshape(batch, length, dil * a.shape[-1])
    spec = pl.BlockSpec((None, tq, WIDTH), lambda b, r, j: (b, j, r))
    lse_spec = pl.BlockSpec((None, tq, LANES), lambda b, r, j: (b, j, r))
    o, lse = pl.pallas_call(
        functools.partial(_band_attn_kernel, tq=tq),
        grid=(batch, dil, length // tq),
        in_specs=[spec, spec, spec],
        out_specs=[spec, lse_spec],
        out_shape=[jax.ShapeDtypeStruct((batch, length, dil * WIDTH), BF16),
                   jax.ShapeDtypeStruct((batch, length, dil * LANES), F32)],
        scratch_shapes=[pltpu.VMEM((N_WIN + tq, WIDTH), BF16), pltpu.VMEM((N_WIN + tq, 2 * WIDTH), BF16)],
        compiler_params=pltpu.CompilerParams(dimension_semantics=("arbitrary", "arbitrary", "arbitrary"),
                                             vmem_limit_bytes=VMEM_LIMIT),
        name=f"band_attn_d{dil}",
    )(view(q2), view(k2), view(v2))
    return o.reshape(batch * seq, WIDTH), lse.reshape(batch * seq, LANES)


def _sample_attn_kernel(qt_ref, knt_ref, vnt_ref, kt_ref, vt_ref, ot_ref):
    qt = qt_ref[...]
    vnt = vnt_ref[...]
    s_new = jnp.sum(qt * knt_ref[...], axis=0, keepdims=True)
    wb = kt_ref.shape[1]
    heads = range(N_HEADS)
    head_rows = lambda h: slice(h * HEAD_DIM, (h + 1) * HEAD_DIM)
    s = jnp.concatenate([jnp.sum(kt_ref[head_rows(h), :] * qt[:, h:h + 1], axis=0, keepdims=True) for h in heads],
                        axis=0)
    s0 = jnp.concatenate([s_new[:, h:h + 1] for h in heads], axis=0)
    pos = lax.broadcasted_iota(I32, (N_HEADS, wb), 1)
    es, e0s, dens, lses = [], [], [], []
    for dil in DILATIONS:
        lo = wb - N_WIN * dil
        sc = s[:, lo:]
        valid = (pos[:, lo:] & (dil - 1)) == 0
        m = jnp.maximum(jnp.max(jnp.where(valid, sc, -jnp.inf), axis=-1, keepdims=True), s0)
        e = jnp.where(valid, jnp.exp(sc - m), 0.0)
        e0 = jnp.exp(s0 - m)
        den = jnp.sum(e, axis=-1, keepdims=True) + e0
        es.append(e)
        e0s.append(e0)
        dens.append(den)
        lses.append(m + jnp.log(den))
    mm = jnp.maximum(jnp.maximum(lses[0], lses[1]), lses[2])
    ws = [jnp.exp(l - mm) for l in lses]
    tot = ws[0] + ws[1] + ws[2]
    coef = [w / (tot * den) for w, den in zip(ws, dens)]
    w1, w4, w16 = [e * c for e, c in zip(es, coef)]
    w_new = coef[0] * e0s[0] + coef[1] * e0s[1] + coef[2] * e0s[2]
    n1, n4 = N_WIN * DILATIONS[0], N_WIN * DILATIONS[1]
    w_pos = jnp.concatenate([w16[:, :wb - n4], w16[:, wb - n4:wb - n1] + w4[:, :n4 - n1],
                             w16[:, wb - n1:] + w4[:, n4 - n1:] + w1], axis=1)
    head_lane = lax.broadcasted_iota(I32, (HEAD_DIM, N_HEADS), 1)
    ot = jnp.zeros((HEAD_DIM, N_HEADS), F32)
    for h in heads:
        col = (jnp.sum(vt_ref[head_rows(h), :] * w_pos[h:h + 1, :], axis=-1, keepdims=True) +
               w_new[h:h + 1, :] * vnt[:, h:h + 1])
        ot = jnp.where(head_lane == h, col, ot)
    ot_ref[...] = ot


def _sample_attention(qt, knt, vnt, kt, vt):
    nb, _, wb = kt.shape
    assert wb == MAX_WINDOW, "cache window must cover every dilated key"
    tok = pl.BlockSpec((None, HEAD_DIM, N_HEADS), lambda b: (b, 0, 0))
    cache = pl.BlockSpec((None, WIDTH, wb), lambda b: (b, 0, 0))
    return pl.pallas_call(
        _sample_attn_kernel,
        grid=(nb,),
        in_specs=[tok, tok, tok, cache, cache],
        out_specs=tok,
        out_shape=jax.ShapeDtypeStruct((nb, HEAD_DIM, N_HEADS), F32),
        compiler_params=pltpu.CompilerParams(dimension_semantics=("arbitrary",), vmem_limit_bytes=VMEM_LIMIT),
        name="sample_attn",
    )(qt, knt, vnt, kt, vt)


def _split_hi_lo(w):
    hi = w.astype(BF16)
    return jnp.concatenate([hi, (w - hi.astype(F32)).astype(BF16)], axis=-1)


def _mix_kernel(*refs, n_cfg):
    n_lse = n_cfg if n_cfg > 1 else 0
    x_ref, oa_ref = refs[0], refs[1]
    o_refs = refs[2:2 + n_cfg]
    lse_refs = refs[2 + n_cfg:2 + n_cfg + n_lse]
    (ga_ref, gb_ref, wout_ref, g2_ref, wr_ref, br_ref, expand_ref, tri_ref, cnt0_ref,
     h_ref, hn_ref, route_ref, cnt_ref, carry) = refs[2 + n_cfg + n_lse:]
    tm = x_ref.shape[0]

    @pl.when(pl.program_id(0) == 0)
    def _():
        carry[...] = cnt0_ref[...]

    if n_cfg == 1:
        ob = o_refs[0][...].astype(F32)
    else:
        lses = [r[...] for r in lse_refs]
        mm = functools.reduce(jnp.maximum, lses)
        ws = [jnp.exp(l - mm) for l in lses]
        tot = functools.reduce(lambda a, b: a + b, ws)
        ob = jnp.zeros((tm, WIDTH), F32)
        for w, o_ref in zip(ws, o_refs):
            ob = ob + _dot(_split_hi_lo(w / tot), expand_ref[...]) * o_ref[...].astype(F32)

    oa = oa_ref[...].astype(F32)
    ya = oa * lax.rsqrt(jnp.mean(oa * oa, axis=-1, keepdims=True) + EPS) * ga_ref[...]
    yb = ob * lax.rsqrt(jnp.mean(ob * ob, axis=-1, keepdims=True) + EPS) * gb_ref[...]
    cat = jnp.concatenate([ya, yb], axis=-1).astype(BF16)
    h = x_ref[...] + _dot(cat, wout_ref[...])
    h_ref[...] = h
    hn = h * lax.rsqrt(jnp.mean(h * h, axis=-1, keepdims=True) + EPS) * g2_ref[...]
    hn_ref[...] = hn

    logits = _dot(hn.astype(BF16), wr_ref[...]) + br_ref[...]
    lane = lax.broadcasted_iota(I32, (tm, LANES), 1)
    lane_f = lane.astype(F32)
    neg = -jnp.inf
    big = float(LANES)
    coarse = (lane >= N_EXPERTS) & (lane < N_EXPERTS + N_GROUPS)
    lg = jnp.where(coarse, logits, neg)
    mx = jnp.max(lg, axis=-1, keepdims=True)
    g_lane = jnp.min(jnp.where(lg == mx, lane_f, big), axis=-1, keepdims=True)
    p_star = 1.0 / jnp.sum(jnp.exp(lg - mx), axis=-1, keepdims=True)
    lo = (g_lane - float(N_EXPERTS)) * float(EXPERTS_PER_GROUP)
    lf = jnp.where((lane_f >= lo) & (lane_f < lo + float(EXPERTS_PER_GROUP)), logits, neg)
    v1 = jnp.max(lf, axis=-1, keepdims=True)
    i1 = jnp.min(jnp.where(lf == v1, lane_f, big), axis=-1, keepdims=True)
    lf2 = jnp.where(lane_f == i1, neg, lf)
    v2 = jnp.max(lf2, axis=-1, keepdims=True)
    i2 = jnp.min(jnp.where(lf2 == v2, lane_f, big), axis=-1, keepdims=True)
    e21 = jnp.exp(v2 - v1)
    w1 = p_star / (1.0 + e21)
    w2 = p_star * e21 / (1.0 + e21)

    sel1 = lane_f == i1
    sel2 = lane_f == i2
    onehot = jnp.where(sel1 | sel2, 1.0, 0.0)
    before = _dot(tri_ref[...], onehot.astype(BF16)) + carry[...]
    r1 = jnp.sum(jnp.where(sel1, before, 0.0), axis=-1, keepdims=True)
    r2 = jnp.sum(jnp.where(sel2, before, 0.0), axis=-1, keepdims=True)
    carry[...] = carry[...] + jnp.sum(onehot, axis=0, keepdims=True)
    cnt_ref[...] = carry[...]

    col = lax.broadcasted_iota(I32, (tm, SUBLANES), 1)
    route = jnp.zeros((tm, SUBLANES), F32)
    for idx, val in enumerate((i1, i2, r1, r2, w1, w2)):
        route = jnp.where(col == idx, val, route)
    route_ref[...] = route


def _mix(x2, oa, obs, lses, ga, gb, wout, g2, wr, br, expand, tri, cnt0):
    n = x2.shape[0]
    tm = min(MIX_TILE, n)
    n_cfg = len(obs)
    row = lambda w: pl.BlockSpec((tm, w), lambda i: (i, 0))
    in_specs = ([row(D_MODEL), row(WIDTH)] + [row(WIDTH)] * n_cfg + [row(LANES)] * len(lses) +
                [_const_spec((1, WIDTH)), _const_spec((1, WIDTH)), _const_spec((D_MODEL, D_MODEL)),
                 _const_spec((1, D_MODEL)), _const_spec((D_MODEL, LANES)), _const_spec((1, LANES)),
                 _const_spec((2 * LANES, WIDTH)), _const_spec((tm, tm)), _const_spec((1, LANES))])
    return pl.pallas_call(
        functools.partial(_mix_kernel, n_cfg=n_cfg),
        grid=(n // tm,),
        in_specs=in_specs,
        out_specs=[row(D_MODEL), row(D_MODEL), row(SUBLANES), _const_spec((1, LANES))],
        out_shape=[jax.ShapeDtypeStruct((n, D_MODEL), F32), jax.ShapeDtypeStruct((n, D_MODEL), F32),
                   jax.ShapeDtypeStruct((n, SUBLANES), F32), jax.ShapeDtypeStruct((1, LANES), F32)],
        scratch_shapes=[pltpu.VMEM((1, LANES), F32)],
        compiler_params=pltpu.CompilerParams(dimension_semantics=("arbitrary",), vmem_limit_bytes=VMEM_LIMIT),
        name="mix",
    )(x2, oa, *obs, *lses, ga, gb, wout, g2, wr, br, expand, tri, cnt0)


def _slot(meta_ref, ri_ref, choice, t, tm):
    return meta_ref[ri_ref[0, 0, choice * tm + t]] + ri_ref[0, 0, (2 + choice) * tm + t]


def _dispatch_kernel(meta_ref, ri_ref, hn_a_ref, hn_b_ref, xs_hbm, zbuf, sem, *, tiles_a):
    tm = hn_a_ref.shape[0]
    n_slots = xs_hbm.shape[0]
    zrows = zbuf.shape[0]
    i = pl.program_id(0)

    def fill(off, rows):
        cp = pltpu.make_async_copy(zbuf.at[pl.ds(0, rows), :], xs_hbm.at[pl.ds(off, rows), :], sem)
        cp.start()
        cp.wait()

    @pl.when(i == 0)
    def _():
        zbuf[...] = jnp.zeros_like(zbuf)

        def pad(e, c):
            lo = meta_ref[e] + meta_ref[N_EXPERTS + e]
            n_pad = meta_ref[2 * N_EXPERTS + e] - meta_ref[N_EXPERTS + e]
            n_single = (-lo) & (SUBLANES - 1)
            for k in range(SUBLANES - 1):
                pl.when(k < n_single)(functools.partial(fill, lo + k, 1))
            lo8 = lo + n_single
            n8 = n_pad - n_single
            bit = zrows
            while bit >= SUBLANES:
                off = pl.multiple_of(lo8 + (n8 & ~(2 * bit - 1)), SUBLANES)
                pl.when((n8 & bit) != 0)(functools.partial(fill, off, bit))
                bit //= 2
            return c

        lax.fori_loop(0, N_EXPERTS, pad, 0)

        def tail(j, c):
            fill(pl.multiple_of(meta_ref[3 * N_EXPERTS] + j * zrows, zrows), zrows)
            return c

        lax.fori_loop(0, (n_slots - meta_ref[3 * N_EXPERTS]) // zrows, tail, 0)

    def scatter(hn_ref):
        def row_copy(t, slot):
            return pltpu.make_async_copy(hn_ref.at[pl.ds(t, 1), :], xs_hbm.at[pl.ds(slot, 1), :], sem)

        def start(t, c):
            row_copy(t, _slot(meta_ref, ri_ref, 0, t, tm)).start(priority=0)
            row_copy(t, _slot(meta_ref, ri_ref, 1, t, tm)).start(priority=1)
            return c

        lax.fori_loop(0, tm, start, 0, unroll=ISSUE_UNROLL)
        for _ in range(2):
            pltpu.make_async_copy(hn_ref, xs_hbm.at[pl.ds(0, tm), :], sem).wait()

    pl.when(i < tiles_a)(functools.partial(scatter, hn_a_ref))
    pl.when(i >= tiles_a)(functools.partial(scatter, hn_b_ref))


def _dispatch(meta, ri, hn_a, hn_b, n_slots):
    tm = TOKEN_TILE
    tiles_a, tiles_b = hn_a.shape[0] // tm, hn_b.shape[0] // tm
    rows = lambda f: pl.BlockSpec((tm, D_MODEL), f)
    return pl.pallas_call(
        functools.partial(_dispatch_kernel, tiles_a=tiles_a),
        grid_spec=pltpu.PrefetchScalarGridSpec(
            num_scalar_prefetch=1, grid=(tiles_a + tiles_b,),
            in_specs=[pl.BlockSpec((1, 1, 4 * tm), lambda i, m: (i, 0, 0), memory_space=pltpu.SMEM),
                      rows(lambda i, m: (jnp.minimum(i, tiles_a - 1), 0)),
                      rows(lambda i, m: (jnp.maximum(i - tiles_a, 0), 0))],
            out_specs=pl.BlockSpec(memory_space=pl.ANY),
            scratch_shapes=[pltpu.VMEM((EXPERT_TILE // 2, D_MODEL), F32), pltpu.SemaphoreType.DMA]),
        out_shape=jax.ShapeDtypeStruct((n_slots, D_MODEL), F32),
        compiler_params=pltpu.CompilerParams(dimension_semantics=("arbitrary",), vmem_limit_bytes=VMEM_LIMIT),
        name="dispatch",
    )(meta, ri, hn_a, hn_b)


def _expert_kernel(te_ref, nv_ref, xs_ref, wg_ref, wu_ref, wd_ref, out_ref, wgu, wdn):
    i = pl.program_id(0)

    @pl.when((i == 0) | (te_ref[i] != te_ref[jnp.maximum(i - 1, 0)]))
    def _():
        wgu[:, 0:D_EXPERT] = wg_ref[...].astype(BF16)
        wgu[:, D_EXPERT:2 * D_EXPERT] = wu_ref[...].astype(BF16)
        wdn[...] = wd_ref[...].astype(BF16)

    @pl.when(i < nv_ref[0])
    def _():
        x = xs_ref[...].astype(BF16)
        ab = _dot(x, wgu[...])
        a = ab[:, 0:D_EXPERT]
        hid = (a * jax.nn.sigmoid(a)) * ab[:, D_EXPERT:2 * D_EXPERT]
        out_ref[...] = _dot(hid.astype(BF16), wdn[...])

    @pl.when(i >= nv_ref[0])
    def _():
        out_ref[...] = jnp.zeros_like(out_ref)


def _experts(tile_expert, n_valid, xs, wg, wu, wd):
    tr = EXPERT_TILE
    n_tiles = xs.shape[0] // tr
    grid_spec = pltpu.PrefetchScalarGridSpec(
        num_scalar_prefetch=2,
        grid=(n_tiles,),
        in_specs=[pl.BlockSpec((tr, D_MODEL), lambda i, te, nv: (jnp.minimum(i, nv[0] - 1), 0)),
                  pl.BlockSpec((None, D_MODEL, D_EXPERT), lambda i, te, nv: (te[i], 0, 0)),
                  pl.BlockSpec((None, D_MODEL, D_EXPERT), lambda i, te, nv: (te[i], 0, 0)),
                  pl.BlockSpec((None, D_EXPERT, D_MODEL), lambda i, te, nv: (te[i], 0, 0))],
        out_specs=pl.BlockSpec((tr, D_MODEL), lambda i, te, nv: (i, 0)),
        scratch_shapes=[pltpu.VMEM((D_MODEL, 2 * D_EXPERT), BF16), pltpu.VMEM((D_EXPERT, D_MODEL), BF16)],
    )
    return pl.pallas_call(
        _expert_kernel,
        grid_spec=grid_spec,
        out_shape=jax.ShapeDtypeStruct((n_tiles * tr, D_MODEL), F32),
        compiler_params=pltpu.CompilerParams(dimension_semantics=("arbitrary",), vmem_limit_bytes=VMEM_LIMIT),
        name="experts",
    )(tile_expert, n_valid, xs, wg, wu, wd)


def _combine_kernel(meta_ref, ri_ref, ri_next_ref, h_ref, route_ref, out_hbm, y_ref, gbuf, sems):
    tm = h_ref.shape[0]
    i = pl.program_id(0)
    rows = 2 * tm

    def gather(ri, half):
        def row_copy(slot, r):
            return pltpu.make_async_copy(out_hbm.at[pl.ds(slot, 1), :], gbuf.at[pl.ds(half * rows + r, 1), :],
                                         sems.at[half])

        def start(t, c):
            row_copy(_slot(meta_ref, ri, 0, t, tm), t).start(priority=0)
            row_copy(_slot(meta_ref, ri, 1, t, tm), tm + t).start(priority=1)
            return c

        lax.fori_loop(0, tm, start, 0, unroll=ISSUE_UNROLL)

    pl.when(i == 0)(functools.partial(gather, ri_ref, 0))
    pl.when(i + 1 < pl.num_programs(0))(functools.partial(gather, ri_next_ref, (i + 1) % 2))

    half = i % 2
    base = pl.multiple_of(half * rows, rows)
    pltpu.make_async_copy(out_hbm.at[pl.ds(0, rows), :], gbuf.at[pl.ds(base, rows), :], sems.at[half]).wait()
    g = gbuf[pl.ds(base, rows), :]
    route = route_ref[...]
    y_ref[...] = h_ref[...] + route[:, 4:5] * g[0:tm] + route[:, 5:6] * g[tm:rows]


def _combine(meta, ri, h, route, out3):
    n = h.shape[0]
    tm = min(COMBINE_TILE, n)
    n_steps = n // tm
    ri_block = lambda f: pl.BlockSpec((1, 1, 4 * tm), f, memory_space=pltpu.SMEM)
    grid_spec = pltpu.PrefetchScalarGridSpec(
        num_scalar_prefetch=1,
        grid=(n_steps,),
        in_specs=[ri_block(lambda i, m: (i, 0, 0)),
                  ri_block(lambda i, m: (jnp.minimum(i + 1, n_steps - 1), 0, 0)),
                  pl.BlockSpec((tm, D_MODEL), lambda i, m: (i, 0)),
                  pl.BlockSpec((tm, SUBLANES), lambda i, m: (i, 0)),
                  pl.BlockSpec(memory_space=pl.ANY)],
        out_specs=pl.BlockSpec((tm, D_MODEL), lambda i, m: (i, 0)),
        scratch_shapes=[pltpu.VMEM((4 * tm, D_MODEL), F32), pltpu.SemaphoreType.DMA((2,))],
    )
    return pl.pallas_call(
        _combine_kernel,
        grid_spec=grid_spec,
        out_shape=jax.ShapeDtypeStruct((n, D_MODEL), F32),
        compiler_params=pltpu.CompilerParams(dimension_semantics=("arbitrary",), vmem_limit_bytes=VMEM_LIMIT),
        name="combine",
    )(meta, ri, ri, h, route, out3)


def _route_ints(route, tile):
    n = route.shape[0]
    tm = min(tile, n)
    ri = route[:, 0:4].astype(I32).reshape(n // tm, tm, 4)
    return jnp.transpose(ri, (0, 2, 1)).reshape(n // tm, 1, 4 * tm)


def _moe(part_a, part_b, counts, wg, wu, wd):
    tr = EXPERT_TILE
    n_tokens = part_a[0].shape[0] + part_b[0].shape[0]
    n_tiles = pl.cdiv(2 * n_tokens, tr) + N_EXPERTS
    cnt = counts[0, :N_EXPERTS].astype(I32)
    padded = ((cnt + tr - 1) // tr) * tr
    ends = jnp.cumsum(padded)
    meta = jnp.concatenate([ends - padded, cnt, padded, ends[N_EXPERTS - 1:]])
    tile_start = jnp.arange(n_tiles, dtype=I32) * tr
    tile_expert = jnp.minimum(jnp.sum((tile_start[:, None] >= ends[None, :]).astype(I32), axis=1), N_EXPERTS - 1)
    n_valid = (ends[N_EXPERTS - 1] // tr).reshape(1)
    ri = jnp.concatenate([_route_ints(part_a[2], TOKEN_TILE), _route_ints(part_b[2], TOKEN_TILE)])
    xs = _dispatch(meta, ri, part_a[1], part_b[1], n_tiles * tr)
    out3 = _experts(tile_expert, n_valid, xs, wg, wu, wd)
    return [_combine(meta, _route_ints(route, COMBINE_TILE), h, route, out3) for h, _, route in (part_a, part_b)]


def kernel(x_prompt, x_sample, cache_k, cache_v, norm1_g, w_in, q_gain, k_gain, v_gain, w_spatial, b_spatial,
           out_gain_a, out_gain_b, w_out, norm2_g, w_router1, b_router1, w_router2, b_router2, w_up, w_gate,
           w_down):
    depth = norm1_g.shape[0]
    assert depth == 1
    l = 0
    batch, seq, _ = x_prompt.shape
    nb, dec_seq, _ = x_sample.shape
    assert dec_seq == 1

    g1 = norm1_g[l].reshape(1, D_MODEL)
    win = w_in[l].astype(BF16)
    gains = jnp.stack([q_gain[l].reshape(WIDTH), k_gain[l].reshape(WIDTH), v_gain[l].reshape(WIDTH)])
    head_of_lane = jnp.arange(WIDTH) // HEAD_DIM
    bd = (head_of_lane[:, None] == head_of_lane[None, :]).astype(BF16)
    ws_tril = jnp.tril(w_spatial[l])
    wcat = jnp.concatenate([ws_tril[0::2], ws_tril[1::2]], axis=-1).astype(BF16)
    bs = b_spatial[l]
    bias = jnp.where(jnp.arange(LANES)[None, None, :] < HEAD_DIM, bs[0::2][:, :, None], bs[1::2][:, :, None])
    avec = jnp.repeat(w_spatial[l][:, 0, 0], HEAD_DIM).reshape(1, WIDTH)
    bvec = jnp.repeat(bs[:, 0], HEAD_DIM).reshape(1, WIDTH)
    ga = out_gain_a[l].reshape(1, WIDTH)
    gb = out_gain_b[l].reshape(1, WIDTH)
    wout = w_out[l].astype(BF16)
    g2 = norm2_g[l].reshape(1, D_MODEL)
    wr = jnp.zeros((D_MODEL, LANES), F32)
    wr = wr.at[:, :N_EXPERTS].set(jnp.transpose(w_router2[l], (1, 0, 2)).reshape(D_MODEL, N_EXPERTS))
    wr = wr.at[:, N_EXPERTS:N_EXPERTS + N_GROUPS].set(w_router1[l]).astype(BF16)
    br = jnp.zeros((1, LANES), F32)
    br = br.at[0, :N_EXPERTS].set(b_router2[l].reshape(N_EXPERTS))
    br = br.at[0, N_EXPERTS:N_EXPERTS + N_GROUPS].set(b_router1[l])
    lane_head = (jnp.arange(LANES)[:, None] == head_of_lane[None, :]).astype(BF16)
    expand = jnp.concatenate([lane_head, lane_head], axis=0)
    wg = w_gate[l].reshape(N_EXPERTS, D_MODEL, D_EXPERT)
    wu = w_up[l].reshape(N_EXPERTS, D_MODEL, D_EXPERT)
    wd = w_down[l].reshape(N_EXPERTS, D_EXPERT, D_MODEL)

    def tri(t):
        return (jnp.arange(t)[:, None] > jnp.arange(t)[None, :]).astype(BF16)

    xp = x_prompt.reshape(batch * seq, D_MODEL)
    q, k, v, oa, kt_last, vt_last = _proj_prompt(xp, g1, win, gains, bd, wcat, bias, seq=seq)
    obs, lses = [], []
    for dil in DILATIONS:
        o, lse = _band_attention(q, k, v, batch=batch, seq=seq, dil=dil)
        obs.append(o)
        lses.append(lse)
    tmix = min(MIX_TILE, batch * seq)
    h, hn3, route, counts = _mix(xp, oa, obs, lses, ga, gb, wout, g2, wr, br, expand, tri(tmix),
                                 jnp.zeros((1, LANES), F32))
    kept = min(MAX_WINDOW, seq)
    to_cache = lambda t: jnp.transpose(t.reshape(1, batch, N_HEADS, HEAD_DIM, kept), (0, 1, 4, 2, 3))

    xs = x_sample.reshape(nb, D_MODEL)
    qs, ks, vs, vgs, oas = _proj_sample(xs, g1, win, gains, bd, avec, bvec)
    cols = lambda t: jnp.transpose(t.reshape(nb, N_HEADS, HEAD_DIM), (0, 2, 1))
    feature_major = lambda c: jnp.transpose(c, (0, 2, 3, 1)).reshape(nb, WIDTH, c.shape[1])
    ot = _sample_attention(cols(qs), cols(ks), cols(vs), feature_major(cache_k[l]), feature_major(cache_v[l]))
    obs_s = jnp.transpose(ot, (0, 2, 1)).reshape(nb, WIDTH)
    tmix_s = min(MIX_TILE, nb)
    hs, hn3s, route_s, counts_all = _mix(xs, oas, [obs_s], [], ga, gb, wout, g2, wr, br, expand, tri(tmix_s),
                                         counts)

    y_prompt, y_sample = _moe((h, hn3, route), (hs, hn3s, route_s), counts_all, wg, wu, wd)
    to5 = lambda t: t.reshape(1, nb, 1, N_HEADS, HEAD_DIM)
    return (y_prompt.reshape(batch, seq, D_MODEL), y_sample.reshape(nb, 1, D_MODEL),
            to_cache(kt_last), to_cache(vt_last), to5(ks), to5(vs), to5(vgs))
```

```python
import functools

import jax
import jax.numpy as jnp
from jax import lax
from jax.experimental import pallas as pl
from jax.experimental.pallas import tpu as pltpu

F32 = jnp.float32
BF16 = jnp.bfloat16
I32 = jnp.int32

D_MODEL = 1024
HEAD_DIM = 64
N_HEADS = 8
WIDTH = N_HEADS * HEAD_DIM
N_PAIRS = N_HEADS // 2
CHUNK = 128
N_WIN = 128
DILATIONS = (1, 4, 16)
MAX_WINDOW = 2048
N_GROUPS = 4
EXPERTS_PER_GROUP = 8
N_EXPERTS = N_GROUPS * EXPERTS_PER_GROUP
D_EXPERT = D_MODEL // 4
EPS = 1e-6

LANES = 128
SUBLANES = 8
VMEM_LIMIT = 48 * 1024 * 1024

PROJ_TILE = 512
ATTN_TILE = 512
MIX_TILE = 256
EXPERT_TILE = 256
TOKEN_TILE = 128
COMBINE_TILE = 256
ISSUE_UNROLL = 4


def _dot(a, b):
    return jnp.dot(a, b, preferred_element_type=F32)


def _gelu(x):
    return 0.5 * x * (1.0 + jnp.tanh(0.7978845608028654 * (x + 0.044715 * (x * x * x))))


def _head_rms(t, gain, bd):
    ss = _dot((t * t).astype(BF16), bd)
    return t * lax.rsqrt(ss * (1.0 / HEAD_DIM) + EPS) * gain


def _proj_common(x_ref, g1_ref, win_ref, gains_ref, bd_ref):
    x = x_ref[...]
    r = lax.rsqrt(jnp.mean(x * x, axis=-1, keepdims=True) + EPS)
    xn = (x * r * g1_ref[...]).astype(BF16)
    bd = bd_ref[...]
    q = _head_rms(_dot(xn, win_ref[:, 0:WIDTH]), gains_ref[0:1, :], bd) * (HEAD_DIM ** -0.5)
    k = _head_rms(_dot(xn, win_ref[:, WIDTH:2 * WIDTH]), gains_ref[1:2, :], bd)
    v = _dot(xn, win_ref[:, 2 * WIDTH:3 * WIDTH])
    u = _gelu(_dot(xn, win_ref[:, 3 * WIDTH:4 * WIDTH]))
    vg = _head_rms(_gelu(_dot(xn, win_ref[:, 4 * WIDTH:5 * WIDTH])), gains_ref[2:3, :], bd)
    return q, k, v, u, vg


def _proj_prompt_kernel(x_ref, g1_ref, win_ref, gains_ref, bd_ref, wcat_ref, bias_ref, *rest,
                        tiles_per_seq, first_kept_tile):
    stream_refs = rest[:3 * len(DILATIONS)]
    oa_ref, kl_ref, vl_ref, slab = rest[3 * len(DILATIONS):]
    q, k, v, u, vg = _proj_common(x_ref, g1_ref, win_ref, gains_ref, bd_ref)
    tm = x_ref.shape[0]
    for a, z in enumerate((q, k, v)):
        for p in range(N_PAIRS):
            slab[p] = z[:, p * LANES:(p + 1) * LANES]
        for c, dil in enumerate(DILATIONS):
            out_ref = stream_refs[3 * c + a]
            if dil == 1:
                out_ref[...] = z.astype(BF16)
                continue
            for r in range(dil):
                for p in range(N_PAIRS):
                    cols = slice(r * WIDTH + p * LANES, r * WIDTH + (p + 1) * LANES)
                    out_ref[:, cols] = slab[p, pl.ds(r, tm // dil, stride=dil), :].astype(BF16)

    @pl.when(pl.program_id(0) % tiles_per_seq >= first_kept_tile)
    def _():
        kl_ref[...] = k.T
        vl_ref[...] = v.T

    vgb = vg.astype(BF16)
    lane = lax.broadcasted_iota(I32, (CHUNK, LANES), 1)
    tm = x_ref.shape[0]
    for c in range(tm // CHUNK):
        rows = slice(c * CHUNK, (c + 1) * CHUNK)
        for p in range(N_PAIRS):
            cols = slice(p * LANES, (p + 1) * LANES)
            vp = vgb[rows, cols]
            zero = jnp.zeros_like(vp)
            rhs = jnp.concatenate([jnp.where(lane < HEAD_DIM, vp, zero),
                                   jnp.where(lane >= HEAD_DIM, vp, zero)], axis=0)
            mixed = _dot(wcat_ref[p], rhs) + bias_ref[p]
            oa_ref[rows, cols] = (u[rows, cols] * mixed).astype(BF16)


def _proj_sample_kernel(x_ref, g1_ref, win_ref, gains_ref, bd_ref, avec_ref, bvec_ref,
                        q_ref, k_ref, v_ref, vg_ref, oa_ref):
    q, k, v, u, vg = _proj_common(x_ref, g1_ref, win_ref, gains_ref, bd_ref)
    q_ref[...] = q
    k_ref[...] = k
    v_ref[...] = v
    vg_ref[...] = vg
    oa_ref[...] = (u * (avec_ref[...] * vg + bvec_ref[...])).astype(BF16)


def _const_spec(shape):
    return pl.BlockSpec(shape, lambda *_: (0,) * len(shape))


def _proj_prompt(x2, g1, win, gains, bd, wcat, bias, *, seq):
    n = x2.shape[0]
    tm = PROJ_TILE
    tiles_per_seq = seq // tm
    kept = min(MAX_WINDOW, seq)
    first_kept_tile = tiles_per_seq - kept // tm

    def kept_map(i):
        return (i // tiles_per_seq, 0, jnp.maximum(i % tiles_per_seq - first_kept_tile, 0))

    row_spec = pl.BlockSpec((tm, WIDTH), lambda i: (i, 0))
    kept_spec = pl.BlockSpec((None, WIDTH, tm), kept_map)
    stream_specs = [pl.BlockSpec((tm // dil, dil * WIDTH), lambda i: (i, 0)) for dil in DILATIONS for _ in range(3)]
    stream_shapes = [jax.ShapeDtypeStruct((n // dil, dil * WIDTH), BF16) for dil in DILATIONS for _ in range(3)]
    outs = pl.pallas_call(
        functools.partial(_proj_prompt_kernel, tiles_per_seq=tiles_per_seq, first_kept_tile=first_kept_tile),
        grid=(n // tm,),
        in_specs=[pl.BlockSpec((tm, D_MODEL), lambda i: (i, 0)),
                  _const_spec((1, D_MODEL)), _const_spec((D_MODEL, 5 * WIDTH)), _const_spec((3, WIDTH)),
                  _const_spec((WIDTH, WIDTH)), _const_spec((N_PAIRS, CHUNK, 2 * CHUNK)),
                  _const_spec((N_PAIRS, CHUNK, LANES))],
        out_specs=stream_specs + [row_spec, kept_spec, kept_spec],
        out_shape=stream_shapes + [jax.ShapeDtypeStruct((n, WIDTH), BF16)] +
                  [jax.ShapeDtypeStruct((n // seq, WIDTH, kept), F32)] * 2,
        scratch_shapes=[pltpu.VMEM((N_PAIRS, tm, LANES), F32)],
        compiler_params=pltpu.CompilerParams(dimension_semantics=("arbitrary",), vmem_limit_bytes=VMEM_LIMIT),
        name="proj_prompt",
    )(x2, g1, win, gains, bd, wcat, bias)
    n_streams = 3 * len(DILATIONS)
    qkv = [outs[3 * c:3 * c + 3] for c in range(len(DILATIONS))]
    return (qkv, *outs[n_streams:])


def _proj_sample(x2, g1, win, gains, bd, avec, bvec):
    n = x2.shape[0]
    full = _const_spec((n, WIDTH))
    return pl.pallas_call(
        _proj_sample_kernel,
        grid=(1,),
        in_specs=[_const_spec((n, D_MODEL)), _const_spec((1, D_MODEL)), _const_spec((D_MODEL, 5 * WIDTH)),
                  _const_spec((3, WIDTH)), _const_spec((WIDTH, WIDTH)), _const_spec((1, WIDTH)),
                  _const_spec((1, WIDTH))],
        out_specs=[full] * 5,
        out_shape=[jax.ShapeDtypeStruct((n, WIDTH), F32)] * 4 + [jax.ShapeDtypeStruct((n, WIDTH), BF16)],
        compiler_params=pltpu.CompilerParams(dimension_semantics=("arbitrary",), vmem_limit_bytes=VMEM_LIMIT),
        name="proj_sample",
    )(x2, g1, win, gains, bd, avec, bvec)


def _band_attn_kernel(q_ref, k_ref, v_ref, o_ref, lse_ref, kbuf, vbuf, *, tq):
    j = pl.program_id(2)
    nblk = tq // N_WIN

    @pl.when(j == 0)
    def _():
        kbuf[0:N_WIN, :] = jnp.zeros((N_WIN, WIDTH), BF16)
        vbuf[...] = jnp.ones(vbuf.shape, BF16)

    @pl.when(j > 0)
    def _():
        kbuf[0:N_WIN, :] = kbuf[tq:tq + N_WIN, :]
        vbuf[0:N_WIN, :] = vbuf[tq:tq + N_WIN, :]

    kbuf[N_WIN:N_WIN + tq, :] = k_ref[...]
    for p in range(N_PAIRS):
        vbuf[N_WIN:N_WIN + tq, 2 * p * LANES:(2 * p + 1) * LANES] = v_ref[:, p * LANES:(p + 1) * LANES]

    qi = lax.broadcasted_iota(I32, (N_WIN, 2 * N_WIN), 0)
    ki = lax.broadcasted_iota(I32, (N_WIN, 2 * N_WIN), 1)
    band = (ki >= qi) & (ki <= qi + N_WIN)
    lane = lax.broadcasted_iota(I32, (N_WIN, LANES), 1)
    low_half = lane < HEAD_DIM

    def block(jb, carry):
        q0 = pl.multiple_of(jb * N_WIN, N_WIN)
        valid = band & ((ki >= N_WIN) | (j * nblk + jb > 0))
        lse_tile = jnp.zeros((N_WIN, LANES), F32)
        for p in range(N_PAIRS):
            cols = slice(p * LANES, (p + 1) * LANES)
            qp = q_ref[pl.ds(q0, N_WIN), cols]
            kp = kbuf[pl.ds(q0, 2 * N_WIN), cols]
            vp = vbuf[pl.ds(q0, 2 * N_WIN), 2 * p * LANES:(2 * p + 2) * LANES]
            halves = []
            for hh in range(2):
                mine = low_half if hh == 0 else ~low_half
                qm = jnp.where(mine, qp, jnp.zeros_like(qp))
                s = lax.dot_general(qm, kp, (((1,), (1,)), ((), ())), preferred_element_type=F32)
                s = jnp.where(valid, s, -jnp.inf)
                m = jnp.max(s, axis=-1, keepdims=True)
                e = jnp.exp(s - m)
                ov = _dot(e.astype(BF16), vp)
                den = ov[:, LANES:2 * LANES]
                halves.append(ov[:, 0:LANES] / den)
                lse_tile = jnp.where(lane == 2 * p + hh, m + jnp.log(den), lse_tile)
            o_ref[pl.ds(q0, N_WIN), cols] = jnp.where(low_half, halves[0], halves[1]).astype(BF16)
        lse_ref[pl.ds(q0, N_WIN), :] = lse_tile
        return carry

    lax.fori_loop(0, nblk, block, 0)


def _band_attention(q2, k2, v2, *, batch, seq, dil):
    length = seq // dil
    tq = min(ATTN_TILE, length)
    view = lambda a: a.reshape(batch, length, a.shape[-1])
    spec = pl.BlockSpec((None, tq, WIDTH), lambda b, r, j: (b, j, r))
    lse_spec = pl.BlockSpec((None, tq, LANES), lambda b, r, j: (b, j, r))
    o, lse = pl.pallas_call(
        functools.partial(_band_attn_kernel, tq=tq),
        grid=(batch, dil, length // tq),
        in_specs=[spec, spec, spec],
        out_specs=[spec, lse_spec],
        out_shape=[jax.ShapeDtypeStruct((batch, length, dil * WIDTH), BF16),
                   jax.ShapeDtypeStruct((batch, length, dil * LANES), F32)],
        scratch_shapes=[pltpu.VMEM((N_WIN + tq, WIDTH), BF16), pltpu.VMEM((N_WIN + tq, 2 * WIDTH), BF16)],
        compiler_params=pltpu.CompilerParams(dimension_semantics=("arbitrary", "arbitrary", "arbitrary"),
                                             vmem_limit_bytes=VMEM_LIMIT),
        name=f"band_attn_d{dil}",
    )(view(q2), view(k2), view(v2))
    return o.reshape(batch * length, dil * WIDTH), lse.reshape(batch * length, dil * LANES)


def _sample_attn_kernel(qt_ref, knt_ref, vnt_ref, kt_ref, vt_ref, ot_ref):
    qt = qt_ref[...]
    vnt = vnt_ref[...]
    s_new = jnp.sum(qt * knt_ref[...], axis=0, keepdims=True)
    wb = kt_ref.shape[1]
    heads = range(N_HEADS)
    head_rows = lambda h: slice(h * HEAD_DIM, (h + 1) * HEAD_DIM)
    s = jnp.concatenate([jnp.sum(kt_ref[head_rows(h), :] * qt[:, h:h + 1], axis=0, keepdims=True) for h in heads],
                        axis=0)
    s0 = jnp.concatenate([s_new[:, h:h + 1] for h in heads], axis=0)
    pos = lax.broadcasted_iota(I32, (N_HEADS, wb), 1)
    es, e0s, dens, lses = [], [], [], []
    for dil in DILATIONS:
        lo = wb - N_WIN * dil
        sc = s[:, lo:]
        valid = (pos[:, lo:] & (dil - 1)) == 0
        m = jnp.maximum(jnp.max(jnp.where(valid, sc, -jnp.inf), axis=-1, keepdims=True), s0)
        e = jnp.where(valid, jnp.exp(sc - m), 0.0)
        e0 = jnp.exp(s0 - m)
        den = jnp.sum(e, axis=-1, keepdims=True) + e0
        es.append(e)
        e0s.append(e0)
        dens.append(den)
        lses.append(m + jnp.log(den))
    mm = jnp.maximum(jnp.maximum(lses[0], lses[1]), lses[2])
    ws = [jnp.exp(l - mm) for l in lses]
    tot = ws[0] + ws[1] + ws[2]
    coef = [w / (tot * den) for w, den in zip(ws, dens)]
    w1, w4, w16 = [e * c for e, c in zip(es, coef)]
    w_new = coef[0] * e0s[0] + coef[1] * e0s[1] + coef[2] * e0s[2]
    n1, n4 = N_WIN * DILATIONS[0], N_WIN * DILATIONS[1]
    w_pos = jnp.concatenate([w16[:, :wb - n4], w16[:, wb - n4:wb - n1] + w4[:, :n4 - n1],
                             w16[:, wb - n1:] + w4[:, n4 - n1:] + w1], axis=1)
    head_lane = lax.broadcasted_iota(I32, (HEAD_DIM, N_HEADS), 1)
    ot = jnp.zeros((HEAD_DIM, N_HEADS), F32)
    for h in heads:
        col = (jnp.sum(vt_ref[head_rows(h), :] * w_pos[h:h + 1, :], axis=-1, keepdims=True) +
               w_new[h:h + 1, :] * vnt[:, h:h + 1])
        ot = jnp.where(head_lane == h, col, ot)
    ot_ref[...] = ot


def _sample_attention(qt, knt, vnt, kt, vt):
    nb, _, wb = kt.shape
    assert wb == MAX_WINDOW, "cache window must cover every dilated key"
    tok = pl.BlockSpec((None, HEAD_DIM, N_HEADS), lambda b: (b, 0, 0))
    cache = pl.BlockSpec((None, WIDTH, wb), lambda b: (b, 0, 0))
    return pl.pallas_call(
        _sample_attn_kernel,
        grid=(nb,),
        in_specs=[tok, tok, tok, cache, cache],
        out_specs=tok,
        out_shape=jax.ShapeDtypeStruct((nb, HEAD_DIM, N_HEADS), F32),
        compiler_params=pltpu.CompilerParams(dimension_semantics=("arbitrary",), vmem_limit_bytes=VMEM_LIMIT),
        name="sample_attn",
    )(qt, knt, vnt, kt, vt)


def _split_hi_lo(w):
    hi = w.astype(BF16)
    return jnp.concatenate([hi, (w - hi.astype(F32)).astype(BF16)], axis=-1)


def _token_order(ref, dil, width, slab, tm):
    if dil == 1:
        return ref[...].astype(F32)
    n_slabs = width // LANES
    for r in range(dil):
        for p in range(n_slabs):
            cols = slice(r * width + p * LANES, r * width + (p + 1) * LANES)
            slab[p, pl.ds(r, tm // dil, stride=dil), :] = ref[:, cols].astype(F32)
    return jnp.concatenate([slab[p] for p in range(n_slabs)], axis=1)


def _mix_kernel(*refs, dils):
    n_cfg = max(len(dils), 1)
    n_lse = len(dils)
    x_ref, oa_ref = refs[0], refs[1]
    o_refs = refs[2:2 + n_cfg]
    lse_refs = refs[2 + n_cfg:2 + n_cfg + n_lse]
    (ga_ref, gb_ref, wout_ref, g2_ref, wr_ref, br_ref, expand_ref, tri_ref, cnt0_ref,
     h_ref, hn_ref, route_ref, cnt_ref, carry, slab) = refs[2 + n_cfg + n_lse:]
    tm = x_ref.shape[0]

    @pl.when(pl.program_id(0) == 0)
    def _():
        carry[...] = cnt0_ref[...]

    if not dils:
        ob = o_refs[0][...].astype(F32)
    else:
        lses = [_token_order(r, dil, LANES, slab, tm) for r, dil in zip(lse_refs, dils)]
        mm = functools.reduce(jnp.maximum, lses)
        ws = [jnp.exp(l - mm) for l in lses]
        tot = functools.reduce(lambda a, b: a + b, ws)
        ob = jnp.zeros((tm, WIDTH), F32)
        for w, o_ref, dil in zip(ws, o_refs, dils):
            ob = ob + _dot(_split_hi_lo(w / tot), expand_ref[...]) * _token_order(o_ref, dil, WIDTH, slab, tm)

    oa = oa_ref[...].astype(F32)
    ya = oa * lax.rsqrt(jnp.mean(oa * oa, axis=-1, keepdims=True) + EPS) * ga_ref[...]
    yb = ob * lax.rsqrt(jnp.mean(ob * ob, axis=-1, keepdims=True) + EPS) * gb_ref[...]
    cat = jnp.concatenate([ya, yb], axis=-1).astype(BF16)
    h = x_ref[...] + _dot(cat, wout_ref[...])
    h_ref[...] = h
    hn = h * lax.rsqrt(jnp.mean(h * h, axis=-1, keepdims=True) + EPS) * g2_ref[...]
    hn_ref[...] = hn

    logits = _dot(hn.astype(BF16), wr_ref[...]) + br_ref[...]
    lane = lax.broadcasted_iota(I32, (tm, LANES), 1)
    lane_f = lane.astype(F32)
    neg = -jnp.inf
    big = float(LANES)
    coarse = (lane >= N_EXPERTS) & (lane < N_EXPERTS + N_GROUPS)
    lg = jnp.where(coarse, logits, neg)
    mx = jnp.max(lg, axis=-1, keepdims=True)
    g_lane = jnp.min(jnp.where(lg == mx, lane_f, big), axis=-1, keepdims=True)
    p_star = 1.0 / jnp.sum(jnp.exp(lg - mx), axis=-1, keepdims=True)
    lo = (g_lane - float(N_EXPERTS)) * float(EXPERTS_PER_GROUP)
    lf = jnp.where((lane_f >= lo) & (lane_f < lo + float(EXPERTS_PER_GROUP)), logits, neg)
    v1 = jnp.max(lf, axis=-1, keepdims=True)
    i1 = jnp.min(jnp.where(lf == v1, lane_f, big), axis=-1, keepdims=True)
    lf2 = jnp.where(lane_f == i1, neg, lf)
    v2 = jnp.max(lf2, axis=-1, keepdims=True)
    i2 = jnp.min(jnp.where(lf2 == v2, lane_f, big), axis=-1, keepdims=True)
    e21 = jnp.exp(v2 - v1)
    w1 = p_star / (1.0 + e21)
    w2 = p_star * e21 / (1.0 + e21)

    sel1 = lane_f == i1
    sel2 = lane_f == i2
    onehot = jnp.where(sel1 | sel2, 1.0, 0.0)
    before = _dot(tri_ref[...], onehot.astype(BF16)) + carry[...]
    r1 = jnp.sum(jnp.where(sel1, before, 0.0), axis=-1, keepdims=True)
    r2 = jnp.sum(jnp.where(sel2, before, 0.0), axis=-1, keepdims=True)
    carry[...] = carry[...] + jnp.sum(onehot, axis=0, keepdims=True)
    cnt_ref[...] = carry[...]

    col = lax.broadcasted_iota(I32, (tm, SUBLANES), 1)
    route = jnp.zeros((tm, SUBLANES), F32)
    for idx, val in enumerate((i1, i2, r1, r2, w1, w2)):
        route = jnp.where(col == idx, val, route)
    route_ref[...] = route


def _mix(x2, oa, obs, lses, dils, ga, gb, wout, g2, wr, br, expand, tri, cnt0):
    n = x2.shape[0]
    tm = min(MIX_TILE, n)
    row = lambda w: pl.BlockSpec((tm, w), lambda i: (i, 0))
    stream = lambda w, dil: pl.BlockSpec((tm // dil, dil * w), lambda i: (i, 0))
    in_specs = ([row(D_MODEL), row(WIDTH)] + [stream(WIDTH, dil) for dil in (dils or (1,))] +
                [stream(LANES, dil) for dil in dils] +
                [_const_spec((1, WIDTH)), _const_spec((1, WIDTH)), _const_spec((D_MODEL, D_MODEL)),
                 _const_spec((1, D_MODEL)), _const_spec((D_MODEL, LANES)), _const_spec((1, LANES)),
                 _const_spec((2 * LANES, WIDTH)), _const_spec((tm, tm)), _const_spec((1, LANES))])
    return pl.pallas_call(
        functools.partial(_mix_kernel, dils=tuple(dils)),
        grid=(n // tm,),
        in_specs=in_specs,
        out_specs=[row(D_MODEL), row(D_MODEL), row(SUBLANES), _const_spec((1, LANES))],
        out_shape=[jax.ShapeDtypeStruct((n, D_MODEL), F32), jax.ShapeDtypeStruct((n, D_MODEL), F32),
                   jax.ShapeDtypeStruct((n, SUBLANES), F32), jax.ShapeDtypeStruct((1, LANES), F32)],
        scratch_shapes=[pltpu.VMEM((1, LANES), F32), pltpu.VMEM((N_PAIRS, tm, LANES), F32)],
        compiler_params=pltpu.CompilerParams(dimension_semantics=("arbitrary",), vmem_limit_bytes=VMEM_LIMIT),
        name="mix",
    )(x2, oa, *obs, *lses, ga, gb, wout, g2, wr, br, expand, tri, cnt0)


def _slot(meta_ref, ri_ref, choice, t, tm):
    return meta_ref[ri_ref[0, 0, choice * tm + t]] + ri_ref[0, 0, (2 + choice) * tm + t]


def _dispatch_kernel(meta_ref, ri_ref, hn_a_ref, hn_b_ref, xs_hbm, zbuf, sem, *, tiles_a):
    tm = hn_a_ref.shape[0]
    n_slots = xs_hbm.shape[0]
    zrows = zbuf.shape[0]
    i = pl.program_id(0)

    def fill(off, rows):
        cp = pltpu.make_async_copy(zbuf.at[pl.ds(0, rows), :], xs_hbm.at[pl.ds(off, rows), :], sem)
        cp.start()
        cp.wait()

    @pl.when(i == 0)
    def _():
        zbuf[...] = jnp.zeros_like(zbuf)

        def pad(e, c):
            lo = meta_ref[e] + meta_ref[N_EXPERTS + e]
            n_pad = meta_ref[2 * N_EXPERTS + e] - meta_ref[N_EXPERTS + e]
            n_single = (-lo) & (SUBLANES - 1)
            for k in range(SUBLANES - 1):
                pl.when(k < n_single)(functools.partial(fill, lo + k, 1))
            lo8 = lo + n_single
            n8 = n_pad - n_single
            bit = zrows
            while bit >= SUBLANES:
                off = pl.multiple_of(lo8 + (n8 & ~(2 * bit - 1)), SUBLANES)
                pl.when((n8 & bit) != 0)(functools.partial(fill, off, bit))
                bit //= 2
            return c

        lax.fori_loop(0, N_EXPERTS, pad, 0)

        def tail(j, c):
            fill(pl.multiple_of(meta_ref[3 * N_EXPERTS] + j * zrows, zrows), zrows)
            return c

        lax.fori_loop(0, (n_slots - meta_ref[3 * N_EXPERTS]) // zrows, tail, 0)

    def scatter(hn_ref):
        def row_copy(t, slot):
            return pltpu.make_async_copy(hn_ref.at[pl.ds(t, 1), :], xs_hbm.at[pl.ds(slot, 1), :], sem)

        def start(t, c):
            row_copy(t, _slot(meta_ref, ri_ref, 0, t, tm)).start(priority=0)
            row_copy(t, _slot(meta_ref, ri_ref, 1, t, tm)).start(priority=1)
            return c

        lax.fori_loop(0, tm, start, 0, unroll=ISSUE_UNROLL)
        for _ in range(2):
            pltpu.make_async_copy(hn_ref, xs_hbm.at[pl.ds(0, tm), :], sem).wait()

    pl.when(i < tiles_a)(functools.partial(scatter, hn_a_ref))
    pl.when(i >= tiles_a)(functools.partial(scatter, hn_b_ref))


def _dispatch(meta, ri, hn_a, hn_b, n_slots):
    tm = TOKEN_TILE
    tiles_a, tiles_b = hn_a.shape[0] // tm, hn_b.shape[0] // tm
    rows = lambda f: pl.BlockSpec((tm, D_MODEL), f)
    return pl.pallas_call(
        functools.partial(_dispatch_kernel, tiles_a=tiles_a),
        grid_spec=pltpu.PrefetchScalarGridSpec(
            num_scalar_prefetch=1, grid=(tiles_a + tiles_b,),
            in_specs=[pl.BlockSpec((1, 1, 4 * tm), lambda i, m: (i, 0, 0), memory_space=pltpu.SMEM),
                      rows(lambda i, m: (jnp.minimum(i, tiles_a - 1), 0)),
                      rows(lambda i, m: (jnp.maximum(i - tiles_a, 0), 0))],
            out_specs=pl.BlockSpec(memory_space=pl.ANY),
            scratch_shapes=[pltpu.VMEM((EXPERT_TILE // 2, D_MODEL), F32), pltpu.SemaphoreType.DMA]),
        out_shape=jax.ShapeDtypeStruct((n_slots, D_MODEL), F32),
        compiler_params=pltpu.CompilerParams(dimension_semantics=("arbitrary",), vmem_limit_bytes=VMEM_LIMIT),
        name="dispatch",
    )(meta, ri, hn_a, hn_b)


def _expert_kernel(te_ref, nv_ref, xs_ref, wg_ref, wu_ref, wd_ref, out_ref, wgu, wdn):
    i = pl.program_id(0)

    @pl.when((i == 0) | (te_ref[i] != te_ref[jnp.maximum(i - 1, 0)]))
    def _():
        wgu[:, 0:D_EXPERT] = wg_ref[...].astype(BF16)
        wgu[:, D_EXPERT:2 * D_EXPERT] = wu_ref[...].astype(BF16)
        wdn[...] = wd_ref[...].astype(BF16)

    @pl.when(i < nv_ref[0])
    def _():
        x = xs_ref[...].astype(BF16)
        ab = _dot(x, wgu[...])
        a = ab[:, 0:D_EXPERT]
        hid = (a * jax.nn.sigmoid(a)) * ab[:, D_EXPERT:2 * D_EXPERT]
        out_ref[...] = _dot(hid.astype(BF16), wdn[...])

    @pl.when(i >= nv_ref[0])
    def _():
        out_ref[...] = jnp.zeros_like(out_ref)


def _experts(tile_expert, n_valid, xs, wg, wu, wd):
    tr = EXPERT_TILE
    n_tiles = xs.shape[0] // tr
    grid_spec = pltpu.PrefetchScalarGridSpec(
        num_scalar_prefetch=2,
        grid=(n_tiles,),
        in_specs=[pl.BlockSpec((tr, D_MODEL), lambda i, te, nv: (jnp.minimum(i, nv[0] - 1), 0)),
                  pl.BlockSpec((None, D_MODEL, D_EXPERT), lambda i, te, nv: (te[i], 0, 0)),
                  pl.BlockSpec((None, D_MODEL, D_EXPERT), lambda i, te, nv: (te[i], 0, 0)),
                  pl.BlockSpec((None, D_EXPERT, D_MODEL), lambda i, te, nv: (te[i], 0, 0))],
        out_specs=pl.BlockSpec((tr, D_MODEL), lambda i, te, nv: (i, 0)),
        scratch_shapes=[pltpu.VMEM((D_MODEL, 2 * D_EXPERT), BF16), pltpu.VMEM((D_EXPERT, D_MODEL), BF16)],
    )
    return pl.pallas_call(
        _expert_kernel,
        grid_spec=grid_spec,
        out_shape=jax.ShapeDtypeStruct((n_tiles * tr, D_MODEL), F32),
        compiler_params=pltpu.CompilerParams(dimension_semantics=("arbitrary",), vmem_limit_bytes=VMEM_LIMIT),
        name="experts",
    )(tile_expert, n_valid, xs, wg, wu, wd)


def _combine_kernel(meta_ref, ri_ref, ri_next_ref, h_ref, route_ref, out_hbm, y_ref, gbuf, sems):
    tm = h_ref.shape[0]
    i = pl.program_id(0)
    rows = 2 * tm

    def gather(ri, half):
        def row_copy(slot, r):
            return pltpu.make_async_copy(out_hbm.at[pl.ds(slot, 1), :], gbuf.at[pl.ds(half * rows + r, 1), :],
                                         sems.at[half])

        def start(t, c):
            row_copy(_slot(meta_ref, ri, 0, t, tm), t).start(priority=0)
            row_copy(_slot(meta_ref, ri, 1, t, tm), tm + t).start(priority=1)
            return c

        lax.fori_loop(0, tm, start, 0, unroll=ISSUE_UNROLL)

    pl.when(i == 0)(functools.partial(gather, ri_ref, 0))
    pl.when(i + 1 < pl.num_programs(0))(functools.partial(gather, ri_next_ref, (i + 1) % 2))

    half = i % 2
    base = pl.multiple_of(half * rows, rows)
    pltpu.make_async_copy(out_hbm.at[pl.ds(0, rows), :], gbuf.at[pl.ds(base, rows), :], sems.at[half]).wait()
    g = gbuf[pl.ds(base, rows), :]
    route = route_ref[...]
    y_ref[...] = h_ref[...] + route[:, 4:5] * g[0:tm] + route[:, 5:6] * g[tm:rows]


def _combine(meta, ri, h, route, out3):
    n = h.shape[0]
    tm = min(COMBINE_TILE, n)
    n_steps = n // tm
    ri_block = lambda f: pl.BlockSpec((1, 1, 4 * tm), f, memory_space=pltpu.SMEM)
    grid_spec = pltpu.PrefetchScalarGridSpec(
        num_scalar_prefetch=1,
        grid=(n_steps,),
        in_specs=[ri_block(lambda i, m: (i, 0, 0)),
                  ri_block(lambda i, m: (jnp.minimum(i + 1, n_steps - 1), 0, 0)),
                  pl.BlockSpec((tm, D_MODEL), lambda i, m: (i, 0)),
                  pl.BlockSpec((tm, SUBLANES), lambda i, m: (i, 0)),
                  pl.BlockSpec(memory_space=pl.ANY)],
        out_specs=pl.BlockSpec((tm, D_MODEL), lambda i, m: (i, 0)),
        scratch_shapes=[pltpu.VMEM((4 * tm, D_MODEL), F32), pltpu.SemaphoreType.DMA((2,))],
    )
    return pl.pallas_call(
        _combine_kernel,
        grid_spec=grid_spec,
        out_shape=jax.ShapeDtypeStruct((n, D_MODEL), F32),
        compiler_params=pltpu.CompilerParams(dimension_semantics=("arbitrary",), vmem_limit_bytes=VMEM_LIMIT),
        name="combine",
    )(meta, ri, ri, h, route, out3)


def _route_ints(route, tile):
    n = route.shape[0]
    tm = min(tile, n)
    ri = route[:, 0:4].astype(I32).reshape(n // tm, tm, 4)
    return jnp.transpose(ri, (0, 2, 1)).reshape(n // tm, 1, 4 * tm)


def _moe(part_a, part_b, counts, wg, wu, wd):
    tr = EXPERT_TILE
    n_tokens = part_a[0].shape[0] + part_b[0].shape[0]
    n_tiles = pl.cdiv(2 * n_tokens, tr) + N_EXPERTS
    cnt = counts[0, :N_EXPERTS].astype(I32)
    padded = ((cnt + tr - 1) // tr) * tr
    ends = jnp.cumsum(padded)
    meta = jnp.concatenate([ends - padded, cnt, padded, ends[N_EXPERTS - 1:]])
    tile_start = jnp.arange(n_tiles, dtype=I32) * tr
    tile_expert = jnp.minimum(jnp.sum((tile_start[:, None] >= ends[None, :]).astype(I32), axis=1), N_EXPERTS - 1)
    n_valid = (ends[N_EXPERTS - 1] // tr).reshape(1)
    ri = jnp.concatenate([_route_ints(part_a[2], TOKEN_TILE), _route_ints(part_b[2], TOKEN_TILE)])
    xs = _dispatch(meta, ri, part_a[1], part_b[1], n_tiles * tr)
    out3 = _experts(tile_expert, n_valid, xs, wg, wu, wd)
    return [_combine(meta, _route_ints(route, COMBINE_TILE), h, route, out3) for h, _, route in (part_a, part_b)]


def kernel(x_prompt, x_sample, cache_k, cache_v, norm1_g, w_in, q_gain, k_gain, v_gain, w_spatial, b_spatial,
           out_gain_a, out_gain_b, w_out, norm2_g, w_router1, b_router1, w_router2, b_router2, w_up, w_gate,
           w_down):
    depth = norm1_g.shape[0]
    assert depth == 1
    l = 0
    batch, seq, _ = x_prompt.shape
    nb, dec_seq, _ = x_sample.shape
    assert dec_seq == 1

    g1 = norm1_g[l].reshape(1, D_MODEL)
    win = w_in[l].astype(BF16)
    gains = jnp.stack([q_gain[l].reshape(WIDTH), k_gain[l].reshape(WIDTH), v_gain[l].reshape(WIDTH)])
    head_of_lane = jnp.arange(WIDTH) // HEAD_DIM
    bd = (head_of_lane[:, None] == head_of_lane[None, :]).astype(BF16)
    ws_tril = jnp.tril(w_spatial[l])
    wcat = jnp.concatenate([ws_tril[0::2], ws_tril[1::2]], axis=-1).astype(BF16)
    bs = b_spatial[l]
    bias = jnp.where(jnp.arange(LANES)[None, None, :] < HEAD_DIM, bs[0::2][:, :, None], bs[1::2][:, :, None])
    avec = jnp.repeat(w_spatial[l][:, 0, 0], HEAD_DIM).reshape(1, WIDTH)
    bvec = jnp.repeat(bs[:, 0], HEAD_DIM).reshape(1, WIDTH)
    ga = out_gain_a[l].reshape(1, WIDTH)
    gb = out_gain_b[l].reshape(1, WIDTH)
    wout = w_out[l].astype(BF16)
    g2 = norm2_g[l].reshape(1, D_MODEL)
    wr = jnp.zeros((D_MODEL, LANES), F32)
    wr = wr.at[:, :N_EXPERTS].set(jnp.transpose(w_router2[l], (1, 0, 2)).reshape(D_MODEL, N_EXPERTS))
    wr = wr.at[:, N_EXPERTS:N_EXPERTS + N_GROUPS].set(w_router1[l]).astype(BF16)
    br = jnp.zeros((1, LANES), F32)
    br = br.at[0, :N_EXPERTS].set(b_router2[l].reshape(N_EXPERTS))
    br = br.at[0, N_EXPERTS:N_EXPERTS + N_GROUPS].set(b_router1[l])
    lane_head = (jnp.arange(LANES)[:, None] == head_of_lane[None, :]).astype(BF16)
    expand = jnp.concatenate([lane_head, lane_head], axis=0)
    wg = w_gate[l].reshape(N_EXPERTS, D_MODEL, D_EXPERT)
    wu = w_up[l].reshape(N_EXPERTS, D_MODEL, D_EXPERT)
    wd = w_down[l].reshape(N_EXPERTS, D_EXPERT, D_MODEL)

    def tri(t):
        return (jnp.arange(t)[:, None] > jnp.arange(t)[None, :]).astype(BF16)

    xp = x_prompt.reshape(batch * seq, D_MODEL)
    qkv, oa, kt_last, vt_last = _proj_prompt(xp, g1, win, gains, bd, wcat, bias, seq=seq)
    obs, lses = [], []
    for (q, k, v), dil in zip(qkv, DILATIONS):
        o, lse = _band_attention(q, k, v, batch=batch, seq=seq, dil=dil)
        obs.append(o)
        lses.append(lse)
    tmix = min(MIX_TILE, batch * seq)
    h, hn3, route, counts = _mix(xp, oa, obs, lses, DILATIONS, ga, gb, wout, g2, wr, br, expand, tri(tmix),
                                 jnp.zeros((1, LANES), F32))
    kept = min(MAX_WINDOW, seq)
    to_cache = lambda t: jnp.transpose(t.reshape(1, batch, N_HEADS, HEAD_DIM, kept), (0, 1, 4, 2, 3))

    xs = x_sample.reshape(nb, D_MODEL)
    qs, ks, vs, vgs, oas = _proj_sample(xs, g1, win, gains, bd, avec, bvec)
    cols = lambda t: jnp.transpose(t.reshape(nb, N_HEADS, HEAD_DIM), (0, 2, 1))
    feature_major = lambda c: jnp.transpose(c, (0, 2, 3, 1)).reshape(nb, WIDTH, c.shape[1])
    ot = _sample_attention(cols(qs), cols(ks), cols(vs), feature_major(cache_k[l]), feature_major(cache_v[l]))
    obs_s = jnp.transpose(ot, (0, 2, 1)).reshape(nb, WIDTH)
    tmix_s = min(MIX_TILE, nb)
    hs, hn3s, route_s, counts_all = _mix(xs, oas, [obs_s], [], (), ga, gb, wout, g2, wr, br, expand,
                                         tri(tmix_s), counts)

    y_prompt, y_sample = _moe((h, hn3, route), (hs, hn3s, route_s), counts_all, wg, wu, wd)
    to5 = lambda t: t.reshape(1, nb, 1, N_HEADS, HEAD_DIM)
    return (y_prompt.reshape(batch, seq, D_MODEL), y_sample.reshape(nb, 1, D_MODEL),
            to_cache(kt_last), to_cache(vt_last), to5(ks), to5(vs), to5(vgs))
```

```python
import functools

import jax
import jax.numpy as jnp
from jax import lax
from jax.experimental import pallas as pl
from jax.experimental.pallas import tpu as pltpu

F32 = jnp.float32
BF16 = jnp.bfloat16
I32 = jnp.int32

D_MODEL = 1024
HEAD_DIM = 64
N_HEADS = 8
WIDTH = N_HEADS * HEAD_DIM
N_PAIRS = N_HEADS // 2
CHUNK = 128
N_WIN = 128
DILATIONS = (1, 4, 16)
MAX_WINDOW = 2048
N_GROUPS = 4
EXPERTS_PER_GROUP = 8
N_EXPERTS = N_GROUPS * EXPERTS_PER_GROUP
D_EXPERT = D_MODEL // 4
EPS = 1e-6

LANES = 128
SUBLANES = 8
VMEM_LIMIT = 48 * 1024 * 1024

PROJ_TILE = 512
ATTN_TILE = 512
MIX_TILE = 256
EXPERT_TILE = 256
SLOT_CHUNK = SUBLANES
SLOTS_PER_TILE = 768
ZERO_CHUNKS = 64


def _dot(a, b):
    return jnp.dot(a, b, preferred_element_type=F32)


def _gelu(x):
    return 0.5 * x * (1.0 + jnp.tanh(0.7978845608028654 * (x + 0.044715 * (x * x * x))))


def _head_rms(t, gain, bd):
    ss = _dot((t * t).astype(BF16), bd)
    return t * lax.rsqrt(ss * (1.0 / HEAD_DIM) + EPS) * gain


def _proj_common(x_ref, g1_ref, win_ref, gains_ref, bd_ref):
    x = x_ref[...]
    r = lax.rsqrt(jnp.mean(x * x, axis=-1, keepdims=True) + EPS)
    xn = (x * r * g1_ref[...]).astype(BF16)
    bd = bd_ref[...]
    q = _head_rms(_dot(xn, win_ref[:, 0:WIDTH]), gains_ref[0:1, :], bd) * (HEAD_DIM ** -0.5)
    k = _head_rms(_dot(xn, win_ref[:, WIDTH:2 * WIDTH]), gains_ref[1:2, :], bd)
    v = _dot(xn, win_ref[:, 2 * WIDTH:3 * WIDTH])
    u = _gelu(_dot(xn, win_ref[:, 3 * WIDTH:4 * WIDTH]))
    vg = _head_rms(_gelu(_dot(xn, win_ref[:, 4 * WIDTH:5 * WIDTH])), gains_ref[2:3, :], bd)
    return q, k, v, u, vg


def _proj_prompt_kernel(x_ref, g1_ref, win_ref, gains_ref, bd_ref, wcat_ref, bias_ref, *rest,
                        tiles_per_seq, first_kept_tile):
    stream_refs = rest[:3 * len(DILATIONS)]
    oa_ref, kl_ref, vl_ref, slab = rest[3 * len(DILATIONS):]
    q, k, v, u, vg = _proj_common(x_ref, g1_ref, win_ref, gains_ref, bd_ref)
    tm = x_ref.shape[0]
    for a, z in enumerate((q, k, v)):
        for p in range(N_PAIRS):
            slab[p] = z[:, p * LANES:(p + 1) * LANES]
        for c, dil in enumerate(DILATIONS):
            out_ref = stream_refs[3 * c + a]
            if dil == 1:
                out_ref[...] = z.astype(BF16)
                continue
            for r in range(dil):
                for p in range(N_PAIRS):
                    cols = slice(r * WIDTH + p * LANES, r * WIDTH + (p + 1) * LANES)
                    out_ref[:, cols] = slab[p, pl.ds(r, tm // dil, stride=dil), :].astype(BF16)

    @pl.when(pl.program_id(0) % tiles_per_seq >= first_kept_tile)
    def _():
        kl_ref[...] = k.T
        vl_ref[...] = v.T

    vgb = vg.astype(BF16)
    lane = lax.broadcasted_iota(I32, (CHUNK, LANES), 1)
    tm = x_ref.shape[0]
    for c in range(tm // CHUNK):
        rows = slice(c * CHUNK, (c + 1) * CHUNK)
        for p in range(N_PAIRS):
            cols = slice(p * LANES, (p + 1) * LANES)
            vp = vgb[rows, cols]
            zero = jnp.zeros_like(vp)
            rhs = jnp.concatenate([jnp.where(lane < HEAD_DIM, vp, zero),
                                   jnp.where(lane >= HEAD_DIM, vp, zero)], axis=0)
            mixed = _dot(wcat_ref[p], rhs) + bias_ref[p]
            oa_ref[rows, cols] = (u[rows, cols] * mixed).astype(BF16)


def _proj_sample_kernel(x_ref, g1_ref, win_ref, gains_ref, bd_ref, avec_ref, bvec_ref,
                        q_ref, k_ref, v_ref, vg_ref, oa_ref):
    q, k, v, u, vg = _proj_common(x_ref, g1_ref, win_ref, gains_ref, bd_ref)
    q_ref[...] = q
    k_ref[...] = k
    v_ref[...] = v
    vg_ref[...] = vg
    oa_ref[...] = (u * (avec_ref[...] * vg + bvec_ref[...])).astype(BF16)


def _const_spec(shape):
    return pl.BlockSpec(shape, lambda *_: (0,) * len(shape))


def _proj_prompt(x2, g1, win, gains, bd, wcat, bias, *, seq):
    n = x2.shape[0]
    tm = PROJ_TILE
    tiles_per_seq = seq // tm
    kept = min(MAX_WINDOW, seq)
    first_kept_tile = tiles_per_seq - kept // tm

    def kept_map(i):
        return (i // tiles_per_seq, 0, jnp.maximum(i % tiles_per_seq - first_kept_tile, 0))

    row_spec = pl.BlockSpec((tm, WIDTH), lambda i: (i, 0))
    kept_spec = pl.BlockSpec((None, WIDTH, tm), kept_map)
    stream_specs = [pl.BlockSpec((tm // dil, dil * WIDTH), lambda i: (i, 0)) for dil in DILATIONS for _ in range(3)]
    stream_shapes = [jax.ShapeDtypeStruct((n // dil, dil * WIDTH), BF16) for dil in DILATIONS for _ in range(3)]
    outs = pl.pallas_call(
        functools.partial(_proj_prompt_kernel, tiles_per_seq=tiles_per_seq, first_kept_tile=first_kept_tile),
        grid=(n // tm,),
        in_specs=[pl.BlockSpec((tm, D_MODEL), lambda i: (i, 0)),
                  _const_spec((1, D_MODEL)), _const_spec((D_MODEL, 5 * WIDTH)), _const_spec((3, WIDTH)),
                  _const_spec((WIDTH, WIDTH)), _const_spec((N_PAIRS, CHUNK, 2 * CHUNK)),
                  _const_spec((N_PAIRS, CHUNK, LANES))],
        out_specs=stream_specs + [row_spec, kept_spec, kept_spec],
        out_shape=stream_shapes + [jax.ShapeDtypeStruct((n, WIDTH), BF16)] +
                  [jax.ShapeDtypeStruct((n // seq, WIDTH, kept), F32)] * 2,
        scratch_shapes=[pltpu.VMEM((N_PAIRS, tm, LANES), F32)],
        compiler_params=pltpu.CompilerParams(dimension_semantics=("arbitrary",), vmem_limit_bytes=VMEM_LIMIT),
        name="proj_prompt",
    )(x2, g1, win, gains, bd, wcat, bias)
    n_streams = 3 * len(DILATIONS)
    qkv = [outs[3 * c:3 * c + 3] for c in range(len(DILATIONS))]
    return (qkv, *outs[n_streams:])


def _proj_sample(x2, g1, win, gains, bd, avec, bvec):
    n = x2.shape[0]
    full = _const_spec((n, WIDTH))
    return pl.pallas_call(
        _proj_sample_kernel,
        grid=(1,),
        in_specs=[_const_spec((n, D_MODEL)), _const_spec((1, D_MODEL)), _const_spec((D_MODEL, 5 * WIDTH)),
                  _const_spec((3, WIDTH)), _const_spec((WIDTH, WIDTH)), _const_spec((1, WIDTH)),
                  _const_spec((1, WIDTH))],
        out_specs=[full] * 5,
        out_shape=[jax.ShapeDtypeStruct((n, WIDTH), F32)] * 4 + [jax.ShapeDtypeStruct((n, WIDTH), BF16)],
        compiler_params=pltpu.CompilerParams(dimension_semantics=("arbitrary",), vmem_limit_bytes=VMEM_LIMIT),
        name="proj_sample",
    )(x2, g1, win, gains, bd, avec, bvec)


def _band_attn_kernel(q_ref, k_ref, v_ref, o_ref, lse_ref, kbuf, vbuf, *, tq):
    j = pl.program_id(2)
    nblk = tq // N_WIN

    @pl.when(j == 0)
    def _():
        kbuf[0:N_WIN, :] = jnp.zeros((N_WIN, WIDTH), BF16)
        vbuf[...] = jnp.ones(vbuf.shape, BF16)

    @pl.when(j > 0)
    def _():
        kbuf[0:N_WIN, :] = kbuf[tq:tq + N_WIN, :]
        vbuf[0:N_WIN, :] = vbuf[tq:tq + N_WIN, :]

    kbuf[N_WIN:N_WIN + tq, :] = k_ref[...]
    for p in range(N_PAIRS):
        vbuf[N_WIN:N_WIN + tq, 2 * p * LANES:(2 * p + 1) * LANES] = v_ref[:, p * LANES:(p + 1) * LANES]

    qi = lax.broadcasted_iota(I32, (N_WIN, 2 * N_WIN), 0)
    ki = lax.broadcasted_iota(I32, (N_WIN, 2 * N_WIN), 1)
    band = (ki >= qi) & (ki <= qi + N_WIN)
    lane = lax.broadcasted_iota(I32, (N_WIN, LANES), 1)
    low_half = lane < HEAD_DIM

    def block(jb, carry):
        q0 = pl.multiple_of(jb * N_WIN, N_WIN)
        valid = band & ((ki >= N_WIN) | (j * nblk + jb > 0))
        lse_tile = jnp.zeros((N_WIN, LANES), F32)
        for p in range(N_PAIRS):
            cols = slice(p * LANES, (p + 1) * LANES)
            qp = q_ref[pl.ds(q0, N_WIN), cols]
            kp = kbuf[pl.ds(q0, 2 * N_WIN), cols]
            vp = vbuf[pl.ds(q0, 2 * N_WIN), 2 * p * LANES:(2 * p + 2) * LANES]
            halves = []
            for hh in range(2):
                mine = low_half if hh == 0 else ~low_half
                qm = jnp.where(mine, qp, jnp.zeros_like(qp))
                s = lax.dot_general(qm, kp, (((1,), (1,)), ((), ())), preferred_element_type=F32)
                s = jnp.where(valid, s, -jnp.inf)
                m = jnp.max(s, axis=-1, keepdims=True)
                e = jnp.exp(s - m)
                ov = _dot(e.astype(BF16), vp)
                den = ov[:, LANES:2 * LANES]
                halves.append(ov[:, 0:LANES] / den)
                lse_tile = jnp.where(lane == 2 * p + hh, m + jnp.log(den), lse_tile)
            o_ref[pl.ds(q0, N_WIN), cols] = jnp.where(low_half, halves[0], halves[1]).astype(BF16)
        lse_ref[pl.ds(q0, N_WIN), :] = lse_tile
        return carry

    lax.fori_loop(0, nblk, block, 0)


def _band_attention(q2, k2, v2, *, batch, seq, dil):
    length = seq // dil
    tq = min(ATTN_TILE, length)
    view = lambda a: a.reshape(batch, length, a.shape[-1])
    spec = pl.BlockSpec((None, tq, WIDTH), lambda b, r, j: (b, j, r))
    lse_spec = pl.BlockSpec((None, tq, LANES), lambda b, r, j: (b, j, r))
    o, lse = pl.pallas_call(
        functools.partial(_band_attn_kernel, tq=tq),
        grid=(batch, dil, length // tq),
        in_specs=[spec, spec, spec],
        out_specs=[spec, lse_spec],
        out_shape=[jax.ShapeDtypeStruct((batch, length, dil * WIDTH), BF16),
                   jax.ShapeDtypeStruct((batch, length, dil * LANES), F32)],
        scratch_shapes=[pltpu.VMEM((N_WIN + tq, WIDTH), BF16), pltpu.VMEM((N_WIN + tq, 2 * WIDTH), BF16)],
        compiler_params=pltpu.CompilerParams(dimension_semantics=("arbitrary", "arbitrary", "arbitrary"),
                                             vmem_limit_bytes=VMEM_LIMIT),
        name=f"band_attn_d{dil}",
    )(view(q2), view(k2), view(v2))
    return o.reshape(batch * length, dil * WIDTH), lse.reshape(batch * length, dil * LANES)


def _sample_attn_kernel(qt_ref, knt_ref, vnt_ref, kt_ref, vt_ref, ot_ref):
    qt = qt_ref[...]
    vnt = vnt_ref[...]
    s_new = jnp.sum(qt * knt_ref[...], axis=0, keepdims=True)
    wb = kt_ref.shape[1]
    heads = range(N_HEADS)
    head_rows = lambda h: slice(h * HEAD_DIM, (h + 1) * HEAD_DIM)
    s = jnp.concatenate([jnp.sum(kt_ref[head_rows(h), :] * qt[:, h:h + 1], axis=0, keepdims=True) for h in heads],
                        axis=0)
    s0 = jnp.concatenate([s_new[:, h:h + 1] for h in heads], axis=0)
    pos = lax.broadcasted_iota(I32, (N_HEADS, wb), 1)
    es, e0s, dens, lses = [], [], [], []
    for dil in DILATIONS:
        lo = wb - N_WIN * dil
        sc = s[:, lo:]
        valid = (pos[:, lo:] & (dil - 1)) == 0
        m = jnp.maximum(jnp.max(jnp.where(valid, sc, -jnp.inf), axis=-1, keepdims=True), s0)
        e = jnp.where(valid, jnp.exp(sc - m), 0.0)
        e0 = jnp.exp(s0 - m)
        den = jnp.sum(e, axis=-1, keepdims=True) + e0
        es.append(e)
        e0s.append(e0)
        dens.append(den)
        lses.append(m + jnp.log(den))
    mm = jnp.maximum(jnp.maximum(lses[0], lses[1]), lses[2])
    ws = [jnp.exp(l - mm) for l in lses]
    tot = ws[0] + ws[1] + ws[2]
    coef = [w / (tot * den) for w, den in zip(ws, dens)]
    w1, w4, w16 = [e * c for e, c in zip(es, coef)]
    w_new = coef[0] * e0s[0] + coef[1] * e0s[1] + coef[2] * e0s[2]
    n1, n4 = N_WIN * DILATIONS[0], N_WIN * DILATIONS[1]
    w_pos = jnp.concatenate([w16[:, :wb - n4], w16[:, wb - n4:wb - n1] + w4[:, :n4 - n1],
                             w16[:, wb - n1:] + w4[:, n4 - n1:] + w1], axis=1)
    head_lane = lax.broadcasted_iota(I32, (HEAD_DIM, N_HEADS), 1)
    ot = jnp.zeros((HEAD_DIM, N_HEADS), F32)
    for h in heads:
        col = (jnp.sum(vt_ref[head_rows(h), :] * w_pos[h:h + 1, :], axis=-1, keepdims=True) +
               w_new[h:h + 1, :] * vnt[:, h:h + 1])
        ot = jnp.where(head_lane == h, col, ot)
    ot_ref[...] = ot


def _sample_attention(qt, knt, vnt, kt, vt):
    nb, _, wb = kt.shape
    assert wb == MAX_WINDOW, "cache window must cover every dilated key"
    tok = pl.BlockSpec((None, HEAD_DIM, N_HEADS), lambda b: (b, 0, 0))
    cache = pl.BlockSpec((None, WIDTH, wb), lambda b: (b, 0, 0))
    return pl.pallas_call(
        _sample_attn_kernel,
        grid=(nb,),
        in_specs=[tok, tok, tok, cache, cache],
        out_specs=tok,
        out_shape=jax.ShapeDtypeStruct((nb, HEAD_DIM, N_HEADS), F32),
        compiler_params=pltpu.CompilerParams(dimension_semantics=("arbitrary",), vmem_limit_bytes=VMEM_LIMIT),
        name="sample_attn",
    )(qt, knt, vnt, kt, vt)


def _split_hi_lo(w):
    hi = w.astype(BF16)
    return jnp.concatenate([hi, (w - hi.astype(F32)).astype(BF16)], axis=-1)


def _token_order(ref, dil, width, slab, tm):
    if dil == 1:
        return ref[...].astype(F32)
    n_slabs = width // LANES
    for r in range(dil):
        for p in range(n_slabs):
            cols = slice(r * width + p * LANES, r * width + (p + 1) * LANES)
            slab[p, pl.ds(r, tm // dil, stride=dil), :] = ref[:, cols].astype(F32)
    return jnp.concatenate([slab[p] for p in range(n_slabs)], axis=1)


def _mix_kernel(*refs, dils):
    n_cfg = max(len(dils), 1)
    n_lse = len(dils)
    x_ref, oa_ref = refs[0], refs[1]
    o_refs = refs[2:2 + n_cfg]
    lse_refs = refs[2 + n_cfg:2 + n_cfg + n_lse]
    (ga_ref, gb_ref, wout_ref, g2_ref, wr_ref, br_ref, expand_ref, tri_ref, upper_ref,
     h_ref, xs_ref, route_ref, nchunk_ref, chunk0_ref, slab) = refs[2 + n_cfg + n_lse:]
    tm = x_ref.shape[0]

    if not dils:
        ob = o_refs[0][...].astype(F32)
    else:
        lses = [_token_order(r, dil, LANES, slab, tm) for r, dil in zip(lse_refs, dils)]
        mm = functools.reduce(jnp.maximum, lses)
        ws = [jnp.exp(l - mm) for l in lses]
        tot = functools.reduce(lambda a, b: a + b, ws)
        ob = jnp.zeros((tm, WIDTH), F32)
        for w, o_ref, dil in zip(ws, o_refs, dils):
            ob = ob + _dot(_split_hi_lo(w / tot), expand_ref[...]) * _token_order(o_ref, dil, WIDTH, slab, tm)

    oa = oa_ref[...].astype(F32)
    ya = oa * lax.rsqrt(jnp.mean(oa * oa, axis=-1, keepdims=True) + EPS) * ga_ref[...]
    yb = ob * lax.rsqrt(jnp.mean(ob * ob, axis=-1, keepdims=True) + EPS) * gb_ref[...]
    cat = jnp.concatenate([ya, yb], axis=-1).astype(BF16)
    h = x_ref[...] + _dot(cat, wout_ref[...])
    h_ref[...] = h
    hn = (h * lax.rsqrt(jnp.mean(h * h, axis=-1, keepdims=True) + EPS) * g2_ref[...]).astype(BF16)

    logits = _dot(hn, wr_ref[...]) + br_ref[...]
    lane = lax.broadcasted_iota(I32, (tm, LANES), 1)
    lane_f = lane.astype(F32)
    neg = -jnp.inf
    big = float(LANES)
    coarse = (lane >= N_EXPERTS) & (lane < N_EXPERTS + N_GROUPS)
    lg = jnp.where(coarse, logits, neg)
    mx = jnp.max(lg, axis=-1, keepdims=True)
    g_lane = jnp.min(jnp.where(lg == mx, lane_f, big), axis=-1, keepdims=True)
    p_star = 1.0 / jnp.sum(jnp.exp(lg - mx), axis=-1, keepdims=True)
    lo = (g_lane - float(N_EXPERTS)) * float(EXPERTS_PER_GROUP)
    lf = jnp.where((lane_f >= lo) & (lane_f < lo + float(EXPERTS_PER_GROUP)), logits, neg)
    v1 = jnp.max(lf, axis=-1, keepdims=True)
    i1 = jnp.min(jnp.where(lf == v1, lane_f, big), axis=-1, keepdims=True)
    lf2 = jnp.where(lane_f == i1, neg, lf)
    v2 = jnp.max(lf2, axis=-1, keepdims=True)
    i2 = jnp.min(jnp.where(lf2 == v2, lane_f, big), axis=-1, keepdims=True)
    e21 = jnp.exp(v2 - v1)
    w1 = p_star / (1.0 + e21)
    w2 = p_star * e21 / (1.0 + e21)

    sel1 = lane_f == i1
    sel2 = lane_f == i2
    onehot = jnp.where(sel1 | sel2, 1.0, 0.0)
    rank = _dot(tri_ref[...], onehot.astype(BF16))
    count = jnp.sum(onehot, axis=0, keepdims=True)
    chunks = jnp.floor((count + float(SLOT_CHUNK - 1)) * (1.0 / SLOT_CHUNK))
    chunk0 = _dot(jnp.broadcast_to(chunks, (SUBLANES, LANES)).astype(BF16), upper_ref[...])[0:1, :]
    slot = rank + chunk0 * float(SLOT_CHUNK)
    s1 = jnp.sum(jnp.where(sel1, slot, 0.0), axis=-1, keepdims=True)
    s2 = jnp.sum(jnp.where(sel2, slot, 0.0), axis=-1, keepdims=True)
    i = pl.program_id(0)
    nchunk_ref[pl.ds(i, 1), :] = chunks.astype(I32)
    chunk0_ref[pl.ds(i, 1), :] = chunk0.astype(I32)

    s1_row = jnp.broadcast_to(s1, (tm, LANES)).T[0:1, :]
    s2_row = jnp.broadcast_to(s2, (tm, LANES)).T[0:1, :]
    slot_id = lax.broadcasted_iota(I32, (SLOTS_PER_TILE, tm), 0).astype(F32)
    perm = jnp.where((slot_id == s1_row) | (slot_id == s2_row), 1.0, 0.0).astype(BF16)
    xs_ref[...] = _dot(perm, hn)

    col = lax.broadcasted_iota(I32, (tm, SUBLANES), 1)
    route = jnp.zeros((tm, SUBLANES), F32)
    for idx, val in enumerate((s1, s2, w1, w2)):
        route = jnp.where(col == idx, val, route)
    route_ref[...] = route


def _mix(x2, oa, obs, lses, dils, ga, gb, wout, g2, wr, br, expand, tri, upper):
    n = x2.shape[0]
    tm = min(MIX_TILE, n)
    n_tiles = n // tm
    row = lambda w: pl.BlockSpec((tm, w), lambda i: (i, 0))
    stream = lambda w, dil: pl.BlockSpec((tm // dil, dil * w), lambda i: (i, 0))
    in_specs = ([row(D_MODEL), row(WIDTH)] + [stream(WIDTH, dil) for dil in (dils or (1,))] +
                [stream(LANES, dil) for dil in dils] +
                [_const_spec((1, WIDTH)), _const_spec((1, WIDTH)), _const_spec((D_MODEL, D_MODEL)),
                 _const_spec((1, D_MODEL)), _const_spec((D_MODEL, LANES)), _const_spec((1, LANES)),
                 _const_spec((2 * LANES, WIDTH)), _const_spec((tm, tm)), _const_spec((LANES, LANES))])
    return pl.pallas_call(
        functools.partial(_mix_kernel, dils=tuple(dils)),
        grid=(n_tiles,),
        in_specs=in_specs,
        out_specs=[row(D_MODEL), pl.BlockSpec((SLOTS_PER_TILE, D_MODEL), lambda i: (i, 0)), row(SUBLANES),
                   _const_spec((n_tiles, LANES)), _const_spec((n_tiles, LANES))],
        out_shape=[jax.ShapeDtypeStruct((n, D_MODEL), F32),
                   jax.ShapeDtypeStruct((n_tiles * SLOTS_PER_TILE, D_MODEL), F32),
                   jax.ShapeDtypeStruct((n, SUBLANES), F32),
                   jax.ShapeDtypeStruct((n_tiles, LANES), I32), jax.ShapeDtypeStruct((n_tiles, LANES), I32)],
        scratch_shapes=[pltpu.VMEM((N_PAIRS, tm, LANES), F32)],
        compiler_params=pltpu.CompilerParams(dimension_semantics=("arbitrary",), vmem_limit_bytes=VMEM_LIMIT),
        name="mix",
    )(x2, oa, *obs, *lses, ga, gb, wout, g2, wr, br, expand, tri, upper)


def _chunk_copy(a_hbm, b_hbm, rows_a, row, buf, slot, sem, *, to_hbm):
    vm = buf.at[pl.ds(pl.multiple_of(slot * SLOT_CHUNK, SLOT_CHUNK), SLOT_CHUNK), :]

    def go(hbm, r):
        hb = hbm.at[pl.ds(pl.multiple_of(r, SLOT_CHUNK), SLOT_CHUNK), :]
        (pltpu.make_async_copy(vm, hb, sem) if to_hbm else pltpu.make_async_copy(hb, vm, sem)).start()

    pl.when(row < rows_a)(functools.partial(go, a_hbm, row))
    pl.when(row >= rows_a)(functools.partial(go, b_hbm, row - rows_a))


def _grouped_expert_kernel(te_ref, nch_ref, nck_ref, ck0_ref, xa_hbm, xb_hbm, wg_ref, wu_ref, wd_ref,
                           ya_hbm, yb_hbm, xbuf, obuf, zbuf, wgu, wdn, rows_smem, cursor, gsem, ssem, zsem):
    g = pl.program_id(0)
    n_steps = pl.num_programs(0)
    rows_a = xa_hbm.shape[0]
    tiles_total = nck_ref.shape[0] // N_EXPERTS
    chunks_per_tile = EXPERT_TILE // SLOT_CHUNK
    chunk_bytes_ref = xbuf.at[pl.ds(0, SLOT_CHUNK), :]

    def gather(step):
        half = step % 2
        e = te_ref[step]
        fresh = (step == 0) | (e != te_ref[jnp.maximum(step - 1, 0)])
        t0 = jnp.where(fresh, 0, cursor[0])
        k0 = jnp.where(fresh, 0, cursor[1])

        def more(state):
            return (state[2] < nch_ref[step]) & (state[0] < tiles_total)

        def walk(state):
            t, k, filled = state
            have = k < nck_ref[t * N_EXPERTS + e]

            @pl.when(have)
            def _():
                row = t * SLOTS_PER_TILE + (ck0_ref[t * N_EXPERTS + e] + k) * SLOT_CHUNK
                rows_smem[half, filled] = row
                _chunk_copy(xa_hbm, xb_hbm, rows_a, row, xbuf, half * chunks_per_tile + filled, gsem.at[half],
                            to_hbm=False)

            return (jnp.where(have, t, t + 1), jnp.where(have, k + 1, 0), jnp.where(have, filled + 1, filled))

        t1, k1, _ = lax.while_loop(more, walk, (t0, k0, jnp.int32(0)))
        cursor[0] = t1
        cursor[1] = k1

    def wait_chunks(n, sem):
        def one(_, c):
            pltpu.make_async_copy(chunk_bytes_ref, chunk_bytes_ref, sem).wait()
            return c

        lax.fori_loop(0, n, one, 0)

    @pl.when(g == 0)
    def _():
        xbuf[...] = jnp.zeros_like(xbuf)
        zbuf[...] = jnp.zeros_like(zbuf)
        gather(0)

    pl.when(g + 1 < n_steps)(functools.partial(gather, g + 1))

    @pl.when(g < tiles_total)
    def _():
        last = g * N_EXPERTS + N_EXPERTS - 1
        used = ck0_ref[last] + nck_ref[last]
        n_free = SLOTS_PER_TILE // SLOT_CHUNK - used
        bit = zbuf.shape[0] // SLOT_CHUNK
        while bit >= 1:
            first = used + (n_free & ~(2 * bit - 1))
            row = g * SLOTS_PER_TILE + first * SLOT_CHUNK

            @pl.when((n_free & bit) != 0)
            def _(row=row, bit=bit):
                src = zbuf.at[pl.ds(0, bit * SLOT_CHUNK), :]

                def go(hbm, r):
                    cp = pltpu.make_async_copy(
                        src, hbm.at[pl.ds(pl.multiple_of(r, SLOT_CHUNK), bit * SLOT_CHUNK), :], zsem)
                    cp.start()
                    cp.wait()

                pl.when(row < rows_a)(functools.partial(go, ya_hbm, row))
                pl.when(row >= rows_a)(functools.partial(go, yb_hbm, row - rows_a))

            bit //= 2

    @pl.when((g == 0) | (te_ref[g] != te_ref[jnp.maximum(g - 1, 0)]))
    def _():
        wgu[:, 0:D_EXPERT] = wg_ref[...].astype(BF16)
        wgu[:, D_EXPERT:2 * D_EXPERT] = wu_ref[...].astype(BF16)
        wdn[...] = wd_ref[...].astype(BF16)

    half = g % 2
    n_chunks = nch_ref[g]
    n_prev = jnp.where(g > 0, nch_ref[jnp.maximum(g - 1, 0)], 0)
    wait_chunks(n_chunks, gsem.at[half])

    @pl.when(n_chunks > 0)
    def _():
        x = xbuf[pl.ds(pl.multiple_of(half * EXPERT_TILE, EXPERT_TILE), EXPERT_TILE), :].astype(BF16)
        ab = _dot(x, wgu[...])
        a = ab[:, 0:D_EXPERT]
        hid = (a * jax.nn.sigmoid(a)) * ab[:, D_EXPERT:2 * D_EXPERT]
        out = _dot(hid.astype(BF16), wdn[...])
        wait_chunks(n_prev, ssem)
        obuf[...] = out

        def scatter(c, carry):
            _chunk_copy(ya_hbm, yb_hbm, rows_a, rows_smem[half, c], obuf, c, ssem, to_hbm=True)
            return carry

        lax.fori_loop(0, n_chunks, scatter, 0)

    pl.when(n_chunks == 0)(functools.partial(wait_chunks, n_prev, ssem))
    pl.when(g == n_steps - 1)(functools.partial(wait_chunks, n_chunks, ssem))


def _grouped_experts(tile_expert, tile_chunks, nck, ck0, xs_a, xs_b, wg, wu, wd):
    n_steps = tile_expert.shape[0]
    weights = lambda shape: pl.BlockSpec((None,) + shape, lambda g, te, *_: (te[g], 0, 0))
    any_spec = pl.BlockSpec(memory_space=pl.ANY)
    chunks_per_tile = EXPERT_TILE // SLOT_CHUNK
    return pl.pallas_call(
        _grouped_expert_kernel,
        grid_spec=pltpu.PrefetchScalarGridSpec(
            num_scalar_prefetch=4, grid=(n_steps,),
            in_specs=[any_spec, any_spec, weights((D_MODEL, D_EXPERT)), weights((D_MODEL, D_EXPERT)),
                      weights((D_EXPERT, D_MODEL))],
            out_specs=[any_spec, any_spec],
            scratch_shapes=[pltpu.VMEM((2 * EXPERT_TILE, D_MODEL), F32), pltpu.VMEM((EXPERT_TILE, D_MODEL), F32),
                            pltpu.VMEM((ZERO_CHUNKS * SLOT_CHUNK, D_MODEL), F32),
                            pltpu.VMEM((D_MODEL, 2 * D_EXPERT), BF16), pltpu.VMEM((D_EXPERT, D_MODEL), BF16),
                            pltpu.SMEM((2, chunks_per_tile), I32), pltpu.SMEM((2,), I32),
                            pltpu.SemaphoreType.DMA((2,)), pltpu.SemaphoreType.DMA, pltpu.SemaphoreType.DMA]),
        out_shape=[jax.ShapeDtypeStruct(xs_a.shape, F32), jax.ShapeDtypeStruct(xs_b.shape, F32)],
        compiler_params=pltpu.CompilerParams(dimension_semantics=("arbitrary",), vmem_limit_bytes=VMEM_LIMIT),
        name="experts",
    )(tile_expert, tile_chunks, nck, ck0, xs_a, xs_b, wg, wu, wd)


def _unsort_kernel(h_ref, route_ref, ys_ref, y_ref):
    tm = h_ref.shape[0]
    route = route_ref[...]
    slot_id = lax.broadcasted_iota(I32, (tm, SLOTS_PER_TILE), 1).astype(F32)
    sel = (jnp.where(slot_id == route[:, 0:1], route[:, 2:3], 0.0) +
           jnp.where(slot_id == route[:, 1:2], route[:, 3:4], 0.0))
    y_ref[...] = h_ref[...] + _dot(sel.astype(BF16), ys_ref[...].astype(BF16))


def _unsort(h, route, ys):
    n = h.shape[0]
    tm = min(MIX_TILE, n)
    return pl.pallas_call(
        _unsort_kernel,
        grid=(n // tm,),
        in_specs=[pl.BlockSpec((tm, D_MODEL), lambda i: (i, 0)), pl.BlockSpec((tm, SUBLANES), lambda i: (i, 0)),
                  pl.BlockSpec((SLOTS_PER_TILE, D_MODEL), lambda i: (i, 0))],
        out_specs=pl.BlockSpec((tm, D_MODEL), lambda i: (i, 0)),
        out_shape=jax.ShapeDtypeStruct((n, D_MODEL), F32),
        compiler_params=pltpu.CompilerParams(dimension_semantics=("arbitrary",), vmem_limit_bytes=VMEM_LIMIT),
        name="combine",
    )(h, route, ys)


def _grouped_moe(part_a, part_b, wg, wu, wd):
    chunks_per_tile = EXPERT_TILE // SLOT_CHUNK
    nck = jnp.concatenate([part_a[3], part_b[3]])[:, :N_EXPERTS]
    ck0 = jnp.concatenate([part_a[4], part_b[4]])[:, :N_EXPERTS]
    tiles_total = nck.shape[0]
    per_expert = jnp.sum(nck, axis=0)
    tiles_e = (per_expert + chunks_per_tile - 1) // chunks_per_tile
    ends = jnp.cumsum(tiles_e)
    max_chunks = tiles_total * (SLOTS_PER_TILE // SLOT_CHUNK)
    n_steps = max_chunks // chunks_per_tile + N_EXPERTS
    step = jnp.arange(n_steps, dtype=I32)
    tile_expert = jnp.minimum(jnp.sum((step[:, None] >= ends[None, :]).astype(I32), axis=1), N_EXPERTS - 1)
    first_step = (ends - tiles_e)[tile_expert]
    tile_chunks = jnp.clip(per_expert[tile_expert] - (step - first_step) * chunks_per_tile, 0, chunks_per_tile)
    tile_chunks = jnp.where(step < ends[N_EXPERTS - 1], tile_chunks, 0).astype(I32)
    ys_a, ys_b = _grouped_experts(tile_expert, tile_chunks, nck.reshape(-1), ck0.reshape(-1),
                                  part_a[1], part_b[1], wg, wu, wd)
    return _unsort(part_a[0], part_a[2], ys_a), _unsort(part_b[0], part_b[2], ys_b)


def kernel(x_prompt, x_sample, cache_k, cache_v, norm1_g, w_in, q_gain, k_gain, v_gain, w_spatial, b_spatial,
           out_gain_a, out_gain_b, w_out, norm2_g, w_router1, b_router1, w_router2, b_router2, w_up, w_gate,
           w_down):
    depth = norm1_g.shape[0]
    assert depth == 1
    l = 0
    batch, seq, _ = x_prompt.shape
    nb, dec_seq, _ = x_sample.shape
    assert dec_seq == 1

    g1 = norm1_g[l].reshape(1, D_MODEL)
    win = w_in[l].astype(BF16)
    gains = jnp.stack([q_gain[l].reshape(WIDTH), k_gain[l].reshape(WIDTH), v_gain[l].reshape(WIDTH)])
    head_of_lane = jnp.arange(WIDTH) // HEAD_DIM
    bd = (head_of_lane[:, None] == head_of_lane[None, :]).astype(BF16)
    ws_tril = jnp.tril(w_spatial[l])
    wcat = jnp.concatenate([ws_tril[0::2], ws_tril[1::2]], axis=-1).astype(BF16)
    bs = b_spatial[l]
    bias = jnp.where(jnp.arange(LANES)[None, None, :] < HEAD_DIM, bs[0::2][:, :, None], bs[1::2][:, :, None])
    avec = jnp.repeat(w_spatial[l][:, 0, 0], HEAD_DIM).reshape(1, WIDTH)
    bvec = jnp.repeat(bs[:, 0], HEAD_DIM).reshape(1, WIDTH)
    ga = out_gain_a[l].reshape(1, WIDTH)
    gb = out_gain_b[l].reshape(1, WIDTH)
    wout = w_out[l].astype(BF16)
    g2 = norm2_g[l].reshape(1, D_MODEL)
    wr = jnp.zeros((D_MODEL, LANES), F32)
    wr = wr.at[:, :N_EXPERTS].set(jnp.transpose(w_router2[l], (1, 0, 2)).reshape(D_MODEL, N_EXPERTS))
    wr = wr.at[:, N_EXPERTS:N_EXPERTS + N_GROUPS].set(w_router1[l]).astype(BF16)
    br = jnp.zeros((1, LANES), F32)
    br = br.at[0, :N_EXPERTS].set(b_router2[l].reshape(N_EXPERTS))
    br = br.at[0, N_EXPERTS:N_EXPERTS + N_GROUPS].set(b_router1[l])
    lane_head = (jnp.arange(LANES)[:, None] == head_of_lane[None, :]).astype(BF16)
    expand = jnp.concatenate([lane_head, lane_head], axis=0)
    wg = w_gate[l].reshape(N_EXPERTS, D_MODEL, D_EXPERT)
    wu = w_up[l].reshape(N_EXPERTS, D_MODEL, D_EXPERT)
    wd = w_down[l].reshape(N_EXPERTS, D_EXPERT, D_MODEL)

    def tri(t):
        return (jnp.arange(t)[:, None] > jnp.arange(t)[None, :]).astype(BF16)

    xp = x_prompt.reshape(batch * seq, D_MODEL)
    qkv, oa, kt_last, vt_last = _proj_prompt(xp, g1, win, gains, bd, wcat, bias, seq=seq)
    obs, lses = [], []
    for (q, k, v), dil in zip(qkv, DILATIONS):
        o, lse = _band_attention(q, k, v, batch=batch, seq=seq, dil=dil)
        obs.append(o)
        lses.append(lse)
    tmix = min(MIX_TILE, batch * seq)
    upper = (jnp.arange(LANES)[:, None] < jnp.arange(LANES)[None, :]).astype(BF16)
    part_p = _mix(xp, oa, obs, lses, DILATIONS, ga, gb, wout, g2, wr, br, expand, tri(tmix), upper)
    kept = min(MAX_WINDOW, seq)
    to_cache = lambda t: jnp.transpose(t.reshape(1, batch, N_HEADS, HEAD_DIM, kept), (0, 1, 4, 2, 3))

    xs = x_sample.reshape(nb, D_MODEL)
    qs, ks, vs, vgs, oas = _proj_sample(xs, g1, win, gains, bd, avec, bvec)
    cols = lambda t: jnp.transpose(t.reshape(nb, N_HEADS, HEAD_DIM), (0, 2, 1))
    feature_major = lambda c: jnp.transpose(c, (0, 2, 3, 1)).reshape(nb, WIDTH, c.shape[1])
    ot = _sample_attention(cols(qs), cols(ks), cols(vs), feature_major(cache_k[l]), feature_major(cache_v[l]))
    obs_s = jnp.transpose(ot, (0, 2, 1)).reshape(nb, WIDTH)
    tmix_s = min(MIX_TILE, nb)
    part_s = _mix(xs, oas, [obs_s], [], (), ga, gb, wout, g2, wr, br, expand, tri(tmix_s), upper)

    y_prompt, y_sample = _grouped_moe(part_p, part_s, wg, wu, wd)
    to5 = lambda t: t.reshape(1, nb, 1, N_HEADS, HEAD_DIM)
    return (y_prompt.reshape(batch, seq, D_MODEL), y_sample.reshape(nb, 1, D_MODEL),
            to_cache(kt_last), to_cache(vt_last), to5(ks), to5(vs), to5(vgs))
```

```python
import functools

import jax
import jax.numpy as jnp
from jax import lax
from jax.experimental import pallas as pl
from jax.experimental.pallas import tpu as pltpu

F32 = jnp.float32
BF16 = jnp.bfloat16
I32 = jnp.int32

D_MODEL = 1024
HEAD_DIM = 64
N_HEADS = 8
WIDTH = N_HEADS * HEAD_DIM
N_PAIRS = N_HEADS // 2
CHUNK = 128
N_WIN = 128
DILATIONS = (1, 4, 16)
MAX_WINDOW = 2048
N_GROUPS = 4
EXPERTS_PER_GROUP = 8
N_EXPERTS = N_GROUPS * EXPERTS_PER_GROUP
D_EXPERT = D_MODEL // 4
EPS = 1e-6

LANES = 128
SUBLANES = 8
VMEM_LIMIT = 48 * 1024 * 1024

PROJ_TILE = 512
ATTN_TILE = 512
MIX_TILE = 256
EXPERT_TILE = 256
SLOT_CHUNK = SUBLANES
SLOTS_PER_TILE = 768
ZERO_CHUNKS = 64


def _dot(a, b):
    return jnp.dot(a, b, preferred_element_type=F32)


def _gelu(x):
    return 0.5 * x * (1.0 + jnp.tanh(0.7978845608028654 * (x + 0.044715 * (x * x * x))))


def _head_rms(t, gain, bd):
    ss = _dot((t * t).astype(BF16), bd)
    return t * lax.rsqrt(ss * (1.0 / HEAD_DIM) + EPS) * gain


def _proj_common(x_ref, g1_ref, win_ref, gains_ref, bd_ref):
    x = x_ref[...]
    r = lax.rsqrt(jnp.mean(x * x, axis=-1, keepdims=True) + EPS)
    xn = (x * r * g1_ref[...]).astype(BF16)
    bd = bd_ref[...]
    q = _head_rms(_dot(xn, win_ref[:, 0:WIDTH]), gains_ref[0:1, :], bd) * (HEAD_DIM ** -0.5)
    k = _head_rms(_dot(xn, win_ref[:, WIDTH:2 * WIDTH]), gains_ref[1:2, :], bd)
    v = _dot(xn, win_ref[:, 2 * WIDTH:3 * WIDTH])
    u = _gelu(_dot(xn, win_ref[:, 3 * WIDTH:4 * WIDTH]))
    vg = _head_rms(_gelu(_dot(xn, win_ref[:, 4 * WIDTH:5 * WIDTH])), gains_ref[2:3, :], bd)
    return q, k, v, u, vg


def _proj_prompt_kernel(x_ref, g1_ref, win_ref, gains_ref, bd_ref, wcat_ref, bias_ref, *rest,
                        tiles_per_seq, first_kept_tile):
    stream_refs = rest[:3 * len(DILATIONS)]
    oa_ref, kl_ref, vl_ref, slab = rest[3 * len(DILATIONS):]
    q, k, v, u, vg = _proj_common(x_ref, g1_ref, win_ref, gains_ref, bd_ref)
    tm = x_ref.shape[0]
    for a, z in enumerate((q, k, v)):
        for p in range(N_PAIRS):
            slab[p] = z[:, p * LANES:(p + 1) * LANES]
        for c, dil in enumerate(DILATIONS):
            out_ref = stream_refs[3 * c + a]
            if dil == 1:
                out_ref[...] = z.astype(BF16)
                continue
            for r in range(dil):
                for p in range(N_PAIRS):
                    cols = slice(r * WIDTH + p * LANES, r * WIDTH + (p + 1) * LANES)
                    out_ref[:, cols] = slab[p, pl.ds(r, tm // dil, stride=dil), :].astype(BF16)

    @pl.when(pl.program_id(0) % tiles_per_seq >= first_kept_tile)
    def _():
        kl_ref[...] = k.T
        vl_ref[...] = v.T

    vgb = vg.astype(BF16)
    lane = lax.broadcasted_iota(I32, (CHUNK, LANES), 1)
    tm = x_ref.shape[0]
    for c in range(tm // CHUNK):
        rows = slice(c * CHUNK, (c + 1) * CHUNK)
        for p in range(N_PAIRS):
            cols = slice(p * LANES, (p + 1) * LANES)
            vp = vgb[rows, cols]
            zero = jnp.zeros_like(vp)
            rhs = jnp.concatenate([jnp.where(lane < HEAD_DIM, vp, zero),
                                   jnp.where(lane >= HEAD_DIM, vp, zero)], axis=0)
            mixed = _dot(wcat_ref[p], rhs) + bias_ref[p]
            oa_ref[rows, cols] = (u[rows, cols] * mixed).astype(BF16)


def _proj_sample_kernel(x_ref, g1_ref, win_ref, gains_ref, bd_ref, avec_ref, bvec_ref,
                        q_ref, k_ref, v_ref, vg_ref, oa_ref):
    q, k, v, u, vg = _proj_common(x_ref, g1_ref, win_ref, gains_ref, bd_ref)
    q_ref[...] = q
    k_ref[...] = k
    v_ref[...] = v
    vg_ref[...] = vg
    oa_ref[...] = (u * (avec_ref[...] * vg + bvec_ref[...])).astype(BF16)


def _const_spec(shape):
    return pl.BlockSpec(shape, lambda *_: (0,) * len(shape))


def _proj_prompt(x2, g1, win, gains, bd, wcat, bias, *, seq):
    n = x2.shape[0]
    tm = PROJ_TILE
    tiles_per_seq = seq // tm
    kept = min(MAX_WINDOW, seq)
    first_kept_tile = tiles_per_seq - kept // tm

    def kept_map(i):
        return (i // tiles_per_seq, 0, jnp.maximum(i % tiles_per_seq - first_kept_tile, 0))

    row_spec = pl.BlockSpec((tm, WIDTH), lambda i: (i, 0))
    kept_spec = pl.BlockSpec((None, WIDTH, tm), kept_map)
    stream_specs = [pl.BlockSpec((tm // dil, dil * WIDTH), lambda i: (i, 0)) for dil in DILATIONS for _ in range(3)]
    stream_shapes = [jax.ShapeDtypeStruct((n // dil, dil * WIDTH), BF16) for dil in DILATIONS for _ in range(3)]
    outs = pl.pallas_call(
        functools.partial(_proj_prompt_kernel, tiles_per_seq=tiles_per_seq, first_kept_tile=first_kept_tile),
        grid=(n // tm,),
        in_specs=[pl.BlockSpec((tm, D_MODEL), lambda i: (i, 0)),
                  _const_spec((1, D_MODEL)), _const_spec((D_MODEL, 5 * WIDTH)), _const_spec((3, WIDTH)),
                  _const_spec((WIDTH, WIDTH)), _const_spec((N_PAIRS, CHUNK, 2 * CHUNK)),
                  _const_spec((N_PAIRS, CHUNK, LANES))],
        out_specs=stream_specs + [row_spec, kept_spec, kept_spec],
        out_shape=stream_shapes + [jax.ShapeDtypeStruct((n, WIDTH), BF16)] +
                  [jax.ShapeDtypeStruct((n // seq, WIDTH, kept), F32)] * 2,
        scratch_shapes=[pltpu.VMEM((N_PAIRS, tm, LANES), F32)],
        compiler_params=pltpu.CompilerParams(dimension_semantics=("arbitrary",), vmem_limit_bytes=VMEM_LIMIT),
        name="proj_prompt",
    )(x2, g1, win, gains, bd, wcat, bias)
    n_streams = 3 * len(DILATIONS)
    qkv = [outs[3 * c:3 * c + 3] for c in range(len(DILATIONS))]
    return (qkv, *outs[n_streams:])


def _proj_sample(x2, g1, win, gains, bd, avec, bvec):
    n = x2.shape[0]
    full = _const_spec((n, WIDTH))
    return pl.pallas_call(
        _proj_sample_kernel,
        grid=(1,),
        in_specs=[_const_spec((n, D_MODEL)), _const_spec((1, D_MODEL)), _const_spec((D_MODEL, 5 * WIDTH)),
                  _const_spec((3, WIDTH)), _const_spec((WIDTH, WIDTH)), _const_spec((1, WIDTH)),
                  _const_spec((1, WIDTH))],
        out_specs=[full] * 5,
        out_shape=[jax.ShapeDtypeStruct((n, WIDTH), F32)] * 4 + [jax.ShapeDtypeStruct((n, WIDTH), BF16)],
        compiler_params=pltpu.CompilerParams(dimension_semantics=("arbitrary",), vmem_limit_bytes=VMEM_LIMIT),
        name="proj_sample",
    )(x2, g1, win, gains, bd, avec, bvec)


def _band_attn_kernel(q_ref, k_ref, v_ref, o_ref, lse_ref, kbuf, vbuf, *, tq):
    j = pl.program_id(2)
    nblk = tq // N_WIN

    @pl.when(j == 0)
    def _():
        kbuf[0:N_WIN, :] = jnp.zeros((N_WIN, WIDTH), BF16)
        vbuf[...] = jnp.ones(vbuf.shape, BF16)

    @pl.when(j > 0)
    def _():
        kbuf[0:N_WIN, :] = kbuf[tq:tq + N_WIN, :]
        vbuf[0:N_WIN, :] = vbuf[tq:tq + N_WIN, :]

    kbuf[N_WIN:N_WIN + tq, :] = k_ref[...]
    for p in range(N_PAIRS):
        vbuf[N_WIN:N_WIN + tq, 2 * p * LANES:(2 * p + 1) * LANES] = v_ref[:, p * LANES:(p + 1) * LANES]

    qi = lax.broadcasted_iota(I32, (N_WIN, 2 * N_WIN), 0)
    ki = lax.broadcasted_iota(I32, (N_WIN, 2 * N_WIN), 1)
    band = (ki >= qi) & (ki <= qi + N_WIN)
    lane = lax.broadcasted_iota(I32, (N_WIN, LANES), 1)
    low_half = lane < HEAD_DIM

    def block(jb, carry):
        q0 = pl.multiple_of(jb * N_WIN, N_WIN)
        valid = band & ((ki >= N_WIN) | (j * nblk + jb > 0))
        lse_tile = jnp.zeros((N_WIN, LANES), F32)
        for p in range(N_PAIRS):
            cols = slice(p * LANES, (p + 1) * LANES)
            qp = q_ref[pl.ds(q0, N_WIN), cols]
            kp = kbuf[pl.ds(q0, 2 * N_WIN), cols]
            vp = vbuf[pl.ds(q0, 2 * N_WIN), 2 * p * LANES:(2 * p + 2) * LANES]
            halves = []
            for hh in range(2):
                mine = low_half if hh == 0 else ~low_half
                qm = jnp.where(mine, qp, jnp.zeros_like(qp))
                s = lax.dot_general(qm, kp, (((1,), (1,)), ((), ())), preferred_element_type=F32)
                s = jnp.where(valid, s, -jnp.inf)
                m = jnp.max(s, axis=-1, keepdims=True)
                e = jnp.exp(s - m)
                ov = _dot(e.astype(BF16), vp)
                den = ov[:, LANES:2 * LANES]
                halves.append(ov[:, 0:LANES] / den)
                lse_tile = jnp.where(lane == 2 * p + hh, m + jnp.log(den), lse_tile)
            o_ref[pl.ds(q0, N_WIN), cols] = jnp.where(low_half, halves[0], halves[1]).astype(BF16)
        lse_ref[pl.ds(q0, N_WIN), :] = lse_tile
        return carry

    lax.fori_loop(0, nblk, block, 0)


def _band_attention(q2, k2, v2, *, batch, seq, dil):
    length = seq // dil
    tq = min(ATTN_TILE, length)
    view = lambda a: a.reshape(batch, length, a.shape[-1])
    spec = pl.BlockSpec((None, tq, WIDTH), lambda b, r, j: (b, j, r))
    lse_spec = pl.BlockSpec((None, tq, LANES), lambda b, r, j: (b, j, r))
    o, lse = pl.pallas_call(
        functools.partial(_band_attn_kernel, tq=tq),
        grid=(batch, dil, length // tq),
        in_specs=[spec, spec, spec],
        out_specs=[spec, lse_spec],
        out_shape=[jax.ShapeDtypeStruct((batch, length, dil * WIDTH), BF16),
                   jax.ShapeDtypeStruct((batch, length, dil * LANES), F32)],
        scratch_shapes=[pltpu.VMEM((N_WIN + tq, WIDTH), BF16), pltpu.VMEM((N_WIN + tq, 2 * WIDTH), BF16)],
        compiler_params=pltpu.CompilerParams(dimension_semantics=("arbitrary", "arbitrary", "arbitrary"),
                                             vmem_limit_bytes=VMEM_LIMIT),
        name=f"band_attn_d{dil}",
    )(view(q2), view(k2), view(v2))
    return o.reshape(batch * length, dil * WIDTH), lse.reshape(batch * length, dil * LANES)


def _sample_attn_kernel(qt_ref, knt_ref, vnt_ref, kt_ref, vt_ref, ot_ref):
    qt = qt_ref[...]
    vnt = vnt_ref[...]
    s_new = jnp.sum(qt * knt_ref[...], axis=0, keepdims=True)
    wb = kt_ref.shape[1]
    heads = range(N_HEADS)
    head_rows = lambda h: slice(h * HEAD_DIM, (h + 1) * HEAD_DIM)
    s = jnp.concatenate([jnp.sum(kt_ref[head_rows(h), :] * qt[:, h:h + 1], axis=0, keepdims=True) for h in heads],
                        axis=0)
    s0 = jnp.concatenate([s_new[:, h:h + 1] for h in heads], axis=0)
    pos = lax.broadcasted_iota(I32, (N_HEADS, wb), 1)
    es, e0s, dens, lses = [], [], [], []
    for dil in DILATIONS:
        lo = wb - N_WIN * dil
        sc = s[:, lo:]
        valid = (pos[:, lo:] & (dil - 1)) == 0
        m = jnp.maximum(jnp.max(jnp.where(valid, sc, -jnp.inf), axis=-1, keepdims=True), s0)
        e = jnp.where(valid, jnp.exp(sc - m), 0.0)
        e0 = jnp.exp(s0 - m)
        den = jnp.sum(e, axis=-1, keepdims=True) + e0
        es.append(e)
        e0s.append(e0)
        dens.append(den)
        lses.append(m + jnp.log(den))
    mm = jnp.maximum(jnp.maximum(lses[0], lses[1]), lses[2])
    ws = [jnp.exp(l - mm) for l in lses]
    tot = ws[0] + ws[1] + ws[2]
    coef = [w / (tot * den) for w, den in zip(ws, dens)]
    w1, w4, w16 = [e * c for e, c in zip(es, coef)]
    w_new = coef[0] * e0s[0] + coef[1] * e0s[1] + coef[2] * e0s[2]
    n1, n4 = N_WIN * DILATIONS[0], N_WIN * DILATIONS[1]
    w_pos = jnp.concatenate([w16[:, :wb - n4], w16[:, wb - n4:wb - n1] + w4[:, :n4 - n1],
                             w16[:, wb - n1:] + w4[:, n4 - n1:] + w1], axis=1)
    head_lane = lax.broadcasted_iota(I32, (HEAD_DIM, N_HEADS), 1)
    ot = jnp.zeros((HEAD_DIM, N_HEADS), F32)
    for h in heads:
        col = (jnp.sum(vt_ref[head_rows(h), :] * w_pos[h:h + 1, :], axis=-1, keepdims=True) +
               w_new[h:h + 1, :] * vnt[:, h:h + 1])
        ot = jnp.where(head_lane == h, col, ot)
    ot_ref[...] = ot


def _sample_attention(qt, knt, vnt, kt, vt):
    nb, _, wb = kt.shape
    assert wb == MAX_WINDOW, "cache window must cover every dilated key"
    tok = pl.BlockSpec((None, HEAD_DIM, N_HEADS), lambda b: (b, 0, 0))
    cache = pl.BlockSpec((None, WIDTH, wb), lambda b: (b, 0, 0))
    return pl.pallas_call(
        _sample_attn_kernel,
        grid=(nb,),
        in_specs=[tok, tok, tok, cache, cache],
        out_specs=tok,
        out_shape=jax.ShapeDtypeStruct((nb, HEAD_DIM, N_HEADS), F32),
        compiler_params=pltpu.CompilerParams(dimension_semantics=("arbitrary",), vmem_limit_bytes=VMEM_LIMIT),
        name="sample_attn",
    )(qt, knt, vnt, kt, vt)


def _split_hi_lo(w):
    hi = w.astype(BF16)
    return jnp.concatenate([hi, (w - hi.astype(F32)).astype(BF16)], axis=-1)


def _token_order(ref, dil, width, slab, tm):
    if dil == 1:
        return ref[...].astype(F32)
    n_slabs = width // LANES
    for r in range(dil):
        for p in range(n_slabs):
            cols = slice(r * width + p * LANES, r * width + (p + 1) * LANES)
            slab[p, pl.ds(r, tm // dil, stride=dil), :] = ref[:, cols].astype(F32)
    return jnp.concatenate([slab[p] for p in range(n_slabs)], axis=1)


def _mix_kernel(*refs, dils, n_tiles, aliased):
    n_cfg = max(len(dils), 1)
    n_lse = len(dils)
    x_ref, oa_ref = refs[0], refs[1]
    o_refs = refs[2:2 + n_cfg]
    lse_refs = refs[2 + n_cfg:2 + n_cfg + n_lse]
    rest = refs[2 + n_cfg + n_lse:]
    ga_ref, gb_ref, wout_ref, g2_ref, wr_ref, br_ref, expand_ref, tri_ref, upper_ref = rest[:9]
    h_ref, xs_ref, route_ref, nchunk_ref, chunk0_ref, slab = rest[9 + int(aliased):]
    tm = x_ref.shape[0]

    if not dils:
        ob = o_refs[0][...].astype(F32)
    else:
        lses = [_token_order(r, dil, LANES, slab, tm) for r, dil in zip(lse_refs, dils)]
        mm = functools.reduce(jnp.maximum, lses)
        ws = [jnp.exp(l - mm) for l in lses]
        tot = functools.reduce(lambda a, b: a + b, ws)
        ob = jnp.zeros((tm, WIDTH), F32)
        for w, o_ref, dil in zip(ws, o_refs, dils):
            ob = ob + _dot(_split_hi_lo(w / tot), expand_ref[...]) * _token_order(o_ref, dil, WIDTH, slab, tm)

    oa = oa_ref[...].astype(F32)
    ya = oa * lax.rsqrt(jnp.mean(oa * oa, axis=-1, keepdims=True) + EPS) * ga_ref[...]
    yb = ob * lax.rsqrt(jnp.mean(ob * ob, axis=-1, keepdims=True) + EPS) * gb_ref[...]
    cat = jnp.concatenate([ya, yb], axis=-1).astype(BF16)
    h = x_ref[...] + _dot(cat, wout_ref[...])
    h_ref[...] = h
    hn = (h * lax.rsqrt(jnp.mean(h * h, axis=-1, keepdims=True) + EPS) * g2_ref[...]).astype(BF16)

    logits = _dot(hn, wr_ref[...]) + br_ref[...]
    lane = lax.broadcasted_iota(I32, (tm, LANES), 1)
    lane_f = lane.astype(F32)
    neg = -jnp.inf
    big = float(LANES)
    coarse = (lane >= N_EXPERTS) & (lane < N_EXPERTS + N_GROUPS)
    lg = jnp.where(coarse, logits, neg)
    mx = jnp.max(lg, axis=-1, keepdims=True)
    g_lane = jnp.min(jnp.where(lg == mx, lane_f, big), axis=-1, keepdims=True)
    p_star = 1.0 / jnp.sum(jnp.exp(lg - mx), axis=-1, keepdims=True)
    lo = (g_lane - float(N_EXPERTS)) * float(EXPERTS_PER_GROUP)
    lf = jnp.where((lane_f >= lo) & (lane_f < lo + float(EXPERTS_PER_GROUP)), logits, neg)
    v1 = jnp.max(lf, axis=-1, keepdims=True)
    i1 = jnp.min(jnp.where(lf == v1, lane_f, big), axis=-1, keepdims=True)
    lf2 = jnp.where(lane_f == i1, neg, lf)
    v2 = jnp.max(lf2, axis=-1, keepdims=True)
    i2 = jnp.min(jnp.where(lf2 == v2, lane_f, big), axis=-1, keepdims=True)
    e21 = jnp.exp(v2 - v1)
    w1 = p_star / (1.0 + e21)
    w2 = p_star * e21 / (1.0 + e21)

    sel1 = lane_f == i1
    sel2 = lane_f == i2
    onehot = jnp.where(sel1 | sel2, 1.0, 0.0)
    rank = _dot(tri_ref[...], onehot.astype(BF16))
    count = jnp.sum(onehot, axis=0, keepdims=True)
    chunks = jnp.floor((count + float(SLOT_CHUNK - 1)) * (1.0 / SLOT_CHUNK))
    chunk0 = _dot(jnp.broadcast_to(chunks, (SUBLANES, LANES)).astype(BF16), upper_ref[...])[0:1, :]
    slot = rank + chunk0 * float(SLOT_CHUNK)
    s1 = jnp.sum(jnp.where(sel1, slot, 0.0), axis=-1, keepdims=True)
    s2 = jnp.sum(jnp.where(sel2, slot, 0.0), axis=-1, keepdims=True)
    i = pl.program_id(0)
    tile = jnp.minimum(i, n_tiles - 1)
    nchunk_ref[pl.ds(tile, 1), :] = chunks.astype(I32)
    chunk0_ref[pl.ds(tile, 1), :] = chunk0.astype(I32)

    s1_row = jnp.broadcast_to(s1, (tm, LANES)).T[0:1, :]
    s2_row = jnp.broadcast_to(s2, (tm, LANES)).T[0:1, :]
    slot_id = lax.broadcasted_iota(I32, (SLOTS_PER_TILE, tm), 0).astype(F32)
    perm = jnp.where((slot_id == s1_row) | (slot_id == s2_row), 1.0, 0.0).astype(BF16)
    xs_ref[...] = jnp.where(i < n_tiles, _dot(perm, hn), 0.0)

    col = lax.broadcasted_iota(I32, (tm, SUBLANES), 1)
    route = jnp.zeros((tm, SUBLANES), F32)
    for idx, val in enumerate((s1, s2, w1, w2)):
        route = jnp.where(col == idx, val, route)
    route_ref[...] = route


def _mix(x2, oa, obs, lses, dils, ga, gb, wout, g2, wr, br, expand, tri, upper, *, sorted_buf=None,
         first_block=0, spare_blocks=0):
    n = x2.shape[0]
    tm = min(MIX_TILE, n)
    n_tiles = n // tm
    last = n_tiles - 1
    row = lambda w: pl.BlockSpec((tm, w), lambda i: (jnp.minimum(i, last), 0))
    stream = lambda w, dil: pl.BlockSpec((tm // dil, dil * w), lambda i: (jnp.minimum(i, last), 0))
    in_specs = ([row(D_MODEL), row(WIDTH)] + [stream(WIDTH, dil) for dil in (dils or (1,))] +
                [stream(LANES, dil) for dil in dils] +
                [_const_spec((1, WIDTH)), _const_spec((1, WIDTH)), _const_spec((D_MODEL, D_MODEL)),
                 _const_spec((1, D_MODEL)), _const_spec((D_MODEL, LANES)), _const_spec((1, LANES)),
                 _const_spec((2 * LANES, WIDTH)), _const_spec((tm, tm)), _const_spec((LANES, LANES))])
    args = [x2, oa, *obs, *lses, ga, gb, wout, g2, wr, br, expand, tri, upper]
    aliases = {}
    if sorted_buf is None:
        sorted_shape = ((n_tiles + spare_blocks) * SLOTS_PER_TILE, D_MODEL)
    else:
        assert spare_blocks == 0
        sorted_shape = sorted_buf.shape
        aliases = {len(args): 1}
        in_specs.append(pl.BlockSpec(memory_space=pl.ANY))
        args.append(sorted_buf)
    return pl.pallas_call(
        functools.partial(_mix_kernel, dils=tuple(dils), n_tiles=n_tiles, aliased=sorted_buf is not None),
        grid=(n_tiles + spare_blocks,),
        in_specs=in_specs,
        out_specs=[row(D_MODEL), pl.BlockSpec((SLOTS_PER_TILE, D_MODEL), lambda i: (first_block + i, 0)),
                   row(SUBLANES), _const_spec((n_tiles, LANES)), _const_spec((n_tiles, LANES))],
        out_shape=[jax.ShapeDtypeStruct((n, D_MODEL), F32), jax.ShapeDtypeStruct(sorted_shape, F32),
                   jax.ShapeDtypeStruct((n, SUBLANES), F32),
                   jax.ShapeDtypeStruct((n_tiles, LANES), I32), jax.ShapeDtypeStruct((n_tiles, LANES), I32)],
        input_output_aliases=aliases,
        scratch_shapes=[pltpu.VMEM((N_PAIRS, tm, LANES), F32)],
        compiler_params=pltpu.CompilerParams(dimension_semantics=("arbitrary",), vmem_limit_bytes=VMEM_LIMIT),
        name="mix",
    )(*args)


def _chunk(ref, row):
    return ref.at[pl.ds(pl.multiple_of(row, SLOT_CHUNK), SLOT_CHUNK), :]


def _grouped_expert_kernel(te_ref, nch_ref, rows_ref, used_ref, xs_hbm, wg_ref, wu_ref, wd_ref, ys_hbm,
                           xbuf, obuf, zbuf, wgu, wdn, gsem, ssem, zsem):
    g = pl.program_id(0)
    n_steps = pl.num_programs(0)
    cpt = EXPERT_TILE // SLOT_CHUNK
    blocks = used_ref.shape[0]

    def gather(step):
        half = step % 2

        @pl.when(nch_ref[step] > 0)
        def _():
            def one(c, carry):
                pltpu.make_async_copy(_chunk(xs_hbm, rows_ref[step * cpt + c]),
                                      _chunk(xbuf, (half * cpt + c) * SLOT_CHUNK), gsem.at[half]).start()
                return carry

            lax.fori_loop(0, cpt, one, 0, unroll=8)

    def wait_scatter(n):
        def one(_, carry):
            pltpu.make_async_copy(_chunk(obuf, 0), _chunk(ys_hbm, 0), ssem).wait()
            return carry

        lax.fori_loop(0, n, one, 0)

    def zero_fill(wait):
        used = used_ref[jnp.minimum(g, blocks - 1)]
        n_free = SLOTS_PER_TILE // SLOT_CHUNK - used
        bit = ZERO_CHUNKS
        while bit >= 1:
            row = g * SLOTS_PER_TILE + (used + (n_free & ~(2 * bit - 1))) * SLOT_CHUNK
            cp = pltpu.make_async_copy(
                zbuf.at[pl.ds(0, bit * SLOT_CHUNK), :],
                ys_hbm.at[pl.ds(pl.multiple_of(row, SLOT_CHUNK), bit * SLOT_CHUNK), :], zsem)
            pl.when((g < blocks) & ((n_free & bit) != 0))(cp.wait if wait else cp.start)
            bit //= 2

    @pl.when(g == 0)
    def _():
        zbuf[...] = jnp.zeros_like(zbuf)
        gather(0)

    pl.when(g + 1 < n_steps)(functools.partial(gather, g + 1))
    zero_fill(wait=False)

    @pl.when((g == 0) | (te_ref[g] != te_ref[jnp.maximum(g - 1, 0)]))
    def _():
        wgu[:, 0:D_EXPERT] = wg_ref[...].astype(BF16)
        wgu[:, D_EXPERT:2 * D_EXPERT] = wu_ref[...].astype(BF16)
        wdn[...] = wd_ref[...].astype(BF16)

    half = g % 2
    n_chunks = nch_ref[g]
    n_prev = jnp.where(g > 0, nch_ref[jnp.maximum(g - 1, 0)], 0)

    @pl.when(n_chunks > 0)
    def _():
        x_ref = xbuf.at[pl.ds(pl.multiple_of(half * EXPERT_TILE, EXPERT_TILE), EXPERT_TILE), :]
        pltpu.make_async_copy(xs_hbm.at[pl.ds(0, EXPERT_TILE), :], x_ref, gsem.at[half]).wait()
        ab = _dot(x_ref[...].astype(BF16), wgu[...])
        a = ab[:, 0:D_EXPERT]
        hid = (a * jax.nn.sigmoid(a)) * ab[:, D_EXPERT:2 * D_EXPERT]
        out = _dot(hid.astype(BF16), wdn[...])
        wait_scatter(n_prev)
        obuf[...] = out

        def scatter(c, carry):
            pltpu.make_async_copy(_chunk(obuf, c * SLOT_CHUNK), _chunk(ys_hbm, rows_ref[g * cpt + c]), ssem).start()
            return carry

        lax.fori_loop(0, n_chunks, scatter, 0)

    pl.when(n_chunks == 0)(functools.partial(wait_scatter, n_prev))
    pl.when(g == n_steps - 1)(functools.partial(wait_scatter, n_chunks))
    zero_fill(wait=True)


def _grouped_experts(tile_expert, tile_chunks, rows, used, xs, wg, wu, wd):
    n_steps = tile_expert.shape[0]
    weights = lambda shape: pl.BlockSpec((None,) + shape, lambda g, te, *_: (te[g], 0, 0))
    any_spec = pl.BlockSpec(memory_space=pl.ANY)
    return pl.pallas_call(
        _grouped_expert_kernel,
        grid_spec=pltpu.PrefetchScalarGridSpec(
            num_scalar_prefetch=4, grid=(n_steps,),
            in_specs=[any_spec, weights((D_MODEL, D_EXPERT)), weights((D_MODEL, D_EXPERT)),
                      weights((D_EXPERT, D_MODEL))],
            out_specs=any_spec,
            scratch_shapes=[pltpu.VMEM((2 * EXPERT_TILE, D_MODEL), F32), pltpu.VMEM((EXPERT_TILE, D_MODEL), F32),
                            pltpu.VMEM((ZERO_CHUNKS * SLOT_CHUNK, D_MODEL), F32),
                            pltpu.VMEM((D_MODEL, 2 * D_EXPERT), BF16), pltpu.VMEM((D_EXPERT, D_MODEL), BF16),
                            pltpu.SemaphoreType.DMA((2,)), pltpu.SemaphoreType.DMA, pltpu.SemaphoreType.DMA]),
        out_shape=jax.ShapeDtypeStruct(xs.shape, F32),
        compiler_params=pltpu.CompilerParams(dimension_semantics=("arbitrary",), vmem_limit_bytes=VMEM_LIMIT),
        name="experts",
    )(tile_expert, tile_chunks, rows, used, xs, wg, wu, wd)


def _unsort_kernel(h_ref, route_ref, ys_ref, y_ref):
    tm = h_ref.shape[0]
    route = route_ref[...]
    slot_id = lax.broadcasted_iota(I32, (tm, SLOTS_PER_TILE), 1).astype(F32)
    sel = (jnp.where(slot_id == route[:, 0:1], route[:, 2:3], 0.0) +
           jnp.where(slot_id == route[:, 1:2], route[:, 3:4], 0.0))
    y_ref[...] = h_ref[...] + _dot(sel.astype(BF16), ys_ref[...].astype(BF16))


def _unsort(h, route, ys, first_block):
    n = h.shape[0]
    tm = min(MIX_TILE, n)
    return pl.pallas_call(
        _unsort_kernel,
        grid=(n // tm,),
        in_specs=[pl.BlockSpec((tm, D_MODEL), lambda i: (i, 0)), pl.BlockSpec((tm, SUBLANES), lambda i: (i, 0)),
                  pl.BlockSpec((SLOTS_PER_TILE, D_MODEL), lambda i: (first_block + i, 0))],
        out_specs=pl.BlockSpec((tm, D_MODEL), lambda i: (i, 0)),
        out_shape=jax.ShapeDtypeStruct((n, D_MODEL), F32),
        compiler_params=pltpu.CompilerParams(dimension_semantics=("arbitrary",), vmem_limit_bytes=VMEM_LIMIT),
        name="combine",
    )(h, route, ys)


def _grouped_moe(parts, xs, wg, wu, wd):
    cpt = EXPERT_TILE // SLOT_CHUNK
    nck = jnp.concatenate([p[2] for p in parts])[:, :N_EXPERTS]
    ck0 = jnp.concatenate([p[3] for p in parts])[:, :N_EXPERTS]
    blocks = nck.shape[0]
    cum_incl = jnp.cumsum(nck, axis=0)
    cum_excl = cum_incl - nck
    per_expert = cum_incl[blocks - 1]
    tiles_e = (per_expert + cpt - 1) // cpt
    ends = jnp.cumsum(tiles_e)
    n_steps = blocks * (SLOTS_PER_TILE // SLOT_CHUNK) // cpt + N_EXPERTS
    step = jnp.arange(n_steps, dtype=I32)
    tile_expert = jnp.minimum(jnp.sum((step[:, None] >= ends[None, :]).astype(I32), axis=1), N_EXPERTS - 1)
    first_step = (ends - tiles_e)[tile_expert]
    mine = per_expert[tile_expert]
    tile_chunks = jnp.clip(mine - (step - first_step) * cpt, 0, cpt)
    tile_chunks = jnp.where(step < ends[N_EXPERTS - 1], tile_chunks, 0).astype(I32)
    p = (step - first_step)[:, None] * cpt + jnp.arange(cpt, dtype=I32)[None, :]
    p = jnp.clip(p, 0, jnp.maximum(mine - 1, 0)[:, None])[:, :, None]
    lo = cum_excl.T[tile_expert][:, None, :]
    hi = cum_incl.T[tile_expert][:, None, :]
    local = ck0.T[tile_expert][:, None, :] + p - lo
    block_row = (jnp.arange(blocks, dtype=I32) * SLOTS_PER_TILE)[None, None, :]
    rows = jnp.sum(jnp.where((lo <= p) & (p < hi), block_row + local * SLOT_CHUNK, 0), axis=-1).astype(I32)
    used = (ck0[:, N_EXPERTS - 1] + nck[:, N_EXPERTS - 1]).astype(I32)
    ys = _grouped_experts(tile_expert, tile_chunks, rows.reshape(-1), used, xs, wg, wu, wd)
    outs, first = [], 0
    for h, route, nck_p, _ in parts:
        outs.append(_unsort(h, route, ys, first))
        first += nck_p.shape[0]
    return outs


def kernel(x_prompt, x_sample, cache_k, cache_v, norm1_g, w_in, q_gain, k_gain, v_gain, w_spatial, b_spatial,
           out_gain_a, out_gain_b, w_out, norm2_g, w_router1, b_router1, w_router2, b_router2, w_up, w_gate,
           w_down):
    depth = norm1_g.shape[0]
    assert depth == 1
    l = 0
    batch, seq, _ = x_prompt.shape
    nb, dec_seq, _ = x_sample.shape
    assert dec_seq == 1

    g1 = norm1_g[l].reshape(1, D_MODEL)
    win = w_in[l].astype(BF16)
    gains = jnp.stack([q_gain[l].reshape(WIDTH), k_gain[l].reshape(WIDTH), v_gain[l].reshape(WIDTH)])
    head_of_lane = jnp.arange(WIDTH) // HEAD_DIM
    bd = (head_of_lane[:, None] == head_of_lane[None, :]).astype(BF16)
    ws_tril = jnp.tril(w_spatial[l])
    wcat = jnp.concatenate([ws_tril[0::2], ws_tril[1::2]], axis=-1).astype(BF16)
    bs = b_spatial[l]
    bias = jnp.where(jnp.arange(LANES)[None, None, :] < HEAD_DIM, bs[0::2][:, :, None], bs[1::2][:, :, None])
    avec = jnp.repeat(w_spatial[l][:, 0, 0], HEAD_DIM).reshape(1, WIDTH)
    bvec = jnp.repeat(bs[:, 0], HEAD_DIM).reshape(1, WIDTH)
    ga = out_gain_a[l].reshape(1, WIDTH)
    gb = out_gain_b[l].reshape(1, WIDTH)
    wout = w_out[l].astype(BF16)
    g2 = norm2_g[l].reshape(1, D_MODEL)
    wr = jnp.zeros((D_MODEL, LANES), F32)
    wr = wr.at[:, :N_EXPERTS].set(jnp.transpose(w_router2[l], (1, 0, 2)).reshape(D_MODEL, N_EXPERTS))
    wr = wr.at[:, N_EXPERTS:N_EXPERTS + N_GROUPS].set(w_router1[l]).astype(BF16)
    br = jnp.zeros((1, LANES), F32)
    br = br.at[0, :N_EXPERTS].set(b_router2[l].reshape(N_EXPERTS))
    br = br.at[0, N_EXPERTS:N_EXPERTS + N_GROUPS].set(b_router1[l])
    lane_head = (jnp.arange(LANES)[:, None] == head_of_lane[None, :]).astype(BF16)
    expand = jnp.concatenate([lane_head, lane_head], axis=0)
    wg = w_gate[l].reshape(N_EXPERTS, D_MODEL, D_EXPERT)
    wu = w_up[l].reshape(N_EXPERTS, D_MODEL, D_EXPERT)
    wd = w_down[l].reshape(N_EXPERTS, D_EXPERT, D_MODEL)

    def tri(t):
        return (jnp.arange(t)[:, None] > jnp.arange(t)[None, :]).astype(BF16)

    xp = x_prompt.reshape(batch * seq, D_MODEL)
    qkv, oa, kt_last, vt_last = _proj_prompt(xp, g1, win, gains, bd, wcat, bias, seq=seq)
    obs, lses = [], []
    for (q, k, v), dil in zip(qkv, DILATIONS):
        o, lse = _band_attention(q, k, v, batch=batch, seq=seq, dil=dil)
        obs.append(o)
        lses.append(lse)
    tmix = min(MIX_TILE, batch * seq)
    upper = (jnp.arange(LANES)[:, None] < jnp.arange(LANES)[None, :]).astype(BF16)
    tmix_s = min(MIX_TILE, nb)
    h, sorted_rows, route, nck, ck0 = _mix(xp, oa, obs, lses, DILATIONS, ga, gb, wout, g2, wr, br, expand,
                                            tri(tmix), upper, spare_blocks=nb // tmix_s)
    kept = min(MAX_WINDOW, seq)
    to_cache = lambda t: jnp.transpose(t.reshape(1, batch, N_HEADS, HEAD_DIM, kept), (0, 1, 4, 2, 3))

    xs = x_sample.reshape(nb, D_MODEL)
    qs, ks, vs, vgs, oas = _proj_sample(xs, g1, win, gains, bd, avec, bvec)
    cols = lambda t: jnp.transpose(t.reshape(nb, N_HEADS, HEAD_DIM), (0, 2, 1))
    feature_major = lambda c: jnp.transpose(c, (0, 2, 3, 1)).reshape(nb, WIDTH, c.shape[1])
    ot = _sample_attention(cols(qs), cols(ks), cols(vs), feature_major(cache_k[l]), feature_major(cache_v[l]))
    obs_s = jnp.transpose(ot, (0, 2, 1)).reshape(nb, WIDTH)
    hs, sorted_rows, route_s, nck_s, ck0_s = _mix(xs, oas, [obs_s], [], (), ga, gb, wout, g2, wr, br, expand,
                                                  tri(tmix_s), upper, sorted_buf=sorted_rows,
                                                  first_block=batch * seq // tmix)

    y_prompt, y_sample = _grouped_moe([(h, route, nck, ck0), (hs, route_s, nck_s, ck0_s)], sorted_rows,
                                      wg, wu, wd)
    to5 = lambda t: t.reshape(1, nb, 1, N_HEADS, HEAD_DIM)
    return (y_prompt.reshape(batch, seq, D_MODEL), y_sample.reshape(nb, 1, D_MODEL),
            to_cache(kt_last), to_cache(vt_last), to5(ks), to5(vs), to5(vgs))
```

```python
import functools

import jax
import jax.numpy as jnp
from jax import lax
from jax.experimental import pallas as pl
from jax.experimental.pallas import tpu as pltpu

F32 = jnp.float32
BF16 = jnp.bfloat16
I32 = jnp.int32

D_MODEL = 1024
HEAD_DIM = 64
N_HEADS = 8
WIDTH = N_HEADS * HEAD_DIM
N_PAIRS = N_HEADS // 2
CHUNK = 128
N_WIN = 128
DILATIONS = (1, 4, 16)
MAX_WINDOW = 2048
N_GROUPS = 4
EXPERTS_PER_GROUP = 8
N_EXPERTS = N_GROUPS * EXPERTS_PER_GROUP
D_EXPERT = D_MODEL // 4
EPS = 1e-6

LANES = 128
SUBLANES = 8
VMEM_LIMIT = 48 * 1024 * 1024

PROJ_TILE = 512
ATTN_TILE = 512
MIX_TILE = 256
EXPERT_TILE = 512
SLOT_CHUNK = SUBLANES
SLOTS_PER_TILE = 768
ZERO_CHUNKS = 64


def _dot(a, b):
    return jnp.dot(a, b, preferred_element_type=F32)


def _gelu(x):
    return 0.5 * x * (1.0 + jnp.tanh(0.7978845608028654 * (x + 0.044715 * (x * x * x))))


def _head_rms(t, gain, bd):
    ss = _dot((t * t).astype(BF16), bd)
    return t * lax.rsqrt(ss * (1.0 / HEAD_DIM) + EPS) * gain


def _proj_common(x_ref, g1_ref, win_ref, gains_ref, bd_ref):
    x = x_ref[...]
    r = lax.rsqrt(jnp.mean(x * x, axis=-1, keepdims=True) + EPS)
    xn = (x * r * g1_ref[...]).astype(BF16)
    bd = bd_ref[...]
    q = _head_rms(_dot(xn, win_ref[:, 0:WIDTH]), gains_ref[0:1, :], bd) * (HEAD_DIM ** -0.5)
    k = _head_rms(_dot(xn, win_ref[:, WIDTH:2 * WIDTH]), gains_ref[1:2, :], bd)
    v = _dot(xn, win_ref[:, 2 * WIDTH:3 * WIDTH])
    u = _gelu(_dot(xn, win_ref[:, 3 * WIDTH:4 * WIDTH]))
    vg = _head_rms(_gelu(_dot(xn, win_ref[:, 4 * WIDTH:5 * WIDTH])), gains_ref[2:3, :], bd)
    return q, k, v, u, vg


def _proj_prompt_kernel(x_ref, g1_ref, win_ref, gains_ref, bd_ref, wcat_ref, bias_ref, *rest,
                        tiles_per_seq, first_kept_tile):
    stream_refs = rest[:3 * len(DILATIONS)]
    oa_ref, kl_ref, vl_ref, slab_a, slab_b = rest[3 * len(DILATIONS):]
    q, k, v, u, vg = _proj_common(x_ref, g1_ref, win_ref, gains_ref, bd_ref)
    tm = x_ref.shape[0]
    for a, z in enumerate((q, k, v)):
        cur, nxt = slab_a, slab_b
        for p in range(N_PAIRS):
            cur[p] = z[:, p * LANES:(p + 1) * LANES]
        for c, dil in enumerate(DILATIONS):
            out_ref = stream_refs[3 * c + a]
            if dil == 1:
                out_ref[...] = z.astype(BF16)
                continue
            prev = DILATIONS[c - 1]
            rows, rows_prev = tm // dil, tm // prev
            for r in range(dil):
                for p in range(N_PAIRS):
                    part = cur[p, pl.ds((r % prev) * rows_prev + r // prev, rows, stride=dil // prev), :]
                    out_ref[:, r * WIDTH + p * LANES:r * WIDTH + (p + 1) * LANES] = part.astype(BF16)
                    if c + 1 < len(DILATIONS):
                        nxt[p, r * rows:(r + 1) * rows, :] = part
            cur, nxt = nxt, cur

    @pl.when(pl.program_id(0) % tiles_per_seq >= first_kept_tile)
    def _():
        kl_ref[...] = k.T
        vl_ref[...] = v.T

    vgb = vg.astype(BF16)
    lane = lax.broadcasted_iota(I32, (CHUNK, LANES), 1)
    tm = x_ref.shape[0]
    for c in range(tm // CHUNK):
        rows = slice(c * CHUNK, (c + 1) * CHUNK)
        for p in range(N_PAIRS):
            cols = slice(p * LANES, (p + 1) * LANES)
            vp = vgb[rows, cols]
            zero = jnp.zeros_like(vp)
            rhs = jnp.concatenate([jnp.where(lane < HEAD_DIM, vp, zero),
                                   jnp.where(lane >= HEAD_DIM, vp, zero)], axis=0)
            mixed = _dot(wcat_ref[p], rhs) + bias_ref[p]
            oa_ref[rows, cols] = (u[rows, cols] * mixed).astype(BF16)


def _proj_sample_kernel(x_ref, g1_ref, win_ref, gains_ref, bd_ref, avec_ref, bvec_ref,
                        q_ref, k_ref, v_ref, vg_ref, oa_ref):
    q, k, v, u, vg = _proj_common(x_ref, g1_ref, win_ref, gains_ref, bd_ref)
    q_ref[...] = q
    k_ref[...] = k
    v_ref[...] = v
    vg_ref[...] = vg
    oa_ref[...] = (u * (avec_ref[...] * vg + bvec_ref[...])).astype(BF16)


def _const_spec(shape):
    return pl.BlockSpec(shape, lambda *_: (0,) * len(shape))


def _proj_prompt(x2, g1, win, gains, bd, wcat, bias, *, seq):
    n = x2.shape[0]
    tm = PROJ_TILE
    tiles_per_seq = seq // tm
    kept = min(MAX_WINDOW, seq)
    first_kept_tile = tiles_per_seq - kept // tm

    def kept_map(i):
        return (i // tiles_per_seq, 0, jnp.maximum(i % tiles_per_seq - first_kept_tile, 0))

    row_spec = pl.BlockSpec((tm, WIDTH), lambda i: (i, 0))
    kept_spec = pl.BlockSpec((None, WIDTH, tm), kept_map)
    stream_specs = [pl.BlockSpec((tm // dil, dil * WIDTH), lambda i: (i, 0)) for dil in DILATIONS for _ in range(3)]
    stream_shapes = [jax.ShapeDtypeStruct((n // dil, dil * WIDTH), BF16) for dil in DILATIONS for _ in range(3)]
    outs = pl.pallas_call(
        functools.partial(_proj_prompt_kernel, tiles_per_seq=tiles_per_seq, first_kept_tile=first_kept_tile),
        grid=(n // tm,),
        in_specs=[pl.BlockSpec((tm, D_MODEL), lambda i: (i, 0)),
                  _const_spec((1, D_MODEL)), _const_spec((D_MODEL, 5 * WIDTH)), _const_spec((3, WIDTH)),
                  _const_spec((WIDTH, WIDTH)), _const_spec((N_PAIRS, CHUNK, 2 * CHUNK)),
                  _const_spec((N_PAIRS, CHUNK, LANES))],
        out_specs=stream_specs + [row_spec, kept_spec, kept_spec],
        out_shape=stream_shapes + [jax.ShapeDtypeStruct((n, WIDTH), BF16)] +
                  [jax.ShapeDtypeStruct((n // seq, WIDTH, kept), F32)] * 2,
        scratch_shapes=[pltpu.VMEM((N_PAIRS, tm, LANES), F32)] * 2,
        compiler_params=pltpu.CompilerParams(dimension_semantics=("arbitrary",), vmem_limit_bytes=VMEM_LIMIT),
        name="proj_prompt",
    )(x2, g1, win, gains, bd, wcat, bias)
    n_streams = 3 * len(DILATIONS)
    qkv = [outs[3 * c:3 * c + 3] for c in range(len(DILATIONS))]
    return (qkv, *outs[n_streams:])


def _proj_sample(x2, g1, win, gains, bd, avec, bvec):
    n = x2.shape[0]
    full = _const_spec((n, WIDTH))
    return pl.pallas_call(
        _proj_sample_kernel,
        grid=(1,),
        in_specs=[_const_spec((n, D_MODEL)), _const_spec((1, D_MODEL)), _const_spec((D_MODEL, 5 * WIDTH)),
                  _const_spec((3, WIDTH)), _const_spec((WIDTH, WIDTH)), _const_spec((1, WIDTH)),
                  _const_spec((1, WIDTH))],
        out_specs=[full] * 5,
        out_shape=[jax.ShapeDtypeStruct((n, WIDTH), F32)] * 4 + [jax.ShapeDtypeStruct((n, WIDTH), BF16)],
        compiler_params=pltpu.CompilerParams(dimension_semantics=("arbitrary",), vmem_limit_bytes=VMEM_LIMIT),
        name="proj_sample",
    )(x2, g1, win, gains, bd, avec, bvec)


def _band_attn_kernel(q_ref, k_ref, v_ref, o_ref, lse_ref, kbuf, vbuf, *, tq):
    j = pl.program_id(2)
    nblk = tq // N_WIN

    @pl.when(j == 0)
    def _():
        kbuf[0:N_WIN, :] = jnp.zeros((N_WIN, WIDTH), BF16)
        vbuf[...] = jnp.ones(vbuf.shape, BF16)

    @pl.when(j > 0)
    def _():
        kbuf[0:N_WIN, :] = kbuf[tq:tq + N_WIN, :]
        vbuf[0:N_WIN, :] = vbuf[tq:tq + N_WIN, :]

    kbuf[N_WIN:N_WIN + tq, :] = k_ref[...]
    for p in range(N_PAIRS):
        vbuf[N_WIN:N_WIN + tq, 2 * p * LANES:(2 * p + 1) * LANES] = v_ref[:, p * LANES:(p + 1) * LANES]

    qi = lax.broadcasted_iota(I32, (N_WIN, 2 * N_WIN), 0)
    ki = lax.broadcasted_iota(I32, (N_WIN, 2 * N_WIN), 1)
    band = (ki >= qi) & (ki <= qi + N_WIN)
    lane = lax.broadcasted_iota(I32, (N_WIN, LANES), 1)
    low_half = lane < HEAD_DIM

    def block(jb, carry):
        q0 = pl.multiple_of(jb * N_WIN, N_WIN)
        valid = band & ((ki >= N_WIN) | (j * nblk + jb > 0))
        valid2 = jnp.concatenate([valid, valid], axis=0)
        lse_tile = jnp.zeros((N_WIN, LANES), F32)
        for p in range(N_PAIRS):
            cols = slice(p * LANES, (p + 1) * LANES)
            qp = q_ref[pl.ds(q0, N_WIN), cols]
            kp = kbuf[pl.ds(q0, 2 * N_WIN), cols]
            vp = vbuf[pl.ds(q0, 2 * N_WIN), 2 * p * LANES:(2 * p + 2) * LANES]
            zero = jnp.zeros_like(qp)
            q2 = jnp.concatenate([jnp.where(low_half, qp, zero), jnp.where(low_half, zero, qp)], axis=0)
            s = lax.dot_general(q2, kp, (((1,), (1,)), ((), ())), preferred_element_type=F32)
            s = jnp.where(valid2, s, -jnp.inf)
            m = jnp.max(s, axis=-1, keepdims=True)
            e = jnp.exp(s - m)
            ov = _dot(e.astype(BF16), vp)
            den = ov[:, LANES:2 * LANES]
            o = ov[:, 0:LANES] / den
            lse = m + jnp.log(den)
            lse_tile = jnp.where(lane == 2 * p, lse[0:N_WIN],
                                 jnp.where(lane == 2 * p + 1, lse[N_WIN:2 * N_WIN], lse_tile))
            o_ref[pl.ds(q0, N_WIN), cols] = jnp.where(low_half, o[0:N_WIN], o[N_WIN:2 * N_WIN]).astype(BF16)
        lse_ref[pl.ds(q0, N_WIN), :] = lse_tile
        return carry

    lax.fori_loop(0, nblk, block, 0)


def _band_attention(q2, k2, v2, *, batch, seq, dil):
    length = seq // dil
    tq = min(ATTN_TILE, length)
    view = lambda a: a.reshape(batch, length, a.shape[-1])
    spec = pl.BlockSpec((None, tq, WIDTH), lambda b, r, j: (b, j, r))
    lse_spec = pl.BlockSpec((None, tq, LANES), lambda b, r, j: (b, j, r))
    o, lse = pl.pallas_call(
        functools.partial(_band_attn_kernel, tq=tq),
        grid=(batch, dil, length // tq),
        in_specs=[spec, spec, spec],
        out_specs=[spec, lse_spec],
        out_shape=[jax.ShapeDtypeStruct((batch, length, dil * WIDTH), BF16),
                   jax.ShapeDtypeStruct((batch, length, dil * LANES), F32)],
        scratch_shapes=[pltpu.VMEM((N_WIN + tq, WIDTH), BF16), pltpu.VMEM((N_WIN + tq, 2 * WIDTH), BF16)],
        compiler_params=pltpu.CompilerParams(dimension_semantics=("arbitrary", "arbitrary", "arbitrary"),
                                             vmem_limit_bytes=VMEM_LIMIT),
        name=f"band_attn_d{dil}",
    )(view(q2), view(k2), view(v2))
    return o.reshape(batch * length, dil * WIDTH), lse.reshape(batch * length, dil * LANES)


def _sample_attn_kernel(qt_ref, knt_ref, vnt_ref, kt_ref, vt_ref, ot_ref):
    qt = qt_ref[...]
    vnt = vnt_ref[...]
    s_new = jnp.sum(qt * knt_ref[...], axis=0, keepdims=True)
    wb = kt_ref.shape[1]
    heads = range(N_HEADS)
    head_rows = lambda h: slice(h * HEAD_DIM, (h + 1) * HEAD_DIM)
    s = jnp.concatenate([jnp.sum(kt_ref[head_rows(h), :] * qt[:, h:h + 1], axis=0, keepdims=True) for h in heads],
                        axis=0)
    s0 = jnp.concatenate([s_new[:, h:h + 1] for h in heads], axis=0)
    pos = lax.broadcasted_iota(I32, (N_HEADS, wb), 1)
    es, e0s, dens, lses = [], [], [], []
    for dil in DILATIONS:
        lo = wb - N_WIN * dil
        sc = s[:, lo:]
        valid = (pos[:, lo:] & (dil - 1)) == 0
        m = jnp.maximum(jnp.max(jnp.where(valid, sc, -jnp.inf), axis=-1, keepdims=True), s0)
        e = jnp.where(valid, jnp.exp(sc - m), 0.0)
        e0 = jnp.exp(s0 - m)
        den = jnp.sum(e, axis=-1, keepdims=True) + e0
        es.append(e)
        e0s.append(e0)
        dens.append(den)
        lses.append(m + jnp.log(den))
    mm = jnp.maximum(jnp.maximum(lses[0], lses[1]), lses[2])
    ws = [jnp.exp(l - mm) for l in lses]
    tot = ws[0] + ws[1] + ws[2]
    coef = [w / (tot * den) for w, den in zip(ws, dens)]
    w1, w4, w16 = [e * c for e, c in zip(es, coef)]
    w_new = coef[0] * e0s[0] + coef[1] * e0s[1] + coef[2] * e0s[2]
    n1, n4 = N_WIN * DILATIONS[0], N_WIN * DILATIONS[1]
    w_pos = jnp.concatenate([w16[:, :wb - n4], w16[:, wb - n4:wb - n1] + w4[:, :n4 - n1],
                             w16[:, wb - n1:] + w4[:, n4 - n1:] + w1], axis=1)
    head_lane = lax.broadcasted_iota(I32, (HEAD_DIM, N_HEADS), 1)
    ot = jnp.zeros((HEAD_DIM, N_HEADS), F32)
    for h in heads:
        col = (jnp.sum(vt_ref[head_rows(h), :] * w_pos[h:h + 1, :], axis=-1, keepdims=True) +
               w_new[h:h + 1, :] * vnt[:, h:h + 1])
        ot = jnp.where(head_lane == h, col, ot)
    ot_ref[...] = ot


def _sample_attention(qt, knt, vnt, kt, vt):
    nb, _, wb = kt.shape
    assert wb == MAX_WINDOW, "cache window must cover every dilated key"
    tok = pl.BlockSpec((None, HEAD_DIM, N_HEADS), lambda b: (b, 0, 0))
    cache = pl.BlockSpec((None, WIDTH, wb), lambda b: (b, 0, 0))
    return pl.pallas_call(
        _sample_attn_kernel,
        grid=(nb,),
        in_specs=[tok, tok, tok, cache, cache],
        out_specs=tok,
        out_shape=jax.ShapeDtypeStruct((nb, HEAD_DIM, N_HEADS), F32),
        compiler_params=pltpu.CompilerParams(dimension_semantics=("arbitrary",), vmem_limit_bytes=VMEM_LIMIT),
        name="sample_attn",
    )(qt, knt, vnt, kt, vt)


def _split_hi_lo(w):
    hi = w.astype(BF16)
    return jnp.concatenate([hi, (w - hi.astype(F32)).astype(BF16)], axis=-1)


def _token_order(ref, dil, width, slab, tm):
    if dil == 1:
        return ref[...].astype(F32)
    n_slabs = width // LANES
    for r in range(dil):
        for p in range(n_slabs):
            cols = slice(r * width + p * LANES, r * width + (p + 1) * LANES)
            slab[p, pl.ds(r, tm // dil, stride=dil), :] = ref[:, cols].astype(F32)
    return jnp.concatenate([slab[p] for p in range(n_slabs)], axis=1)


def _mix_kernel(*refs, dils, n_tiles, aliased):
    n_cfg = max(len(dils), 1)
    n_lse = len(dils)
    x_ref, oa_ref = refs[0], refs[1]
    o_refs = refs[2:2 + n_cfg]
    lse_refs = refs[2 + n_cfg:2 + n_cfg + n_lse]
    rest = refs[2 + n_cfg + n_lse:]
    ga_ref, gb_ref, wout_ref, g2_ref, wr_ref, br_ref, expand_ref, tri_ref, upper_ref = rest[:9]
    h_ref, xs_ref, route_ref, nchunk_ref, chunk0_ref, slab = rest[9 + int(aliased):]
    tm = x_ref.shape[0]

    if not dils:
        ob = o_refs[0][...].astype(F32)
    else:
        lses = [_token_order(r, dil, LANES, slab, tm) for r, dil in zip(lse_refs, dils)]
        mm = functools.reduce(jnp.maximum, lses)
        ws = [jnp.exp(l - mm) for l in lses]
        tot = functools.reduce(lambda a, b: a + b, ws)
        ob = jnp.zeros((tm, WIDTH), F32)
        for w, o_ref, dil in zip(ws, o_refs, dils):
            ob = ob + _dot(_split_hi_lo(w / tot), expand_ref[...]) * _token_order(o_ref, dil, WIDTH, slab, tm)

    oa = oa_ref[...].astype(F32)
    ya = oa * lax.rsqrt(jnp.mean(oa * oa, axis=-1, keepdims=True) + EPS) * ga_ref[...]
    yb = ob * lax.rsqrt(jnp.mean(ob * ob, axis=-1, keepdims=True) + EPS) * gb_ref[...]
    cat = jnp.concatenate([ya, yb], axis=-1).astype(BF16)
    h = x_ref[...] + _dot(cat, wout_ref[...])
    h_ref[...] = h
    hn = (h * lax.rsqrt(jnp.mean(h * h, axis=-1, keepdims=True) + EPS) * g2_ref[...]).astype(BF16)

    logits = _dot(hn, wr_ref[...]) + br_ref[...]
    lane = lax.broadcasted_iota(I32, (tm, LANES), 1)
    lane_f = lane.astype(F32)
    neg = -jnp.inf
    big = float(LANES)
    coarse = (lane >= N_EXPERTS) & (lane < N_EXPERTS + N_GROUPS)
    lg = jnp.where(coarse, logits, neg)
    mx = jnp.max(lg, axis=-1, keepdims=True)
    g_lane = jnp.min(jnp.where(lg == mx, lane_f, big), axis=-1, keepdims=True)
    p_star = 1.0 / jnp.sum(jnp.exp(lg - mx), axis=-1, keepdims=True)
    lo = (g_lane - float(N_EXPERTS)) * float(EXPERTS_PER_GROUP)
    lf = jnp.where((lane_f >= lo) & (lane_f < lo + float(EXPERTS_PER_GROUP)), logits, neg)
    v1 = jnp.max(lf, axis=-1, keepdims=True)
    i1 = jnp.min(jnp.where(lf == v1, lane_f, big), axis=-1, keepdims=True)
    lf2 = jnp.where(lane_f == i1, neg, lf)
    v2 = jnp.max(lf2, axis=-1, keepdims=True)
    i2 = jnp.min(jnp.where(lf2 == v2, lane_f, big), axis=-1, keepdims=True)
    e21 = jnp.exp(v2 - v1)
    w1 = p_star / (1.0 + e21)
    w2 = p_star * e21 / (1.0 + e21)

    sel1 = lane_f == i1
    sel2 = lane_f == i2
    onehot = jnp.where(sel1 | sel2, 1.0, 0.0)
    rank = _dot(tri_ref[...], onehot.astype(BF16))
    count = jnp.sum(onehot, axis=0, keepdims=True)
    chunks = jnp.floor((count + float(SLOT_CHUNK - 1)) * (1.0 / SLOT_CHUNK))
    chunk0 = _dot(jnp.broadcast_to(chunks, (SUBLANES, LANES)).astype(BF16), upper_ref[...])[0:1, :]
    slot = rank + chunk0 * float(SLOT_CHUNK)
    s1 = jnp.sum(jnp.where(sel1, slot, 0.0), axis=-1, keepdims=True)
    s2 = jnp.sum(jnp.where(sel2, slot, 0.0), axis=-1, keepdims=True)
    i = pl.program_id(0)
    tile = jnp.minimum(i, n_tiles - 1)
    nchunk_ref[pl.ds(tile, 1), :] = chunks.astype(I32)
    chunk0_ref[pl.ds(tile, 1), :] = chunk0.astype(I32)

    s1_row = jnp.broadcast_to(s1, (tm, LANES)).T[0:1, :]
    s2_row = jnp.broadcast_to(s2, (tm, LANES)).T[0:1, :]
    slot_id = lax.broadcasted_iota(I32, (SLOTS_PER_TILE, tm), 0).astype(F32)
    perm = jnp.where((slot_id == s1_row) | (slot_id == s2_row), 1.0, 0.0).astype(BF16)
    xs_ref[...] = jnp.where(i < n_tiles, _dot(perm, hn), 0.0)

    col = lax.broadcasted_iota(I32, (tm, SUBLANES), 1)
    route = jnp.zeros((tm, SUBLANES), F32)
    for idx, val in enumerate((s1, s2, w1, w2)):
        route = jnp.where(col == idx, val, route)
    route_ref[...] = route


def _mix(x2, oa, obs, lses, dils, ga, gb, wout, g2, wr, br, expand, tri, upper, *, sorted_buf=None,
         first_block=0, spare_blocks=0):
    n = x2.shape[0]
    tm = min(MIX_TILE, n)
    n_tiles = n // tm
    last = n_tiles - 1
    row = lambda w: pl.BlockSpec((tm, w), lambda i: (jnp.minimum(i, last), 0))
    stream = lambda w, dil: pl.BlockSpec((tm // dil, dil * w), lambda i: (jnp.minimum(i, last), 0))
    in_specs = ([row(D_MODEL), row(WIDTH)] + [stream(WIDTH, dil) for dil in (dils or (1,))] +
                [stream(LANES, dil) for dil in dils] +
                [_const_spec((1, WIDTH)), _const_spec((1, WIDTH)), _const_spec((D_MODEL, D_MODEL)),
                 _const_spec((1, D_MODEL)), _const_spec((D_MODEL, LANES)), _const_spec((1, LANES)),
                 _const_spec((2 * LANES, WIDTH)), _const_spec((tm, tm)), _const_spec((LANES, LANES))])
    args = [x2, oa, *obs, *lses, ga, gb, wout, g2, wr, br, expand, tri, upper]
    aliases = {}
    if sorted_buf is None:
        sorted_shape = ((n_tiles + spare_blocks) * SLOTS_PER_TILE, D_MODEL)
    else:
        assert spare_blocks == 0
        sorted_shape = sorted_buf.shape
        aliases = {len(args): 1}
        in_specs.append(pl.BlockSpec(memory_space=pl.ANY))
        args.append(sorted_buf)
    return pl.pallas_call(
        functools.partial(_mix_kernel, dils=tuple(dils), n_tiles=n_tiles, aliased=sorted_buf is not None),
        grid=(n_tiles + spare_blocks,),
        in_specs=in_specs,
        out_specs=[row(D_MODEL), pl.BlockSpec((SLOTS_PER_TILE, D_MODEL), lambda i: (first_block + i, 0)),
                   row(SUBLANES), _const_spec((n_tiles, LANES)), _const_spec((n_tiles, LANES))],
        out_shape=[jax.ShapeDtypeStruct((n, D_MODEL), F32), jax.ShapeDtypeStruct(sorted_shape, F32),
                   jax.ShapeDtypeStruct((n, SUBLANES), F32),
                   jax.ShapeDtypeStruct((n_tiles, LANES), I32), jax.ShapeDtypeStruct((n_tiles, LANES), I32)],
        input_output_aliases=aliases,
        scratch_shapes=[pltpu.VMEM((N_PAIRS, tm, LANES), F32)],
        compiler_params=pltpu.CompilerParams(dimension_semantics=("arbitrary",), vmem_limit_bytes=VMEM_LIMIT),
        name="mix",
    )(*args)


def _chunk(ref, row):
    return ref.at[pl.ds(pl.multiple_of(row, SLOT_CHUNK), SLOT_CHUNK), :]


def _grouped_expert_kernel(te_ref, nch_ref, rows_ref, used_ref, xs_hbm, wg_ref, wu_ref, wd_ref, ys_hbm,
                           xbuf, obuf, zbuf, wgu, wdn, gsem, ssem, zsem):
    g = pl.program_id(0)
    n_steps = pl.num_programs(0)
    cpt = EXPERT_TILE // SLOT_CHUNK
    blocks = used_ref.shape[0]

    def gather(step):
        half = step % 2

        @pl.when(nch_ref[step] > 0)
        def _():
            def one(c, carry):
                pltpu.make_async_copy(_chunk(xs_hbm, rows_ref[step * cpt + c]),
                                      _chunk(xbuf, (half * cpt + c) * SLOT_CHUNK), gsem.at[half]).start()
                return carry

            lax.fori_loop(0, cpt, one, 0, unroll=8)

    def wait_scatter(n):
        def one(_, carry):
            pltpu.make_async_copy(_chunk(obuf, 0), _chunk(ys_hbm, 0), ssem).wait()
            return carry

        lax.fori_loop(0, n, one, 0)

    def zero_fill(wait):
        used = used_ref[jnp.minimum(g, blocks - 1)]
        n_free = SLOTS_PER_TILE // SLOT_CHUNK - used
        bit = ZERO_CHUNKS
        while bit >= 1:
            row = g * SLOTS_PER_TILE + (used + (n_free & ~(2 * bit - 1))) * SLOT_CHUNK
            cp = pltpu.make_async_copy(
                zbuf.at[pl.ds(0, bit * SLOT_CHUNK), :],
                ys_hbm.at[pl.ds(pl.multiple_of(row, SLOT_CHUNK), bit * SLOT_CHUNK), :], zsem)
            pl.when((g < blocks) & ((n_free & bit) != 0))(cp.wait if wait else cp.start)
            bit //= 2

    @pl.when(g == 0)
    def _():
        zbuf[...] = jnp.zeros_like(zbuf)
        gather(0)

    pl.when(g + 1 < n_steps)(functools.partial(gather, g + 1))
    zero_fill(wait=False)

    @pl.when((g == 0) | (te_ref[g] != te_ref[jnp.maximum(g - 1, 0)]))
    def _():
        wgu[:, 0:D_EXPERT] = wg_ref[...].astype(BF16)
        wgu[:, D_EXPERT:2 * D_EXPERT] = wu_ref[...].astype(BF16)
        wdn[...] = wd_ref[...].astype(BF16)

    half = g % 2
    n_chunks = nch_ref[g]
    n_prev = jnp.where(g > 0, nch_ref[jnp.maximum(g - 1, 0)], 0)

    @pl.when(n_chunks > 0)
    def _():
        x_ref = xbuf.at[pl.ds(pl.multiple_of(half * EXPERT_TILE, EXPERT_TILE), EXPERT_TILE), :]
        pltpu.make_async_copy(xs_hbm.at[pl.ds(0, EXPERT_TILE), :], x_ref, gsem.at[half]).wait()
        ab = _dot(x_ref[...].astype(BF16), wgu[...])
        a = ab[:, 0:D_EXPERT]
        hid = (a * jax.nn.sigmoid(a)) * ab[:, D_EXPERT:2 * D_EXPERT]
        out = _dot(hid.astype(BF16), wdn[...])
        wait_scatter(n_prev)
        obuf[...] = out

        def scatter(c, carry):
            pltpu.make_async_copy(_chunk(obuf, c * SLOT_CHUNK), _chunk(ys_hbm, rows_ref[g * cpt + c]), ssem).start()
            return carry

        lax.fori_loop(0, n_chunks, scatter, 0)

    pl.when(n_chunks == 0)(functools.partial(wait_scatter, n_prev))
    pl.when(g == n_steps - 1)(functools.partial(wait_scatter, n_chunks))
    zero_fill(wait=True)


def _grouped_experts(tile_expert, tile_chunks, rows, used, xs, wg, wu, wd):
    n_steps = tile_expert.shape[0]
    weights = lambda shape: pl.BlockSpec((None,) + shape, lambda g, te, *_: (te[g], 0, 0))
    any_spec = pl.BlockSpec(memory_space=pl.ANY)
    return pl.pallas_call(
        _grouped_expert_kernel,
        grid_spec=pltpu.PrefetchScalarGridSpec(
            num_scalar_prefetch=4, grid=(n_steps,),
            in_specs=[any_spec, weights((D_MODEL, D_EXPERT)), weights((D_MODEL, D_EXPERT)),
                      weights((D_EXPERT, D_MODEL))],
            out_specs=any_spec,
            scratch_shapes=[pltpu.VMEM((2 * EXPERT_TILE, D_MODEL), F32), pltpu.VMEM((EXPERT_TILE, D_MODEL), F32),
                            pltpu.VMEM((ZERO_CHUNKS * SLOT_CHUNK, D_MODEL), F32),
                            pltpu.VMEM((D_MODEL, 2 * D_EXPERT), BF16), pltpu.VMEM((D_EXPERT, D_MODEL), BF16),
                            pltpu.SemaphoreType.DMA((2,)), pltpu.SemaphoreType.DMA, pltpu.SemaphoreType.DMA]),
        out_shape=jax.ShapeDtypeStruct(xs.shape, F32),
        compiler_params=pltpu.CompilerParams(dimension_semantics=("arbitrary",), vmem_limit_bytes=VMEM_LIMIT),
        name="experts",
    )(tile_expert, tile_chunks, rows, used, xs, wg, wu, wd)


def _unsort_kernel(h_ref, route_ref, ys_ref, y_ref):
    tm = h_ref.shape[0]
    route = route_ref[...]
    slot_id = lax.broadcasted_iota(I32, (tm, SLOTS_PER_TILE), 1).astype(F32)
    sel = (jnp.where(slot_id == route[:, 0:1], route[:, 2:3], 0.0) +
           jnp.where(slot_id == route[:, 1:2], route[:, 3:4], 0.0))
    y_ref[...] = h_ref[...] + _dot(sel.astype(BF16), ys_ref[...].astype(BF16))


def _unsort(h, route, ys, first_block):
    n = h.shape[0]
    tm = min(MIX_TILE, n)
    return pl.pallas_call(
        _unsort_kernel,
        grid=(n // tm,),
        in_specs=[pl.BlockSpec((tm, D_MODEL), lambda i: (i, 0)), pl.BlockSpec((tm, SUBLANES), lambda i: (i, 0)),
                  pl.BlockSpec((SLOTS_PER_TILE, D_MODEL), lambda i: (first_block + i, 0))],
        out_specs=pl.BlockSpec((tm, D_MODEL), lambda i: (i, 0)),
        out_shape=jax.ShapeDtypeStruct((n, D_MODEL), F32),
        compiler_params=pltpu.CompilerParams(dimension_semantics=("arbitrary",), vmem_limit_bytes=VMEM_LIMIT),
        name="combine",
    )(h, route, ys)


def _grouped_moe(parts, xs, wg, wu, wd):
    cpt = EXPERT_TILE // SLOT_CHUNK
    nck = jnp.concatenate([p[2] for p in parts])[:, :N_EXPERTS]
    ck0 = jnp.concatenate([p[3] for p in parts])[:, :N_EXPERTS]
    blocks = nck.shape[0]
    cum_incl = jnp.cumsum(nck, axis=0)
    cum_excl = cum_incl - nck
    per_expert = cum_incl[blocks - 1]
    tiles_e = (per_expert + cpt - 1) // cpt
    ends = jnp.cumsum(tiles_e)
    n_steps = blocks * (SLOTS_PER_TILE // SLOT_CHUNK) // cpt + N_EXPERTS
    step = jnp.arange(n_steps, dtype=I32)
    tile_expert = jnp.minimum(jnp.sum((step[:, None] >= ends[None, :]).astype(I32), axis=1), N_EXPERTS - 1)
    first_step = (ends - tiles_e)[tile_expert]
    mine = per_expert[tile_expert]
    tile_chunks = jnp.clip(mine - (step - first_step) * cpt, 0, cpt)
    tile_chunks = jnp.where(step < ends[N_EXPERTS - 1], tile_chunks, 0).astype(I32)
    p = (step - first_step)[:, None] * cpt + jnp.arange(cpt, dtype=I32)[None, :]
    p = jnp.clip(p, 0, jnp.maximum(mine - 1, 0)[:, None])[:, :, None]
    lo = cum_excl.T[tile_expert][:, None, :]
    hi = cum_incl.T[tile_expert][:, None, :]
    local = ck0.T[tile_expert][:, None, :] + p - lo
    block_row = (jnp.arange(blocks, dtype=I32) * SLOTS_PER_TILE)[None, None, :]
    rows = jnp.sum(jnp.where((lo <= p) & (p < hi), block_row + local * SLOT_CHUNK, 0), axis=-1).astype(I32)
    used = (ck0[:, N_EXPERTS - 1] + nck[:, N_EXPERTS - 1]).astype(I32)
    ys = _grouped_experts(tile_expert, tile_chunks, rows.reshape(-1), used, xs, wg, wu, wd)
    outs, first = [], 0
    for h, route, nck_p, _ in parts:
        outs.append(_unsort(h, route, ys, first))
        first += nck_p.shape[0]
    return outs


def kernel(x_prompt, x_sample, cache_k, cache_v, norm1_g, w_in, q_gain, k_gain, v_gain, w_spatial, b_spatial,
           out_gain_a, out_gain_b, w_out, norm2_g, w_router1, b_router1, w_router2, b_router2, w_up, w_gate,
           w_down):
    depth = norm1_g.shape[0]
    assert depth == 1
    l = 0
    batch, seq, _ = x_prompt.shape
    nb, dec_seq, _ = x_sample.shape
    assert dec_seq == 1

    g1 = norm1_g[l].reshape(1, D_MODEL)
    win = w_in[l].astype(BF16)
    gains = jnp.stack([q_gain[l].reshape(WIDTH), k_gain[l].reshape(WIDTH), v_gain[l].reshape(WIDTH)])
    head_of_lane = jnp.arange(WIDTH) // HEAD_DIM
    bd = (head_of_lane[:, None] == head_of_lane[None, :]).astype(BF16)
    ws_tril = jnp.tril(w_spatial[l])
    wcat = jnp.concatenate([ws_tril[0::2], ws_tril[1::2]], axis=-1).astype(BF16)
    bs = b_spatial[l]
    bias = jnp.where(jnp.arange(LANES)[None, None, :] < HEAD_DIM, bs[0::2][:, :, None], bs[1::2][:, :, None])
    avec = jnp.repeat(w_spatial[l][:, 0, 0], HEAD_DIM).reshape(1, WIDTH)
    bvec = jnp.repeat(bs[:, 0], HEAD_DIM).reshape(1, WIDTH)
    ga = out_gain_a[l].reshape(1, WIDTH)
    gb = out_gain_b[l].reshape(1, WIDTH)
    wout = w_out[l].astype(BF16)
    g2 = norm2_g[l].reshape(1, D_MODEL)
    wr = jnp.zeros((D_MODEL, LANES), F32)
    wr = wr.at[:, :N_EXPERTS].set(jnp.transpose(w_router2[l], (1, 0, 2)).reshape(D_MODEL, N_EXPERTS))
    wr = wr.at[:, N_EXPERTS:N_EXPERTS + N_GROUPS].set(w_router1[l]).astype(BF16)
    br = jnp.zeros((1, LANES), F32)
    br = br.at[0, :N_EXPERTS].set(b_router2[l].reshape(N_EXPERTS))
    br = br.at[0, N_EXPERTS:N_EXPERTS + N_GROUPS].set(b_router1[l])
    lane_head = (jnp.arange(LANES)[:, None] == head_of_lane[None, :]).astype(BF16)
    expand = jnp.concatenate([lane_head, lane_head], axis=0)
    wg = w_gate[l].reshape(N_EXPERTS, D_MODEL, D_EXPERT)
    wu = w_up[l].reshape(N_EXPERTS, D_MODEL, D_EXPERT)
    wd = w_down[l].reshape(N_EXPERTS, D_EXPERT, D_MODEL)

    def tri(t):
        return (jnp.arange(t)[:, None] > jnp.arange(t)[None, :]).astype(BF16)

    xp = x_prompt.reshape(batch * seq, D_MODEL)
    qkv, oa, kt_last, vt_last = _proj_prompt(xp, g1, win, gains, bd, wcat, bias, seq=seq)
    obs, lses = [], []
    for (q, k, v), dil in zip(qkv, DILATIONS):
        o, lse = _band_attention(q, k, v, batch=batch, seq=seq, dil=dil)
        obs.append(o)
        lses.append(lse)
    tmix = min(MIX_TILE, batch * seq)
    upper = (jnp.arange(LANES)[:, None] < jnp.arange(LANES)[None, :]).astype(BF16)
    tmix_s = min(MIX_TILE, nb)
    h, sorted_rows, route, nck, ck0 = _mix(xp, oa, obs, lses, DILATIONS, ga, gb, wout, g2, wr, br, expand,
                                            tri(tmix), upper, spare_blocks=nb // tmix_s)
    kept = min(MAX_WINDOW, seq)
    to_cache = lambda t: jnp.transpose(t.reshape(1, batch, N_HEADS, HEAD_DIM, kept), (0, 1, 4, 2, 3))

    xs = x_sample.reshape(nb, D_MODEL)
    qs, ks, vs, vgs, oas = _proj_sample(xs, g1, win, gains, bd, avec, bvec)
    cols = lambda t: jnp.transpose(t.reshape(nb, N_HEADS, HEAD_DIM), (0, 2, 1))
    feature_major = lambda c: jnp.transpose(c, (0, 2, 3, 1)).reshape(nb, WIDTH, c.shape[1])
    ot = _sample_attention(cols(qs), cols(ks), cols(vs), feature_major(cache_k[l]), feature_major(cache_v[l]))
    obs_s = jnp.transpose(ot, (0, 2, 1)).reshape(nb, WIDTH)
    hs, sorted_rows, route_s, nck_s, ck0_s = _mix(xs, oas, [obs_s], [], (), ga, gb, wout, g2, wr, br, expand,
                                                  tri(tmix_s), upper, sorted_buf=sorted_rows,
                                                  first_block=batch * seq // tmix)

    y_prompt, y_sample = _grouped_moe([(h, route, nck, ck0), (hs, route_s, nck_s, ck0_s)], sorted_rows,
                                      wg, wu, wd)
    to5 = lambda t: t.reshape(1, nb, 1, N_HEADS, HEAD_DIM)
    return (y_prompt.reshape(batch, seq, D_MODEL), y_sample.reshape(nb, 1, D_MODEL),
            to_cache(kt_last), to_cache(vt_last), to5(ks), to5(vs), to5(vgs))
```

```python
import functools

import jax
import jax.numpy as jnp
from jax import lax
from jax.experimental import pallas as pl
from jax.experimental.pallas import tpu as pltpu

F32 = jnp.float32
BF16 = jnp.bfloat16
I32 = jnp.int32

D_MODEL = 1024
HEAD_DIM = 64
N_HEADS = 8
WIDTH = N_HEADS * HEAD_DIM
N_PAIRS = N_HEADS // 2
CHUNK = 128
N_WIN = 128
DILATIONS = (1, 4, 16)
MAX_WINDOW = 2048
N_GROUPS = 4
EXPERTS_PER_GROUP = 8
N_EXPERTS = N_GROUPS * EXPERTS_PER_GROUP
D_EXPERT = D_MODEL // 4
EPS = 1e-6

LANES = 128
SUBLANES = 8
VMEM_LIMIT = 48 * 1024 * 1024

PROJ_TILE = 512
ATTN_TILE = 512
MIX_TILE = 256
EXPERT_TILE = 512
SLOT_CHUNK = 2 * SUBLANES
SLOTS_PER_TILE = 1024
ZERO_CHUNKS = 64


def _dot(a, b):
    return jnp.dot(a, b, preferred_element_type=F32)


def _gelu(x):
    return 0.5 * x * (1.0 + jnp.tanh(0.7978845608028654 * (x + 0.044715 * (x * x * x))))


def _head_rms(t, gain, bd):
    ss = _dot((t * t).astype(BF16), bd)
    return t * lax.rsqrt(ss * (1.0 / HEAD_DIM) + EPS) * gain


def _proj_common(x_ref, g1_ref, win_ref, gains_ref, bd_ref):
    x = x_ref[...]
    r = lax.rsqrt(jnp.mean(x * x, axis=-1, keepdims=True) + EPS)
    xn = (x * r * g1_ref[...]).astype(BF16)
    bd = bd_ref[...]
    q = _head_rms(_dot(xn, win_ref[:, 0:WIDTH]), gains_ref[0:1, :], bd) * (HEAD_DIM ** -0.5)
    k = _head_rms(_dot(xn, win_ref[:, WIDTH:2 * WIDTH]), gains_ref[1:2, :], bd)
    v = _dot(xn, win_ref[:, 2 * WIDTH:3 * WIDTH])
    u = _gelu(_dot(xn, win_ref[:, 3 * WIDTH:4 * WIDTH]))
    vg = _head_rms(_gelu(_dot(xn, win_ref[:, 4 * WIDTH:5 * WIDTH])), gains_ref[2:3, :], bd)
    return q, k, v, u, vg


def _proj_prompt_kernel(x_ref, g1_ref, win_ref, gains_ref, bd_ref, wcat_ref, bias_ref, *rest,
                        tiles_per_seq, first_kept_tile):
    stream_refs = rest[:3 * len(DILATIONS)]
    oa_ref, kl_ref, vl_ref, slab_a, slab_b = rest[3 * len(DILATIONS):]
    q, k, v, u, vg = _proj_common(x_ref, g1_ref, win_ref, gains_ref, bd_ref)
    tm = x_ref.shape[0]
    for a, z in enumerate((q, k, v)):
        cur, nxt = slab_a, slab_b
        for p in range(N_PAIRS):
            cur[p] = z[:, p * LANES:(p + 1) * LANES]
        for c, dil in enumerate(DILATIONS):
            out_ref = stream_refs[3 * c + a]
            if dil == 1:
                out_ref[...] = z.astype(BF16)
                continue
            prev = DILATIONS[c - 1]
            rows, rows_prev = tm // dil, tm // prev
            for r in range(dil):
                for p in range(N_PAIRS):
                    part = cur[p, pl.ds((r % prev) * rows_prev + r // prev, rows, stride=dil // prev), :]
                    out_ref[:, r * WIDTH + p * LANES:r * WIDTH + (p + 1) * LANES] = part.astype(BF16)
                    if c + 1 < len(DILATIONS):
                        nxt[p, r * rows:(r + 1) * rows, :] = part
            cur, nxt = nxt, cur

    @pl.when(pl.program_id(0) % tiles_per_seq >= first_kept_tile)
    def _():
        kl_ref[...] = k.T
        vl_ref[...] = v.T

    vgb = vg.astype(BF16)
    lane = lax.broadcasted_iota(I32, (CHUNK, LANES), 1)
    tm = x_ref.shape[0]
    for c in range(tm // CHUNK):
        rows = slice(c * CHUNK, (c + 1) * CHUNK)
        for p in range(N_PAIRS):
            cols = slice(p * LANES, (p + 1) * LANES)
            vp = vgb[rows, cols]
            zero = jnp.zeros_like(vp)
            rhs = jnp.concatenate([jnp.where(lane < HEAD_DIM, vp, zero),
                                   jnp.where(lane >= HEAD_DIM, vp, zero)], axis=0)
            mixed = _dot(wcat_ref[p], rhs) + bias_ref[p]
            oa_ref[rows, cols] = (u[rows, cols] * mixed).astype(BF16)


def _proj_sample_kernel(x_ref, g1_ref, win_ref, gains_ref, bd_ref, avec_ref, bvec_ref,
                        q_ref, k_ref, v_ref, vg_ref, oa_ref):
    q, k, v, u, vg = _proj_common(x_ref, g1_ref, win_ref, gains_ref, bd_ref)
    q_ref[...] = q
    k_ref[...] = k
    v_ref[...] = v
    vg_ref[...] = vg
    oa_ref[...] = (u * (avec_ref[...] * vg + bvec_ref[...])).astype(BF16)


def _const_spec(shape):
    return pl.BlockSpec(shape, lambda *_: (0,) * len(shape))


def _proj_prompt(x2, g1, win, gains, bd, wcat, bias, *, seq):
    n = x2.shape[0]
    tm = PROJ_TILE
    tiles_per_seq = seq // tm
    kept = min(MAX_WINDOW, seq)
    first_kept_tile = tiles_per_seq - kept // tm

    def kept_map(i):
        return (i // tiles_per_seq, 0, jnp.maximum(i % tiles_per_seq - first_kept_tile, 0))

    row_spec = pl.BlockSpec((tm, WIDTH), lambda i: (i, 0))
    kept_spec = pl.BlockSpec((None, WIDTH, tm), kept_map)
    stream_specs = [pl.BlockSpec((tm // dil, dil * WIDTH), lambda i: (i, 0)) for dil in DILATIONS for _ in range(3)]
    stream_shapes = [jax.ShapeDtypeStruct((n // dil, dil * WIDTH), BF16) for dil in DILATIONS for _ in range(3)]
    outs = pl.pallas_call(
        functools.partial(_proj_prompt_kernel, tiles_per_seq=tiles_per_seq, first_kept_tile=first_kept_tile),
        grid=(n // tm,),
        in_specs=[pl.BlockSpec((tm, D_MODEL), lambda i: (i, 0)),
                  _const_spec((1, D_MODEL)), _const_spec((D_MODEL, 5 * WIDTH)), _const_spec((3, WIDTH)),
                  _const_spec((WIDTH, WIDTH)), _const_spec((N_PAIRS, CHUNK, 2 * CHUNK)),
                  _const_spec((N_PAIRS, CHUNK, LANES))],
        out_specs=stream_specs + [row_spec, kept_spec, kept_spec],
        out_shape=stream_shapes + [jax.ShapeDtypeStruct((n, WIDTH), BF16)] +
                  [jax.ShapeDtypeStruct((n // seq, WIDTH, kept), F32)] * 2,
        scratch_shapes=[pltpu.VMEM((N_PAIRS, tm, LANES), F32)] * 2,
        compiler_params=pltpu.CompilerParams(dimension_semantics=("arbitrary",), vmem_limit_bytes=VMEM_LIMIT),
        name="proj_prompt",
    )(x2, g1, win, gains, bd, wcat, bias)
    n_streams = 3 * len(DILATIONS)
    qkv = [outs[3 * c:3 * c + 3] for c in range(len(DILATIONS))]
    return (qkv, *outs[n_streams:])


def _proj_sample(x2, g1, win, gains, bd, avec, bvec):
    n = x2.shape[0]
    full = _const_spec((n, WIDTH))
    return pl.pallas_call(
        _proj_sample_kernel,
        grid=(1,),
        in_specs=[_const_spec((n, D_MODEL)), _const_spec((1, D_MODEL)), _const_spec((D_MODEL, 5 * WIDTH)),
                  _const_spec((3, WIDTH)), _const_spec((WIDTH, WIDTH)), _const_spec((1, WIDTH)),
                  _const_spec((1, WIDTH))],
        out_specs=[full] * 5,
        out_shape=[jax.ShapeDtypeStruct((n, WIDTH), F32)] * 4 + [jax.ShapeDtypeStruct((n, WIDTH), BF16)],
        compiler_params=pltpu.CompilerParams(dimension_semantics=("arbitrary",), vmem_limit_bytes=VMEM_LIMIT),
        name="proj_sample",
    )(x2, g1, win, gains, bd, avec, bvec)


def _band_attn_kernel(q_ref, k_ref, v_ref, o_ref, lse_ref, kbuf, vbuf, *, tq):
    j = pl.program_id(2)
    nblk = tq // N_WIN

    @pl.when(j == 0)
    def _():
        kbuf[0:N_WIN, :] = jnp.zeros((N_WIN, WIDTH), BF16)
        vbuf[...] = jnp.ones(vbuf.shape, BF16)

    @pl.when(j > 0)
    def _():
        kbuf[0:N_WIN, :] = kbuf[tq:tq + N_WIN, :]
        vbuf[0:N_WIN, :] = vbuf[tq:tq + N_WIN, :]

    kbuf[N_WIN:N_WIN + tq, :] = k_ref[...]
    for p in range(N_PAIRS):
        vbuf[N_WIN:N_WIN + tq, 2 * p * LANES:(2 * p + 1) * LANES] = v_ref[:, p * LANES:(p + 1) * LANES]

    qi = lax.broadcasted_iota(I32, (N_WIN, 2 * N_WIN), 0)
    ki = lax.broadcasted_iota(I32, (N_WIN, 2 * N_WIN), 1)
    band = (ki >= qi) & (ki <= qi + N_WIN)
    lane = lax.broadcasted_iota(I32, (N_WIN, LANES), 1)
    low_half = lane < HEAD_DIM

    def block(jb, carry):
        q0 = pl.multiple_of(jb * N_WIN, N_WIN)
        valid = band & ((ki >= N_WIN) | (j * nblk + jb > 0))
        valid2 = jnp.concatenate([valid, valid], axis=0)
        lse_tile = jnp.zeros((N_WIN, LANES), F32)
        for p in range(N_PAIRS):
            cols = slice(p * LANES, (p + 1) * LANES)
            qp = q_ref[pl.ds(q0, N_WIN), cols]
            kp = kbuf[pl.ds(q0, 2 * N_WIN), cols]
            vp = vbuf[pl.ds(q0, 2 * N_WIN), 2 * p * LANES:(2 * p + 2) * LANES]
            zero = jnp.zeros_like(qp)
            q2 = jnp.concatenate([jnp.where(low_half, qp, zero), jnp.where(low_half, zero, qp)], axis=0)
            s = lax.dot_general(q2, kp, (((1,), (1,)), ((), ())), preferred_element_type=F32)
            s = jnp.where(valid2, s, -jnp.inf)
            m = jnp.max(s, axis=-1, keepdims=True)
            e = jnp.exp(s - m)
            ov = _dot(e.astype(BF16), vp)
            den = ov[:, LANES:2 * LANES]
            o = ov[:, 0:LANES] / den
            lse = m + jnp.log(den)
            lse_tile = jnp.where(lane == 2 * p, lse[0:N_WIN],
                                 jnp.where(lane == 2 * p + 1, lse[N_WIN:2 * N_WIN], lse_tile))
            o_ref[pl.ds(q0, N_WIN), cols] = jnp.where(low_half, o[0:N_WIN], o[N_WIN:2 * N_WIN]).astype(BF16)
        lse_ref[pl.ds(q0, N_WIN), :] = lse_tile
        return carry

    lax.fori_loop(0, nblk, block, 0)


def _band_attention(q2, k2, v2, *, batch, seq, dil):
    length = seq // dil
    tq = min(ATTN_TILE, length)
    view = lambda a: a.reshape(batch, length, a.shape[-1])
    spec = pl.BlockSpec((None, tq, WIDTH), lambda b, r, j: (b, j, r))
    lse_spec = pl.BlockSpec((None, tq, LANES), lambda b, r, j: (b, j, r))
    o, lse = pl.pallas_call(
        functools.partial(_band_attn_kernel, tq=tq),
        grid=(batch, dil, length // tq),
        in_specs=[spec, spec, spec],
        out_specs=[spec, lse_spec],
        out_shape=[jax.ShapeDtypeStruct((batch, length, dil * WIDTH), BF16),
                   jax.ShapeDtypeStruct((batch, length, dil * LANES), F32)],
        scratch_shapes=[pltpu.VMEM((N_WIN + tq, WIDTH), BF16), pltpu.VMEM((N_WIN + tq, 2 * WIDTH), BF16)],
        compiler_params=pltpu.CompilerParams(dimension_semantics=("arbitrary", "arbitrary", "arbitrary"),
                                             vmem_limit_bytes=VMEM_LIMIT),
        name=f"band_attn_d{dil}",
    )(view(q2), view(k2), view(v2))
    return o.reshape(batch * length, dil * WIDTH), lse.reshape(batch * length, dil * LANES)


def _sample_attn_kernel(qt_ref, knt_ref, vnt_ref, kt_ref, vt_ref, ot_ref):
    qt = qt_ref[...]
    vnt = vnt_ref[...]
    s_new = jnp.sum(qt * knt_ref[...], axis=0, keepdims=True)
    wb = kt_ref.shape[1]
    heads = range(N_HEADS)
    head_rows = lambda h: slice(h * HEAD_DIM, (h + 1) * HEAD_DIM)
    s = jnp.concatenate([jnp.sum(kt_ref[head_rows(h), :] * qt[:, h:h + 1], axis=0, keepdims=True) for h in heads],
                        axis=0)
    s0 = jnp.concatenate([s_new[:, h:h + 1] for h in heads], axis=0)
    pos = lax.broadcasted_iota(I32, (N_HEADS, wb), 1)
    es, e0s, dens, lses = [], [], [], []
    for dil in DILATIONS:
        lo = wb - N_WIN * dil
        sc = s[:, lo:]
        valid = (pos[:, lo:] & (dil - 1)) == 0
        m = jnp.maximum(jnp.max(jnp.where(valid, sc, -jnp.inf), axis=-1, keepdims=True), s0)
        e = jnp.where(valid, jnp.exp(sc - m), 0.0)
        e0 = jnp.exp(s0 - m)
        den = jnp.sum(e, axis=-1, keepdims=True) + e0
        es.append(e)
        e0s.append(e0)
        dens.append(den)
        lses.append(m + jnp.log(den))
    mm = jnp.maximum(jnp.maximum(lses[0], lses[1]), lses[2])
    ws = [jnp.exp(l - mm) for l in lses]
    tot = ws[0] + ws[1] + ws[2]
    coef = [w / (tot * den) for w, den in zip(ws, dens)]
    w1, w4, w16 = [e * c for e, c in zip(es, coef)]
    w_new = coef[0] * e0s[0] + coef[1] * e0s[1] + coef[2] * e0s[2]
    n1, n4 = N_WIN * DILATIONS[0], N_WIN * DILATIONS[1]
    w_pos = jnp.concatenate([w16[:, :wb - n4], w16[:, wb - n4:wb - n1] + w4[:, :n4 - n1],
                             w16[:, wb - n1:] + w4[:, n4 - n1:] + w1], axis=1)
    head_lane = lax.broadcasted_iota(I32, (HEAD_DIM, N_HEADS), 1)
    ot = jnp.zeros((HEAD_DIM, N_HEADS), F32)
    for h in heads:
        col = (jnp.sum(vt_ref[head_rows(h), :] * w_pos[h:h + 1, :], axis=-1, keepdims=True) +
               w_new[h:h + 1, :] * vnt[:, h:h + 1])
        ot = jnp.where(head_lane == h, col, ot)
    ot_ref[...] = ot


def _sample_attention(qt, knt, vnt, kt, vt):
    nb, _, wb = kt.shape
    assert wb == MAX_WINDOW, "cache window must cover every dilated key"
    tok = pl.BlockSpec((None, HEAD_DIM, N_HEADS), lambda b: (b, 0, 0))
    cache = pl.BlockSpec((None, WIDTH, wb), lambda b: (b, 0, 0))
    return pl.pallas_call(
        _sample_attn_kernel,
        grid=(nb,),
        in_specs=[tok, tok, tok, cache, cache],
        out_specs=tok,
        out_shape=jax.ShapeDtypeStruct((nb, HEAD_DIM, N_HEADS), F32),
        compiler_params=pltpu.CompilerParams(dimension_semantics=("arbitrary",), vmem_limit_bytes=VMEM_LIMIT),
        name="sample_attn",
    )(qt, knt, vnt, kt, vt)


def _split_hi_lo(w):
    hi = w.astype(BF16)
    return jnp.concatenate([hi, (w - hi.astype(F32)).astype(BF16)], axis=-1)


def _token_order(ref, dil, width, slab, tm):
    if dil == 1:
        return ref[...].astype(F32)
    n_slabs = width // LANES
    for r in range(dil):
        for p in range(n_slabs):
            cols = slice(r * width + p * LANES, r * width + (p + 1) * LANES)
            slab[p, pl.ds(r, tm // dil, stride=dil), :] = ref[:, cols].astype(F32)
    return jnp.concatenate([slab[p] for p in range(n_slabs)], axis=1)


def _mix_kernel(*refs, dils, n_tiles, aliased):
    n_cfg = max(len(dils), 1)
    n_lse = len(dils)
    x_ref, oa_ref = refs[0], refs[1]
    o_refs = refs[2:2 + n_cfg]
    lse_refs = refs[2 + n_cfg:2 + n_cfg + n_lse]
    rest = refs[2 + n_cfg + n_lse:]
    ga_ref, gb_ref, wout_ref, g2_ref, wr_ref, br_ref, expand_ref, tri_ref, upper_ref = rest[:9]
    h_ref, xs_ref, route_ref, nchunk_ref, chunk0_ref, slab = rest[9 + int(aliased):]
    tm = x_ref.shape[0]

    if not dils:
        ob = o_refs[0][...].astype(F32)
    else:
        lses = [_token_order(r, dil, LANES, slab, tm) for r, dil in zip(lse_refs, dils)]
        mm = functools.reduce(jnp.maximum, lses)
        ws = [jnp.exp(l - mm) for l in lses]
        tot = functools.reduce(lambda a, b: a + b, ws)
        ob = jnp.zeros((tm, WIDTH), F32)
        for w, o_ref, dil in zip(ws, o_refs, dils):
            ob = ob + _dot(_split_hi_lo(w / tot), expand_ref[...]) * _token_order(o_ref, dil, WIDTH, slab, tm)

    oa = oa_ref[...].astype(F32)
    ya = oa * lax.rsqrt(jnp.mean(oa * oa, axis=-1, keepdims=True) + EPS) * ga_ref[...]
    yb = ob * lax.rsqrt(jnp.mean(ob * ob, axis=-1, keepdims=True) + EPS) * gb_ref[...]
    cat = jnp.concatenate([ya, yb], axis=-1).astype(BF16)
    h = x_ref[...] + _dot(cat, wout_ref[...])
    h_ref[...] = h
    hn = (h * lax.rsqrt(jnp.mean(h * h, axis=-1, keepdims=True) + EPS) * g2_ref[...]).astype(BF16)

    logits = _dot(hn, wr_ref[...]) + br_ref[...]
    lane = lax.broadcasted_iota(I32, (tm, LANES), 1)
    lane_f = lane.astype(F32)
    neg = -jnp.inf
    big = float(LANES)
    coarse = (lane >= N_EXPERTS) & (lane < N_EXPERTS + N_GROUPS)
    lg = jnp.where(coarse, logits, neg)
    mx = jnp.max(lg, axis=-1, keepdims=True)
    g_lane = jnp.min(jnp.where(lg == mx, lane_f, big), axis=-1, keepdims=True)
    p_star = 1.0 / jnp.sum(jnp.exp(lg - mx), axis=-1, keepdims=True)
    lo = (g_lane - float(N_EXPERTS)) * float(EXPERTS_PER_GROUP)
    lf = jnp.where((lane_f >= lo) & (lane_f < lo + float(EXPERTS_PER_GROUP)), logits, neg)
    v1 = jnp.max(lf, axis=-1, keepdims=True)
    i1 = jnp.min(jnp.where(lf == v1, lane_f, big), axis=-1, keepdims=True)
    lf2 = jnp.where(lane_f == i1, neg, lf)
    v2 = jnp.max(lf2, axis=-1, keepdims=True)
    i2 = jnp.min(jnp.where(lf2 == v2, lane_f, big), axis=-1, keepdims=True)
    e21 = jnp.exp(v2 - v1)
    w1 = p_star / (1.0 + e21)
    w2 = p_star * e21 / (1.0 + e21)

    sel1 = lane_f == i1
    sel2 = lane_f == i2
    onehot = jnp.where(sel1 | sel2, 1.0, 0.0)
    rank = _dot(tri_ref[...], onehot.astype(BF16))
    count = jnp.sum(onehot, axis=0, keepdims=True)
    chunks = jnp.floor((count + float(SLOT_CHUNK - 1)) * (1.0 / SLOT_CHUNK))
    chunk0 = _dot(jnp.broadcast_to(chunks, (SUBLANES, LANES)).astype(BF16), upper_ref[...])[0:1, :]
    slot = rank + chunk0 * float(SLOT_CHUNK)
    s1 = jnp.sum(jnp.where(sel1, slot, 0.0), axis=-1, keepdims=True)
    s2 = jnp.sum(jnp.where(sel2, slot, 0.0), axis=-1, keepdims=True)
    i = pl.program_id(0)
    tile = jnp.minimum(i, n_tiles - 1)
    nchunk_ref[pl.ds(tile, 1), :] = chunks.astype(I32)
    chunk0_ref[pl.ds(tile, 1), :] = chunk0.astype(I32)

    s1_row = jnp.broadcast_to(s1, (tm, LANES)).T[0:1, :]
    s2_row = jnp.broadcast_to(s2, (tm, LANES)).T[0:1, :]
    slot_id = lax.broadcasted_iota(I32, (SLOTS_PER_TILE, tm), 0).astype(F32)
    perm = jnp.where((slot_id == s1_row) | (slot_id == s2_row), 1.0, 0.0).astype(BF16)
    xs_ref[...] = jnp.where(i < n_tiles, _dot(perm, hn), 0.0).astype(BF16)

    col = lax.broadcasted_iota(I32, (tm, SUBLANES), 1)
    route = jnp.zeros((tm, SUBLANES), F32)
    for idx, val in enumerate((s1, s2, w1, w2)):
        route = jnp.where(col == idx, val, route)
    route_ref[...] = route


def _mix(x2, oa, obs, lses, dils, ga, gb, wout, g2, wr, br, expand, tri, upper, *, sorted_buf=None,
         first_block=0, spare_blocks=0):
    n = x2.shape[0]
    tm = min(MIX_TILE, n)
    n_tiles = n // tm
    last = n_tiles - 1
    row = lambda w: pl.BlockSpec((tm, w), lambda i: (jnp.minimum(i, last), 0))
    stream = lambda w, dil: pl.BlockSpec((tm // dil, dil * w), lambda i: (jnp.minimum(i, last), 0))
    in_specs = ([row(D_MODEL), row(WIDTH)] + [stream(WIDTH, dil) for dil in (dils or (1,))] +
                [stream(LANES, dil) for dil in dils] +
                [_const_spec((1, WIDTH)), _const_spec((1, WIDTH)), _const_spec((D_MODEL, D_MODEL)),
                 _const_spec((1, D_MODEL)), _const_spec((D_MODEL, LANES)), _const_spec((1, LANES)),
                 _const_spec((2 * LANES, WIDTH)), _const_spec((tm, tm)), _const_spec((LANES, LANES))])
    args = [x2, oa, *obs, *lses, ga, gb, wout, g2, wr, br, expand, tri, upper]
    aliases = {}
    if sorted_buf is None:
        sorted_shape = ((n_tiles + spare_blocks) * SLOTS_PER_TILE, D_MODEL)
    else:
        assert spare_blocks == 0
        sorted_shape = sorted_buf.shape
        aliases = {len(args): 1}
        in_specs.append(pl.BlockSpec(memory_space=pl.ANY))
        args.append(sorted_buf)
    return pl.pallas_call(
        functools.partial(_mix_kernel, dils=tuple(dils), n_tiles=n_tiles, aliased=sorted_buf is not None),
        grid=(n_tiles + spare_blocks,),
        in_specs=in_specs,
        out_specs=[row(D_MODEL), pl.BlockSpec((SLOTS_PER_TILE, D_MODEL), lambda i: (first_block + i, 0)),
                   row(SUBLANES), _const_spec((n_tiles, LANES)), _const_spec((n_tiles, LANES))],
        out_shape=[jax.ShapeDtypeStruct((n, D_MODEL), F32), jax.ShapeDtypeStruct(sorted_shape, BF16),
                   jax.ShapeDtypeStruct((n, SUBLANES), F32),
                   jax.ShapeDtypeStruct((n_tiles, LANES), I32), jax.ShapeDtypeStruct((n_tiles, LANES), I32)],
        input_output_aliases=aliases,
        scratch_shapes=[pltpu.VMEM((N_PAIRS, tm, LANES), F32)],
        compiler_params=pltpu.CompilerParams(dimension_semantics=("arbitrary",), vmem_limit_bytes=VMEM_LIMIT),
        name="mix",
    )(*args)


def _chunk(ref, row):
    return ref.at[pl.ds(pl.multiple_of(row, SLOT_CHUNK), SLOT_CHUNK), :]


def _grouped_expert_kernel(te_ref, nch_ref, rows_ref, used_ref, xs_hbm, wg_ref, wu_ref, wd_ref, ys_hbm,
                           xbuf, obuf, zbuf, wgu, wdn, gsem, ssem, zsem):
    g = pl.program_id(0)
    n_steps = pl.num_programs(0)
    cpt = EXPERT_TILE // SLOT_CHUNK
    blocks = used_ref.shape[0]

    def gather(step):
        half = step % 2

        @pl.when(nch_ref[step] > 0)
        def _():
            def one(c, carry):
                pltpu.make_async_copy(_chunk(xs_hbm, rows_ref[step * cpt + c]),
                                      _chunk(xbuf, (half * cpt + c) * SLOT_CHUNK), gsem.at[half]).start()
                return carry

            lax.fori_loop(0, cpt, one, 0, unroll=8)

    def wait_scatter(n):
        def one(_, carry):
            pltpu.make_async_copy(_chunk(obuf, 0), _chunk(ys_hbm, 0), ssem).wait()
            return carry

        lax.fori_loop(0, n, one, 0)

    def zero_fill(wait):
        used = used_ref[jnp.minimum(g, blocks - 1)]
        n_free = SLOTS_PER_TILE // SLOT_CHUNK - used
        bit = ZERO_CHUNKS
        while bit >= 1:
            row = g * SLOTS_PER_TILE + (used + (n_free & ~(2 * bit - 1))) * SLOT_CHUNK
            cp = pltpu.make_async_copy(
                zbuf.at[pl.ds(0, bit * SLOT_CHUNK), :],
                ys_hbm.at[pl.ds(pl.multiple_of(row, SLOT_CHUNK), bit * SLOT_CHUNK), :], zsem)
            pl.when((g < blocks) & ((n_free & bit) != 0))(cp.wait if wait else cp.start)
            bit //= 2

    @pl.when(g == 0)
    def _():
        zbuf[...] = jnp.zeros_like(zbuf)
        gather(0)

    pl.when(g + 1 < n_steps)(functools.partial(gather, g + 1))
    zero_fill(wait=False)

    @pl.when((g == 0) | (te_ref[g] != te_ref[jnp.maximum(g - 1, 0)]))
    def _():
        wgu[:, 0:D_EXPERT] = wg_ref[...].astype(BF16)
        wgu[:, D_EXPERT:2 * D_EXPERT] = wu_ref[...].astype(BF16)
        wdn[...] = wd_ref[...].astype(BF16)

    half = g % 2
    n_chunks = nch_ref[g]
    n_prev = jnp.where(g > 0, nch_ref[jnp.maximum(g - 1, 0)], 0)

    @pl.when(n_chunks > 0)
    def _():
        x_ref = xbuf.at[pl.ds(pl.multiple_of(half * EXPERT_TILE, EXPERT_TILE), EXPERT_TILE), :]
        pltpu.make_async_copy(xs_hbm.at[pl.ds(0, EXPERT_TILE), :], x_ref, gsem.at[half]).wait()
        ab = _dot(x_ref[...], wgu[...])
        a = ab[:, 0:D_EXPERT]
        hid = (a * jax.nn.sigmoid(a)) * ab[:, D_EXPERT:2 * D_EXPERT]
        out = _dot(hid.astype(BF16), wdn[...])
        wait_scatter(n_prev)
        obuf[...] = out.astype(BF16)

        def scatter(c, carry):
            pltpu.make_async_copy(_chunk(obuf, c * SLOT_CHUNK), _chunk(ys_hbm, rows_ref[g * cpt + c]), ssem).start()
            return carry

        lax.fori_loop(0, n_chunks, scatter, 0)

    pl.when(n_chunks == 0)(functools.partial(wait_scatter, n_prev))
    pl.when(g == n_steps - 1)(functools.partial(wait_scatter, n_chunks))
    zero_fill(wait=True)


def _grouped_experts(tile_expert, tile_chunks, rows, used, xs, wg, wu, wd):
    n_steps = tile_expert.shape[0]
    weights = lambda shape: pl.BlockSpec((None,) + shape, lambda g, te, *_: (te[g], 0, 0))
    any_spec = pl.BlockSpec(memory_space=pl.ANY)
    return pl.pallas_call(
        _grouped_expert_kernel,
        grid_spec=pltpu.PrefetchScalarGridSpec(
            num_scalar_prefetch=4, grid=(n_steps,),
            in_specs=[any_spec, weights((D_MODEL, D_EXPERT)), weights((D_MODEL, D_EXPERT)),
                      weights((D_EXPERT, D_MODEL))],
            out_specs=any_spec,
            scratch_shapes=[pltpu.VMEM((2 * EXPERT_TILE, D_MODEL), BF16), pltpu.VMEM((EXPERT_TILE, D_MODEL), BF16),
                            pltpu.VMEM((ZERO_CHUNKS * SLOT_CHUNK, D_MODEL), BF16),
                            pltpu.VMEM((D_MODEL, 2 * D_EXPERT), BF16), pltpu.VMEM((D_EXPERT, D_MODEL), BF16),
                            pltpu.SemaphoreType.DMA((2,)), pltpu.SemaphoreType.DMA, pltpu.SemaphoreType.DMA]),
        out_shape=jax.ShapeDtypeStruct(xs.shape, BF16),
        compiler_params=pltpu.CompilerParams(dimension_semantics=("arbitrary",), vmem_limit_bytes=VMEM_LIMIT),
        name="experts",
    )(tile_expert, tile_chunks, rows, used, xs, wg, wu, wd)


def _unsort_kernel(h_ref, route_ref, ys_ref, y_ref):
    tm = h_ref.shape[0]
    route = route_ref[...]
    slot_id = lax.broadcasted_iota(I32, (tm, SLOTS_PER_TILE), 1).astype(F32)
    sel = (jnp.where(slot_id == route[:, 0:1], route[:, 2:3], 0.0) +
           jnp.where(slot_id == route[:, 1:2], route[:, 3:4], 0.0))
    y_ref[...] = h_ref[...] + _dot(sel.astype(BF16), ys_ref[...])


def _unsort(h, route, ys, first_block):
    n = h.shape[0]
    tm = min(MIX_TILE, n)
    return pl.pallas_call(
        _unsort_kernel,
        grid=(n // tm,),
        in_specs=[pl.BlockSpec((tm, D_MODEL), lambda i: (i, 0)), pl.BlockSpec((tm, SUBLANES), lambda i: (i, 0)),
                  pl.BlockSpec((SLOTS_PER_TILE, D_MODEL), lambda i: (first_block + i, 0))],
        out_specs=pl.BlockSpec((tm, D_MODEL), lambda i: (i, 0)),
        out_shape=jax.ShapeDtypeStruct((n, D_MODEL), F32),
        compiler_params=pltpu.CompilerParams(dimension_semantics=("arbitrary",), vmem_limit_bytes=VMEM_LIMIT),
        name="combine",
    )(h, route, ys)


def _grouped_moe(parts, xs, wg, wu, wd):
    cpt = EXPERT_TILE // SLOT_CHUNK
    nck = jnp.concatenate([p[2] for p in parts])[:, :N_EXPERTS]
    ck0 = jnp.concatenate([p[3] for p in parts])[:, :N_EXPERTS]
    blocks = nck.shape[0]
    cum_incl = jnp.cumsum(nck, axis=0)
    cum_excl = cum_incl - nck
    per_expert = cum_incl[blocks - 1]
    tiles_e = (per_expert + cpt - 1) // cpt
    ends = jnp.cumsum(tiles_e)
    n_steps = blocks * (SLOTS_PER_TILE // SLOT_CHUNK) // cpt + N_EXPERTS
    step = jnp.arange(n_steps, dtype=I32)
    tile_expert = jnp.minimum(jnp.sum((step[:, None] >= ends[None, :]).astype(I32), axis=1), N_EXPERTS - 1)
    first_step = (ends - tiles_e)[tile_expert]
    mine = per_expert[tile_expert]
    tile_chunks = jnp.clip(mine - (step - first_step) * cpt, 0, cpt)
    tile_chunks = jnp.where(step < ends[N_EXPERTS - 1], tile_chunks, 0).astype(I32)
    p = (step - first_step)[:, None] * cpt + jnp.arange(cpt, dtype=I32)[None, :]
    p = jnp.clip(p, 0, jnp.maximum(mine - 1, 0)[:, None])[:, :, None]
    lo = cum_excl.T[tile_expert][:, None, :]
    hi = cum_incl.T[tile_expert][:, None, :]
    local = ck0.T[tile_expert][:, None, :] + p - lo
    block_row = (jnp.arange(blocks, dtype=I32) * SLOTS_PER_TILE)[None, None, :]
    rows = jnp.sum(jnp.where((lo <= p) & (p < hi), block_row + local * SLOT_CHUNK, 0), axis=-1).astype(I32)
    used = (ck0[:, N_EXPERTS - 1] + nck[:, N_EXPERTS - 1]).astype(I32)
    ys = _grouped_experts(tile_expert, tile_chunks, rows.reshape(-1), used, xs, wg, wu, wd)
    outs, first = [], 0
    for h, route, nck_p, _ in parts:
        outs.append(_unsort(h, route, ys, first))
        first += nck_p.shape[0]
    return outs


def kernel(x_prompt, x_sample, cache_k, cache_v, norm1_g, w_in, q_gain, k_gain, v_gain, w_spatial, b_spatial,
           out_gain_a, out_gain_b, w_out, norm2_g, w_router1, b_router1, w_router2, b_router2, w_up, w_gate,
           w_down):
    depth = norm1_g.shape[0]
    assert depth == 1
    l = 0
    batch, seq, _ = x_prompt.shape
    nb, dec_seq, _ = x_sample.shape
    assert dec_seq == 1

    g1 = norm1_g[l].reshape(1, D_MODEL)
    win = w_in[l].astype(BF16)
    gains = jnp.stack([q_gain[l].reshape(WIDTH), k_gain[l].reshape(WIDTH), v_gain[l].reshape(WIDTH)])
    head_of_lane = jnp.arange(WIDTH) // HEAD_DIM
    bd = (head_of_lane[:, None] == head_of_lane[None, :]).astype(BF16)
    ws_tril = jnp.tril(w_spatial[l])
    wcat = jnp.concatenate([ws_tril[0::2], ws_tril[1::2]], axis=-1).astype(BF16)
    bs = b_spatial[l]
    bias = jnp.where(jnp.arange(LANES)[None, None, :] < HEAD_DIM, bs[0::2][:, :, None], bs[1::2][:, :, None])
    avec = jnp.repeat(w_spatial[l][:, 0, 0], HEAD_DIM).reshape(1, WIDTH)
    bvec = jnp.repeat(bs[:, 0], HEAD_DIM).reshape(1, WIDTH)
    ga = out_gain_a[l].reshape(1, WIDTH)
    gb = out_gain_b[l].reshape(1, WIDTH)
    wout = w_out[l].astype(BF16)
    g2 = norm2_g[l].reshape(1, D_MODEL)
    wr = jnp.zeros((D_MODEL, LANES), F32)
    wr = wr.at[:, :N_EXPERTS].set(jnp.transpose(w_router2[l], (1, 0, 2)).reshape(D_MODEL, N_EXPERTS))
    wr = wr.at[:, N_EXPERTS:N_EXPERTS + N_GROUPS].set(w_router1[l]).astype(BF16)
    br = jnp.zeros((1, LANES), F32)
    br = br.at[0, :N_EXPERTS].set(b_router2[l].reshape(N_EXPERTS))
    br = br.at[0, N_EXPERTS:N_EXPERTS + N_GROUPS].set(b_router1[l])
    lane_head = (jnp.arange(LANES)[:, None] == head_of_lane[None, :]).astype(BF16)
    expand = jnp.concatenate([lane_head, lane_head], axis=0)
    wg = w_gate[l].reshape(N_EXPERTS, D_MODEL, D_EXPERT)
    wu = w_up[l].reshape(N_EXPERTS, D_MODEL, D_EXPERT)
    wd = w_down[l].reshape(N_EXPERTS, D_EXPERT, D_MODEL)

    def tri(t):
        return (jnp.arange(t)[:, None] > jnp.arange(t)[None, :]).astype(BF16)

    xp = x_prompt.reshape(batch * seq, D_MODEL)
    qkv, oa, kt_last, vt_last = _proj_prompt(xp, g1, win, gains, bd, wcat, bias, seq=seq)
    obs, lses = [], []
    for (q, k, v), dil in zip(qkv, DILATIONS):
        o, lse = _band_attention(q, k, v, batch=batch, seq=seq, dil=dil)
        obs.append(o)
        lses.append(lse)
    tmix = min(MIX_TILE, batch * seq)
    upper = (jnp.arange(LANES)[:, None] < jnp.arange(LANES)[None, :]).astype(BF16)
    tmix_s = min(MIX_TILE, nb)
    h, sorted_rows, route, nck, ck0 = _mix(xp, oa, obs, lses, DILATIONS, ga, gb, wout, g2, wr, br, expand,
                                            tri(tmix), upper, spare_blocks=nb // tmix_s)
    kept = min(MAX_WINDOW, seq)
    to_cache = lambda t: jnp.transpose(t.reshape(1, batch, N_HEADS, HEAD_DIM, kept), (0, 1, 4, 2, 3))

    xs = x_sample.reshape(nb, D_MODEL)
    qs, ks, vs, vgs, oas = _proj_sample(xs, g1, win, gains, bd, avec, bvec)
    cols = lambda t: jnp.transpose(t.reshape(nb, N_HEADS, HEAD_DIM), (0, 2, 1))
    feature_major = lambda c: jnp.transpose(c, (0, 2, 3, 1)).reshape(nb, WIDTH, c.shape[1])
    ot = _sample_attention(cols(qs), cols(ks), cols(vs), feature_major(cache_k[l]), feature_major(cache_v[l]))
    obs_s = jnp.transpose(ot, (0, 2, 1)).reshape(nb, WIDTH)
    hs, sorted_rows, route_s, nck_s, ck0_s = _mix(xs, oas, [obs_s], [], (), ga, gb, wout, g2, wr, br, expand,
                                                  tri(tmix_s), upper, sorted_buf=sorted_rows,
                                                  first_block=batch * seq // tmix)

    y_prompt, y_sample = _grouped_moe([(h, route, nck, ck0), (hs, route_s, nck_s, ck0_s)], sorted_rows,
                                      wg, wu, wd)
    to5 = lambda t: t.reshape(1, nb, 1, N_HEADS, HEAD_DIM)
    return (y_prompt.reshape(batch, seq, D_MODEL), y_sample.reshape(nb, 1, D_MODEL),
            to_cache(kt_last), to_cache(vt_last), to5(ks), to5(vs), to5(vgs))
```

```python
import functools

import jax
import jax.numpy as jnp
from jax import lax
from jax.experimental import pallas as pl
from jax.experimental.pallas import tpu as pltpu

F32 = jnp.float32
BF16 = jnp.bfloat16
I32 = jnp.int32

D_MODEL = 1024
HEAD_DIM = 64
N_HEADS = 8
WIDTH = N_HEADS * HEAD_DIM
N_PAIRS = N_HEADS // 2
CHUNK = 128
N_WIN = 128
DILATIONS = (1, 4, 16)
MAX_WINDOW = 2048
N_GROUPS = 4
EXPERTS_PER_GROUP = 8
N_EXPERTS = N_GROUPS * EXPERTS_PER_GROUP
D_EXPERT = D_MODEL // 4
EPS = 1e-6
LOG2E = 1.4426950408889634
LN2 = 0.6931471805599453

LANES = 128
SUBLANES = 8
VMEM_LIMIT = 48 * 1024 * 1024

PROJ_TILE = 512
ATTN_TILE = 1024
MIX_TILE = 256
EXPERT_TILE = 512
SLOT_CHUNK = 2 * SUBLANES
SLOTS_PER_TILE = 1024
ZERO_CHUNKS = 64


def _dot(a, b):
    return jnp.dot(a, b, preferred_element_type=F32)


def _gelu(x):
    return 0.5 * x * (1.0 + jnp.tanh(0.7978845608028654 * (x + 0.044715 * (x * x * x))))


def _head_rms(t, gain, bd):
    ss = _dot((t * t).astype(BF16), bd)
    return t * lax.rsqrt(ss * (1.0 / HEAD_DIM) + EPS) * gain


def _proj_common(x_ref, g1_ref, win_ref, gains_ref, bd_ref, q_scale):
    x = x_ref[...]
    r = lax.rsqrt(jnp.mean(x * x, axis=-1, keepdims=True) + EPS)
    xn = (x * r * g1_ref[...]).astype(BF16)
    bd = bd_ref[...]
    q = _head_rms(_dot(xn, win_ref[:, 0:WIDTH]), gains_ref[0:1, :], bd) * q_scale
    k = _head_rms(_dot(xn, win_ref[:, WIDTH:2 * WIDTH]), gains_ref[1:2, :], bd)
    v = _dot(xn, win_ref[:, 2 * WIDTH:3 * WIDTH])
    u = _gelu(_dot(xn, win_ref[:, 3 * WIDTH:4 * WIDTH]))
    vg = _head_rms(_gelu(_dot(xn, win_ref[:, 4 * WIDTH:5 * WIDTH])), gains_ref[2:3, :], bd)
    return q, k, v, u, vg


def _proj_prompt_kernel(x_ref, g1_ref, win_ref, gains_ref, bd_ref, wcat_ref, bias_ref, *rest,
                        tiles_per_seq, first_kept_tile):
    stream_refs = rest[:3 * len(DILATIONS)]
    oa_ref, kl_ref, vl_ref, slab_a, slab_b = rest[3 * len(DILATIONS):]
    q, k, v, u, vg = _proj_common(x_ref, g1_ref, win_ref, gains_ref, bd_ref, HEAD_DIM ** -0.5 * LOG2E)
    tm = x_ref.shape[0]
    for a, z in enumerate((q, k, v)):
        cur, nxt = slab_a, slab_b
        for p in range(N_PAIRS):
            cur[p] = z[:, p * LANES:(p + 1) * LANES]
        for c, dil in enumerate(DILATIONS):
            out_ref = stream_refs[3 * c + a]
            if dil == 1:
                out_ref[...] = z.astype(BF16)
                continue
            prev = DILATIONS[c - 1]
            rows, rows_prev = tm // dil, tm // prev
            for r in range(dil):
                for p in range(N_PAIRS):
                    part = cur[p, pl.ds((r % prev) * rows_prev + r // prev, rows, stride=dil // prev), :]
                    out_ref[:, r * WIDTH + p * LANES:r * WIDTH + (p + 1) * LANES] = part.astype(BF16)
                    if c + 1 < len(DILATIONS):
                        nxt[p, r * rows:(r + 1) * rows, :] = part
            cur, nxt = nxt, cur

    @pl.when(pl.program_id(0) % tiles_per_seq >= first_kept_tile)
    def _():
        kl_ref[...] = k.T
        vl_ref[...] = v.T

    vgb = vg.astype(BF16)
    lane = lax.broadcasted_iota(I32, (CHUNK, LANES), 1)
    tm = x_ref.shape[0]
    for c in range(tm // CHUNK):
        rows = slice(c * CHUNK, (c + 1) * CHUNK)
        for p in range(N_PAIRS):
            cols = slice(p * LANES, (p + 1) * LANES)
            vp = vgb[rows, cols]
            zero = jnp.zeros_like(vp)
            rhs = jnp.concatenate([jnp.where(lane < HEAD_DIM, vp, zero),
                                   jnp.where(lane >= HEAD_DIM, vp, zero)], axis=0)
            mixed = _dot(wcat_ref[p], rhs) + bias_ref[p]
            oa_ref[rows, cols] = (u[rows, cols] * mixed).astype(BF16)


def _proj_sample_kernel(x_ref, g1_ref, win_ref, gains_ref, bd_ref, avec_ref, bvec_ref,
                        q_ref, k_ref, v_ref, vg_ref, oa_ref):
    q, k, v, u, vg = _proj_common(x_ref, g1_ref, win_ref, gains_ref, bd_ref, HEAD_DIM ** -0.5)
    q_ref[...] = q
    k_ref[...] = k
    v_ref[...] = v
    vg_ref[...] = vg
    oa_ref[...] = (u * (avec_ref[...] * vg + bvec_ref[...])).astype(BF16)


def _const_spec(shape):
    return pl.BlockSpec(shape, lambda *_: (0,) * len(shape))


def _proj_prompt(x2, g1, win, gains, bd, wcat, bias, *, seq):
    n = x2.shape[0]
    tm = PROJ_TILE
    tiles_per_seq = seq // tm
    kept = min(MAX_WINDOW, seq)
    first_kept_tile = tiles_per_seq - kept // tm

    def kept_map(i):
        return (i // tiles_per_seq, 0, jnp.maximum(i % tiles_per_seq - first_kept_tile, 0))

    row_spec = pl.BlockSpec((tm, WIDTH), lambda i: (i, 0))
    kept_spec = pl.BlockSpec((None, WIDTH, tm), kept_map)
    stream_specs = [pl.BlockSpec((tm // dil, dil * WIDTH), lambda i: (i, 0)) for dil in DILATIONS for _ in range(3)]
    stream_shapes = [jax.ShapeDtypeStruct((n // dil, dil * WIDTH), BF16) for dil in DILATIONS for _ in range(3)]
    outs = pl.pallas_call(
        functools.partial(_proj_prompt_kernel, tiles_per_seq=tiles_per_seq, first_kept_tile=first_kept_tile),
        grid=(n // tm,),
        in_specs=[pl.BlockSpec((tm, D_MODEL), lambda i: (i, 0)),
                  _const_spec((1, D_MODEL)), _const_spec((D_MODEL, 5 * WIDTH)), _const_spec((3, WIDTH)),
                  _const_spec((WIDTH, WIDTH)), _const_spec((N_PAIRS, CHUNK, 2 * CHUNK)),
                  _const_spec((N_PAIRS, CHUNK, LANES))],
        out_specs=stream_specs + [row_spec, kept_spec, kept_spec],
        out_shape=stream_shapes + [jax.ShapeDtypeStruct((n, WIDTH), BF16)] +
                  [jax.ShapeDtypeStruct((n // seq, WIDTH, kept), F32)] * 2,
        scratch_shapes=[pltpu.VMEM((N_PAIRS, tm, LANES), F32)] * 2,
        compiler_params=pltpu.CompilerParams(dimension_semantics=("arbitrary",), vmem_limit_bytes=VMEM_LIMIT),
        name="proj_prompt",
    )(x2, g1, win, gains, bd, wcat, bias)
    n_streams = 3 * len(DILATIONS)
    qkv = [outs[3 * c:3 * c + 3] for c in range(len(DILATIONS))]
    return (qkv, *outs[n_streams:])


def _proj_sample(x2, g1, win, gains, bd, avec, bvec):
    n = x2.shape[0]
    full = _const_spec((n, WIDTH))
    return pl.pallas_call(
        _proj_sample_kernel,
        grid=(1,),
        in_specs=[_const_spec((n, D_MODEL)), _const_spec((1, D_MODEL)), _const_spec((D_MODEL, 5 * WIDTH)),
                  _const_spec((3, WIDTH)), _const_spec((WIDTH, WIDTH)), _const_spec((1, WIDTH)),
                  _const_spec((1, WIDTH))],
        out_specs=[full] * 5,
        out_shape=[jax.ShapeDtypeStruct((n, WIDTH), F32)] * 4 + [jax.ShapeDtypeStruct((n, WIDTH), BF16)],
        compiler_params=pltpu.CompilerParams(dimension_semantics=("arbitrary",), vmem_limit_bytes=VMEM_LIMIT),
        name="proj_sample",
    )(x2, g1, win, gains, bd, avec, bvec)


def _band_attn_kernel(q_ref, k_ref, v_ref, o_ref, lse_ref, kprev, vprev, *, tq):
    j = pl.program_id(2)

    @pl.when(j == 0)
    def _():
        kprev[...] = jnp.zeros_like(kprev)
        vprev[...] = jnp.zeros_like(vprev)

    qi = lax.broadcasted_iota(I32, (2 * N_WIN, 2 * N_WIN), 0) % N_WIN
    ki = lax.broadcasted_iota(I32, (2 * N_WIN, 2 * N_WIN), 1)
    band = (ki >= qi) & (ki <= qi + N_WIN)
    lane = lax.broadcasted_iota(I32, (N_WIN, LANES), 1)
    low_half = lane < HEAD_DIM
    ones = jnp.ones((2 * N_WIN, LANES), BF16)

    for jb in range(tq // N_WIN):
        rows = slice(jb * N_WIN, (jb + 1) * N_WIN)
        window = slice((jb - 1) * N_WIN, (jb + 1) * N_WIN)
        valid = band & ((ki >= N_WIN) | (j > 0)) if jb == 0 else band
        lse_tile = jnp.zeros((N_WIN, LANES), F32)
        for p in range(N_PAIRS):
            cols = slice(p * LANES, (p + 1) * LANES)
            qp = q_ref[rows, cols]
            if jb == 0:
                kp = jnp.concatenate([kprev[:, cols], k_ref[rows, cols]], axis=0)
                vv = jnp.concatenate([vprev[:, cols], v_ref[rows, cols]], axis=0)
            else:
                kp = k_ref[window, cols]
                vv = v_ref[window, cols]
            zero = jnp.zeros_like(qp)
            q2 = jnp.concatenate([jnp.where(low_half, qp, zero), jnp.where(low_half, zero, qp)], axis=0)
            s = lax.dot_general(q2, kp, (((1,), (1,)), ((), ())), preferred_element_type=F32)
            s = jnp.where(valid, s, -jnp.inf)
            m = jnp.max(s, axis=-1, keepdims=True)
            e = jnp.exp2((s - m).astype(BF16))
            ov = _dot(e, jnp.concatenate([vv, ones], axis=1))
            den = ov[:, LANES:2 * LANES]
            o = ov[:, 0:LANES] / den
            lse = m * LN2 + jnp.log(den)
            lse_tile = jnp.where(lane == 2 * p, lse[0:N_WIN],
                                 jnp.where(lane == 2 * p + 1, lse[N_WIN:2 * N_WIN], lse_tile))
            o_ref[rows, cols] = jnp.where(low_half, o[0:N_WIN], o[N_WIN:2 * N_WIN]).astype(BF16)
        lse_ref[rows, :] = lse_tile

    kprev[...] = k_ref[tq - N_WIN:tq, :]
    vprev[...] = v_ref[tq - N_WIN:tq, :]


def _band_attention(q2, k2, v2, *, batch, seq, dil):
    length = seq // dil
    tq = min(ATTN_TILE, length)
    view = lambda a: a.reshape(batch, length, a.shape[-1])
    spec = pl.BlockSpec((None, tq, WIDTH), lambda b, r, j: (b, j, r))
    lse_spec = pl.BlockSpec((None, tq, LANES), lambda b, r, j: (b, j, r))
    o, lse = pl.pallas_call(
        functools.partial(_band_attn_kernel, tq=tq),
        grid=(batch, dil, length // tq),
        in_specs=[spec, spec, spec],
        out_specs=[spec, lse_spec],
        out_shape=[jax.ShapeDtypeStruct((batch, length, dil * WIDTH), BF16),
                   jax.ShapeDtypeStruct((batch, length, dil * LANES), F32)],
        scratch_shapes=[pltpu.VMEM((N_WIN, WIDTH), BF16), pltpu.VMEM((N_WIN, WIDTH), BF16)],
        compiler_params=pltpu.CompilerParams(dimension_semantics=("arbitrary", "arbitrary", "arbitrary"),
                                             vmem_limit_bytes=VMEM_LIMIT),
        name=f"band_attn_d{dil}",
    )(view(q2), view(k2), view(v2))
    return o.reshape(batch * length, dil * WIDTH), lse.reshape(batch * length, dil * LANES)


def _sample_attn_kernel(qt_ref, knt_ref, vnt_ref, kt_ref, vt_ref, ot_ref):
    qt = qt_ref[...]
    vnt = vnt_ref[...]
    s_new = jnp.sum(qt * knt_ref[...], axis=0, keepdims=True)
    wb = kt_ref.shape[1]
    heads = range(N_HEADS)
    head_rows = lambda h: slice(h * HEAD_DIM, (h + 1) * HEAD_DIM)
    s = jnp.concatenate([jnp.sum(kt_ref[head_rows(h), :] * qt[:, h:h + 1], axis=0, keepdims=True) for h in heads],
                        axis=0)
    s0 = jnp.concatenate([s_new[:, h:h + 1] for h in heads], axis=0)
    pos = lax.broadcasted_iota(I32, (N_HEADS, wb), 1)
    es, e0s, dens, lses = [], [], [], []
    for dil in DILATIONS:
        lo = wb - N_WIN * dil
        sc = s[:, lo:]
        valid = (pos[:, lo:] & (dil - 1)) == 0
        m = jnp.maximum(jnp.max(jnp.where(valid, sc, -jnp.inf), axis=-1, keepdims=True), s0)
        e = jnp.where(valid, jnp.exp(sc - m), 0.0)
        e0 = jnp.exp(s0 - m)
        den = jnp.sum(e, axis=-1, keepdims=True) + e0
        es.append(e)
        e0s.append(e0)
        dens.append(den)
        lses.append(m + jnp.log(den))
    mm = jnp.maximum(jnp.maximum(lses[0], lses[1]), lses[2])
    ws = [jnp.exp(l - mm) for l in lses]
    tot = ws[0] + ws[1] + ws[2]
    coef = [w / (tot * den) for w, den in zip(ws, dens)]
    w1, w4, w16 = [e * c for e, c in zip(es, coef)]
    w_new = coef[0] * e0s[0] + coef[1] * e0s[1] + coef[2] * e0s[2]
    n1, n4 = N_WIN * DILATIONS[0], N_WIN * DILATIONS[1]
    w_pos = jnp.concatenate([w16[:, :wb - n4], w16[:, wb - n4:wb - n1] + w4[:, :n4 - n1],
                             w16[:, wb - n1:] + w4[:, n4 - n1:] + w1], axis=1)
    head_lane = lax.broadcasted_iota(I32, (HEAD_DIM, N_HEADS), 1)
    ot = jnp.zeros((HEAD_DIM, N_HEADS), F32)
    for h in heads:
        col = (jnp.sum(vt_ref[head_rows(h), :] * w_pos[h:h + 1, :], axis=-1, keepdims=True) +
               w_new[h:h + 1, :] * vnt[:, h:h + 1])
        ot = jnp.where(head_lane == h, col, ot)
    ot_ref[...] = ot


def _sample_attention(qt, knt, vnt, kt, vt):
    nb, _, wb = kt.shape
    assert wb == MAX_WINDOW, "cache window must cover every dilated key"
    tok = pl.BlockSpec((None, HEAD_DIM, N_HEADS), lambda b: (b, 0, 0))
    cache = pl.BlockSpec((None, WIDTH, wb), lambda b: (b, 0, 0))
    return pl.pallas_call(
        _sample_attn_kernel,
        grid=(nb,),
        in_specs=[tok, tok, tok, cache, cache],
        out_specs=tok,
        out_shape=jax.ShapeDtypeStruct((nb, HEAD_DIM, N_HEADS), F32),
        compiler_params=pltpu.CompilerParams(dimension_semantics=("arbitrary",), vmem_limit_bytes=VMEM_LIMIT),
        name="sample_attn",
    )(qt, knt, vnt, kt, vt)


def _split_hi_lo(w):
    hi = w.astype(BF16)
    return jnp.concatenate([hi, (w - hi.astype(F32)).astype(BF16)], axis=-1)


def _token_order(ref, dil, width, slab, tm):
    if dil == 1:
        return ref[...].astype(F32)
    n_slabs = width // LANES
    for r in range(dil):
        for p in range(n_slabs):
            cols = slice(r * width + p * LANES, r * width + (p + 1) * LANES)
            slab[p, pl.ds(r, tm // dil, stride=dil), :] = ref[:, cols].astype(F32)
    return jnp.concatenate([slab[p] for p in range(n_slabs)], axis=1)


def _mix_kernel(*refs, dils, n_tiles, aliased):
    n_cfg = max(len(dils), 1)
    n_lse = len(dils)
    x_ref, oa_ref = refs[0], refs[1]
    o_refs = refs[2:2 + n_cfg]
    lse_refs = refs[2 + n_cfg:2 + n_cfg + n_lse]
    rest = refs[2 + n_cfg + n_lse:]
    ga_ref, gb_ref, wout_ref, g2_ref, wr_ref, br_ref, expand_ref, tri_ref, upper_ref = rest[:9]
    h_ref, xs_ref, route_ref, nchunk_ref, chunk0_ref, slab = rest[9 + int(aliased):]
    tm = x_ref.shape[0]

    if not dils:
        ob = o_refs[0][...].astype(F32)
    else:
        lses = [_token_order(r, dil, LANES, slab, tm) for r, dil in zip(lse_refs, dils)]
        mm = functools.reduce(jnp.maximum, lses)
        ws = [jnp.exp(l - mm) for l in lses]
        tot = functools.reduce(lambda a, b: a + b, ws)
        ob = jnp.zeros((tm, WIDTH), F32)
        for w, o_ref, dil in zip(ws, o_refs, dils):
            ob = ob + _dot(_split_hi_lo(w / tot), expand_ref[...]) * _token_order(o_ref, dil, WIDTH, slab, tm)

    oa = oa_ref[...].astype(F32)
    ya = oa * lax.rsqrt(jnp.mean(oa * oa, axis=-1, keepdims=True) + EPS) * ga_ref[...]
    yb = ob * lax.rsqrt(jnp.mean(ob * ob, axis=-1, keepdims=True) + EPS) * gb_ref[...]
    cat = jnp.concatenate([ya, yb], axis=-1).astype(BF16)
    h = x_ref[...] + _dot(cat, wout_ref[...])
    h_ref[...] = h
    hn = (h * lax.rsqrt(jnp.mean(h * h, axis=-1, keepdims=True) + EPS) * g2_ref[...]).astype(BF16)

    logits = _dot(hn, wr_ref[...]) + br_ref[...]
    lane = lax.broadcasted_iota(I32, (tm, LANES), 1)
    lane_f = lane.astype(F32)
    neg = -jnp.inf
    big = float(LANES)
    coarse = (lane >= N_EXPERTS) & (lane < N_EXPERTS + N_GROUPS)
    lg = jnp.where(coarse, logits, neg)
    mx = jnp.max(lg, axis=-1, keepdims=True)
    g_lane = jnp.min(jnp.where(lg == mx, lane_f, big), axis=-1, keepdims=True)
    p_star = 1.0 / jnp.sum(jnp.exp(lg - mx), axis=-1, keepdims=True)
    lo = (g_lane - float(N_EXPERTS)) * float(EXPERTS_PER_GROUP)
    lf = jnp.where((lane_f >= lo) & (lane_f < lo + float(EXPERTS_PER_GROUP)), logits, neg)
    v1 = jnp.max(lf, axis=-1, keepdims=True)
    i1 = jnp.min(jnp.where(lf == v1, lane_f, big), axis=-1, keepdims=True)
    lf2 = jnp.where(lane_f == i1, neg, lf)
    v2 = jnp.max(lf2, axis=-1, keepdims=True)
    i2 = jnp.min(jnp.where(lf2 == v2, lane_f, big), axis=-1, keepdims=True)
    e21 = jnp.exp(v2 - v1)
    w1 = p_star / (1.0 + e21)
    w2 = p_star * e21 / (1.0 + e21)

    sel1 = lane_f == i1
    sel2 = lane_f == i2
    onehot = jnp.where(sel1 | sel2, 1.0, 0.0)
    rank = _dot(tri_ref[...], onehot.astype(BF16))
    count = jnp.sum(onehot, axis=0, keepdims=True)
    chunks = jnp.floor((count + float(SLOT_CHUNK - 1)) * (1.0 / SLOT_CHUNK))
    chunk0 = _dot(jnp.broadcast_to(chunks, (SUBLANES, LANES)).astype(BF16), upper_ref[...])[0:1, :]
    slot = rank + chunk0 * float(SLOT_CHUNK)
    s1 = jnp.sum(jnp.where(sel1, slot, 0.0), axis=-1, keepdims=True)
    s2 = jnp.sum(jnp.where(sel2, slot, 0.0), axis=-1, keepdims=True)
    i = pl.program_id(0)
    tile = jnp.minimum(i, n_tiles - 1)
    nchunk_ref[pl.ds(tile, 1), :] = chunks.astype(I32)
    chunk0_ref[pl.ds(tile, 1), :] = chunk0.astype(I32)

    s1_row = jnp.broadcast_to(s1, (tm, LANES)).T[0:1, :]
    s2_row = jnp.broadcast_to(s2, (tm, LANES)).T[0:1, :]
    slot_id = lax.broadcasted_iota(I32, (SLOTS_PER_TILE, tm), 0).astype(F32)
    perm = jnp.where((slot_id == s1_row) | (slot_id == s2_row), 1.0, 0.0).astype(BF16)
    xs_ref[...] = jnp.where(i < n_tiles, _dot(perm, hn), 0.0).astype(BF16)

    col = lax.broadcasted_iota(I32, (tm, SUBLANES), 1)
    route = jnp.zeros((tm, SUBLANES), F32)
    for idx, val in enumerate((s1, s2, w1, w2)):
        route = jnp.where(col == idx, val, route)
    route_ref[...] = route


def _mix(x2, oa, obs, lses, dils, ga, gb, wout, g2, wr, br, expand, tri, upper, *, sorted_buf=None,
         first_block=0, spare_blocks=0):
    n = x2.shape[0]
    tm = min(MIX_TILE, n)
    n_tiles = n // tm
    last = n_tiles - 1
    row = lambda w: pl.BlockSpec((tm, w), lambda i: (jnp.minimum(i, last), 0))
    stream = lambda w, dil: pl.BlockSpec((tm // dil, dil * w), lambda i: (jnp.minimum(i, last), 0))
    in_specs = ([row(D_MODEL), row(WIDTH)] + [stream(WIDTH, dil) for dil in (dils or (1,))] +
                [stream(LANES, dil) for dil in dils] +
                [_const_spec((1, WIDTH)), _const_spec((1, WIDTH)), _const_spec((D_MODEL, D_MODEL)),
                 _const_spec((1, D_MODEL)), _const_spec((D_MODEL, LANES)), _const_spec((1, LANES)),
                 _const_spec((2 * LANES, WIDTH)), _const_spec((tm, tm)), _const_spec((LANES, LANES))])
    args = [x2, oa, *obs, *lses, ga, gb, wout, g2, wr, br, expand, tri, upper]
    aliases = {}
    if sorted_buf is None:
        sorted_shape = ((n_tiles + spare_blocks) * SLOTS_PER_TILE, D_MODEL)
    else:
        assert spare_blocks == 0
        sorted_shape = sorted_buf.shape
        aliases = {len(args): 1}
        in_specs.append(pl.BlockSpec(memory_space=pl.ANY))
        args.append(sorted_buf)
    return pl.pallas_call(
        functools.partial(_mix_kernel, dils=tuple(dils), n_tiles=n_tiles, aliased=sorted_buf is not None),
        grid=(n_tiles + spare_blocks,),
        in_specs=in_specs,
        out_specs=[row(D_MODEL), pl.BlockSpec((SLOTS_PER_TILE, D_MODEL), lambda i: (first_block + i, 0)),
                   row(SUBLANES), _const_spec((n_tiles, LANES)), _const_spec((n_tiles, LANES))],
        out_shape=[jax.ShapeDtypeStruct((n, D_MODEL), F32), jax.ShapeDtypeStruct(sorted_shape, BF16),
                   jax.ShapeDtypeStruct((n, SUBLANES), F32),
                   jax.ShapeDtypeStruct((n_tiles, LANES), I32), jax.ShapeDtypeStruct((n_tiles, LANES), I32)],
        input_output_aliases=aliases,
        scratch_shapes=[pltpu.VMEM((N_PAIRS, tm, LANES), F32)],
        compiler_params=pltpu.CompilerParams(dimension_semantics=("arbitrary",), vmem_limit_bytes=VMEM_LIMIT),
        name="mix",
    )(*args)


def _chunk(ref, row):
    return ref.at[pl.ds(pl.multiple_of(row, SLOT_CHUNK), SLOT_CHUNK), :]


def _grouped_expert_kernel(te_ref, nch_ref, rows_ref, used_ref, xs_hbm, wg_ref, wu_ref, wd_ref, ys_hbm,
                           xbuf, obuf, zbuf, wgu, wdn, gsem, ssem, zsem):
    g = pl.program_id(0)
    n_steps = pl.num_programs(0)
    cpt = EXPERT_TILE // SLOT_CHUNK
    blocks = used_ref.shape[0]

    def gather(step):
        half = step % 2

        @pl.when(nch_ref[step] > 0)
        def _():
            def one(c, carry):
                pltpu.make_async_copy(_chunk(xs_hbm, rows_ref[step * cpt + c]),
                                      _chunk(xbuf, (half * cpt + c) * SLOT_CHUNK), gsem.at[half]).start()
                return carry

            lax.fori_loop(0, cpt, one, 0, unroll=8)

    def wait_scatter(n):
        def one(_, carry):
            pltpu.make_async_copy(_chunk(obuf, 0), _chunk(ys_hbm, 0), ssem).wait()
            return carry

        lax.fori_loop(0, n, one, 0)

    def zero_fill(wait):
        used = used_ref[jnp.minimum(g, blocks - 1)]
        n_free = SLOTS_PER_TILE // SLOT_CHUNK - used
        bit = ZERO_CHUNKS
        while bit >= 1:
            row = g * SLOTS_PER_TILE + (used + (n_free & ~(2 * bit - 1))) * SLOT_CHUNK
            cp = pltpu.make_async_copy(
                zbuf.at[pl.ds(0, bit * SLOT_CHUNK), :],
                ys_hbm.at[pl.ds(pl.multiple_of(row, SLOT_CHUNK), bit * SLOT_CHUNK), :], zsem)
            pl.when((g < blocks) & ((n_free & bit) != 0))(cp.wait if wait else cp.start)
            bit //= 2

    @pl.when(g == 0)
    def _():
        zbuf[...] = jnp.zeros_like(zbuf)
        gather(0)

    pl.when(g + 1 < n_steps)(functools.partial(gather, g + 1))
    zero_fill(wait=False)

    @pl.when((g == 0) | (te_ref[g] != te_ref[jnp.maximum(g - 1, 0)]))
    def _():
        wgu[:, 0:D_EXPERT] = wg_ref[...].astype(BF16)
        wgu[:, D_EXPERT:2 * D_EXPERT] = wu_ref[...].astype(BF16)
        wdn[...] = wd_ref[...].astype(BF16)

    half = g % 2
    n_chunks = nch_ref[g]
    n_prev = jnp.where(g > 0, nch_ref[jnp.maximum(g - 1, 0)], 0)

    @pl.when(n_chunks > 0)
    def _():
        x_ref = xbuf.at[pl.ds(pl.multiple_of(half * EXPERT_TILE, EXPERT_TILE), EXPERT_TILE), :]
        pltpu.make_async_copy(xs_hbm.at[pl.ds(0, EXPERT_TILE), :], x_ref, gsem.at[half]).wait()
        ab = _dot(x_ref[...], wgu[...])
        a = ab[:, 0:D_EXPERT]
        hid = (a * jax.nn.sigmoid(a)) * ab[:, D_EXPERT:2 * D_EXPERT]
        out = _dot(hid.astype(BF16), wdn[...])
        wait_scatter(n_prev)
        obuf[...] = out.astype(BF16)

        def scatter(c, carry):
            pltpu.make_async_copy(_chunk(obuf, c * SLOT_CHUNK), _chunk(ys_hbm, rows_ref[g * cpt + c]), ssem).start()
            return carry

        lax.fori_loop(0, n_chunks, scatter, 0)

    pl.when(n_chunks == 0)(functools.partial(wait_scatter, n_prev))
    pl.when(g == n_steps - 1)(functools.partial(wait_scatter, n_chunks))
    zero_fill(wait=True)


def _grouped_experts(tile_expert, tile_chunks, rows, used, xs, wg, wu, wd):
    n_steps = tile_expert.shape[0]
    weights = lambda shape: pl.BlockSpec((None,) + shape, lambda g, te, *_: (te[g], 0, 0))
    any_spec = pl.BlockSpec(memory_space=pl.ANY)
    return pl.pallas_call(
        _grouped_expert_kernel,
        grid_spec=pltpu.PrefetchScalarGridSpec(
            num_scalar_prefetch=4, grid=(n_steps,),
            in_specs=[any_spec, weights((D_MODEL, D_EXPERT)), weights((D_MODEL, D_EXPERT)),
                      weights((D_EXPERT, D_MODEL))],
            out_specs=any_spec,
            scratch_shapes=[pltpu.VMEM((2 * EXPERT_TILE, D_MODEL), BF16), pltpu.VMEM((EXPERT_TILE, D_MODEL), BF16),
                            pltpu.VMEM((ZERO_CHUNKS * SLOT_CHUNK, D_MODEL), BF16),
                            pltpu.VMEM((D_MODEL, 2 * D_EXPERT), BF16), pltpu.VMEM((D_EXPERT, D_MODEL), BF16),
                            pltpu.SemaphoreType.DMA((2,)), pltpu.SemaphoreType.DMA, pltpu.SemaphoreType.DMA]),
        out_shape=jax.ShapeDtypeStruct(xs.shape, BF16),
        compiler_params=pltpu.CompilerParams(dimension_semantics=("arbitrary",), vmem_limit_bytes=VMEM_LIMIT),
        name="experts",
    )(tile_expert, tile_chunks, rows, used, xs, wg, wu, wd)


def _unsort_kernel(h_ref, route_ref, ys_ref, y_ref):
    tm = h_ref.shape[0]
    route = route_ref[...]
    slot_id = lax.broadcasted_iota(I32, (tm, SLOTS_PER_TILE), 1).astype(F32)
    sel = (jnp.where(slot_id == route[:, 0:1], route[:, 2:3], 0.0) +
           jnp.where(slot_id == route[:, 1:2], route[:, 3:4], 0.0))
    y_ref[...] = h_ref[...] + _dot(sel.astype(BF16), ys_ref[...])


def _unsort(h, route, ys, first_block):
    n = h.shape[0]
    tm = min(MIX_TILE, n)
    return pl.pallas_call(
        _unsort_kernel,
        grid=(n // tm,),
        in_specs=[pl.BlockSpec((tm, D_MODEL), lambda i: (i, 0)), pl.BlockSpec((tm, SUBLANES), lambda i: (i, 0)),
                  pl.BlockSpec((SLOTS_PER_TILE, D_MODEL), lambda i: (first_block + i, 0))],
        out_specs=pl.BlockSpec((tm, D_MODEL), lambda i: (i, 0)),
        out_shape=jax.ShapeDtypeStruct((n, D_MODEL), F32),
        compiler_params=pltpu.CompilerParams(dimension_semantics=("arbitrary",), vmem_limit_bytes=VMEM_LIMIT),
        name="combine",
    )(h, route, ys)


def _grouped_moe(parts, xs, wg, wu, wd):
    cpt = EXPERT_TILE // SLOT_CHUNK
    nck = jnp.concatenate([p[2] for p in parts])[:, :N_EXPERTS]
    ck0 = jnp.concatenate([p[3] for p in parts])[:, :N_EXPERTS]
    blocks = nck.shape[0]
    cum_incl = jnp.cumsum(nck, axis=0)
    cum_excl = cum_incl - nck
    per_expert = cum_incl[blocks - 1]
    tiles_e = (per_expert + cpt - 1) // cpt
    ends = jnp.cumsum(tiles_e)
    n_steps = blocks * (SLOTS_PER_TILE // SLOT_CHUNK) // cpt + N_EXPERTS
    step = jnp.arange(n_steps, dtype=I32)
    tile_expert = jnp.minimum(jnp.sum((step[:, None] >= ends[None, :]).astype(I32), axis=1), N_EXPERTS - 1)
    first_step = (ends - tiles_e)[tile_expert]
    mine = per_expert[tile_expert]
    tile_chunks = jnp.clip(mine - (step - first_step) * cpt, 0, cpt)
    tile_chunks = jnp.where(step < ends[N_EXPERTS - 1], tile_chunks, 0).astype(I32)
    p = (step - first_step)[:, None] * cpt + jnp.arange(cpt, dtype=I32)[None, :]
    p = jnp.clip(p, 0, jnp.maximum(mine - 1, 0)[:, None])[:, :, None]
    lo = cum_excl.T[tile_expert][:, None, :]
    hi = cum_incl.T[tile_expert][:, None, :]
    local = ck0.T[tile_expert][:, None, :] + p - lo
    block_row = (jnp.arange(blocks, dtype=I32) * SLOTS_PER_TILE)[None, None, :]
    rows = jnp.sum(jnp.where((lo <= p) & (p < hi), block_row + local * SLOT_CHUNK, 0), axis=-1).astype(I32)
    used = (ck0[:, N_EXPERTS - 1] + nck[:, N_EXPERTS - 1]).astype(I32)
    ys = _grouped_experts(tile_expert, tile_chunks, rows.reshape(-1), used, xs, wg, wu, wd)
    outs, first = [], 0
    for h, route, nck_p, _ in parts:
        outs.append(_unsort(h, route, ys, first))
        first += nck_p.shape[0]
    return outs


def kernel(x_prompt, x_sample, cache_k, cache_v, norm1_g, w_in, q_gain, k_gain, v_gain, w_spatial, b_spatial,
           out_gain_a, out_gain_b, w_out, norm2_g, w_router1, b_router1, w_router2, b_router2, w_up, w_gate,
           w_down):
    depth = norm1_g.shape[0]
    assert depth == 1
    l = 0
    batch, seq, _ = x_prompt.shape
    nb, dec_seq, _ = x_sample.shape
    assert dec_seq == 1

    g1 = norm1_g[l].reshape(1, D_MODEL)
    win = w_in[l].astype(BF16)
    gains = jnp.stack([q_gain[l].reshape(WIDTH), k_gain[l].reshape(WIDTH), v_gain[l].reshape(WIDTH)])
    head_of_lane = jnp.arange(WIDTH) // HEAD_DIM
    bd = (head_of_lane[:, None] == head_of_lane[None, :]).astype(BF16)
    ws_tril = jnp.tril(w_spatial[l])
    wcat = jnp.concatenate([ws_tril[0::2], ws_tril[1::2]], axis=-1).astype(BF16)
    bs = b_spatial[l]
    bias = jnp.where(jnp.arange(LANES)[None, None, :] < HEAD_DIM, bs[0::2][:, :, None], bs[1::2][:, :, None])
    avec = jnp.repeat(w_spatial[l][:, 0, 0], HEAD_DIM).reshape(1, WIDTH)
    bvec = jnp.repeat(bs[:, 0], HEAD_DIM).reshape(1, WIDTH)
    ga = out_gain_a[l].reshape(1, WIDTH)
    gb = out_gain_b[l].reshape(1, WIDTH)
    wout = w_out[l].astype(BF16)
    g2 = norm2_g[l].reshape(1, D_MODEL)
    wr = jnp.zeros((D_MODEL, LANES), F32)
    wr = wr.at[:, :N_EXPERTS].set(jnp.transpose(w_router2[l], (1, 0, 2)).reshape(D_MODEL, N_EXPERTS))
    wr = wr.at[:, N_EXPERTS:N_EXPERTS + N_GROUPS].set(w_router1[l]).astype(BF16)
    br = jnp.zeros((1, LANES), F32)
    br = br.at[0, :N_EXPERTS].set(b_router2[l].reshape(N_EXPERTS))
    br = br.at[0, N_EXPERTS:N_EXPERTS + N_GROUPS].set(b_router1[l])
    lane_head = (jnp.arange(LANES)[:, None] == head_of_lane[None, :]).astype(BF16)
    expand = jnp.concatenate([lane_head, lane_head], axis=0)
    wg = w_gate[l].reshape(N_EXPERTS, D_MODEL, D_EXPERT)
    wu = w_up[l].reshape(N_EXPERTS, D_MODEL, D_EXPERT)
    wd = w_down[l].reshape(N_EXPERTS, D_EXPERT, D_MODEL)

    def tri(t):
        return (jnp.arange(t)[:, None] > jnp.arange(t)[None, :]).astype(BF16)

    xp = x_prompt.reshape(batch * seq, D_MODEL)
    qkv, oa, kt_last, vt_last = _proj_prompt(xp, g1, win, gains, bd, wcat, bias, seq=seq)
    obs, lses = [], []
    for (q, k, v), dil in zip(qkv, DILATIONS):
        o, lse = _band_attention(q, k, v, batch=batch, seq=seq, dil=dil)
        obs.append(o)
        lses.append(lse)
    tmix = min(MIX_TILE, batch * seq)
    upper = (jnp.arange(LANES)[:, None] < jnp.arange(LANES)[None, :]).astype(BF16)
    tmix_s = min(MIX_TILE, nb)
    h, sorted_rows, route, nck, ck0 = _mix(xp, oa, obs, lses, DILATIONS, ga, gb, wout, g2, wr, br, expand,
                                            tri(tmix), upper, spare_blocks=nb // tmix_s)
    kept = min(MAX_WINDOW, seq)
    to_cache = lambda t: jnp.transpose(t.reshape(1, batch, N_HEADS, HEAD_DIM, kept), (0, 1, 4, 2, 3))

    xs = x_sample.reshape(nb, D_MODEL)
    qs, ks, vs, vgs, oas = _proj_sample(xs, g1, win, gains, bd, avec, bvec)
    cols = lambda t: jnp.transpose(t.reshape(nb, N_HEADS, HEAD_DIM), (0, 2, 1))
    feature_major = lambda c: jnp.transpose(c, (0, 2, 3, 1)).reshape(nb, WIDTH, c.shape[1])
    ot = _sample_attention(cols(qs), cols(ks), cols(vs), feature_major(cache_k[l]), feature_major(cache_v[l]))
    obs_s = jnp.transpose(ot, (0, 2, 1)).reshape(nb, WIDTH)
    hs, sorted_rows, route_s, nck_s, ck0_s = _mix(xs, oas, [obs_s], [], (), ga, gb, wout, g2, wr, br, expand,
                                                  tri(tmix_s), upper, sorted_buf=sorted_rows,
                                                  first_block=batch * seq // tmix)

    y_prompt, y_sample = _grouped_moe([(h, route, nck, ck0), (hs, route_s, nck_s, ck0_s)], sorted_rows,
                                      wg, wu, wd)
    to5 = lambda t: t.reshape(1, nb, 1, N_HEADS, HEAD_DIM)
    return (y_prompt.reshape(batch, seq, D_MODEL), y_sample.reshape(nb, 1, D_MODEL),
            to_cache(kt_last), to_cache(vt_last), to5(ks), to5(vs), to5(vgs))
```

```python
import functools

import jax
import jax.numpy as jnp
from jax import lax
from jax.experimental import pallas as pl
from jax.experimental.pallas import tpu as pltpu

F32 = jnp.float32
BF16 = jnp.bfloat16
I32 = jnp.int32

D_MODEL = 1024
HEAD_DIM = 64
N_HEADS = 8
WIDTH = N_HEADS * HEAD_DIM
N_PAIRS = N_HEADS // 2
CHUNK = 128
N_WIN = 128
DILATIONS = (1, 4, 16)
MAX_WINDOW = 2048
N_GROUPS = 4
EXPERTS_PER_GROUP = 8
N_EXPERTS = N_GROUPS * EXPERTS_PER_GROUP
D_EXPERT = D_MODEL // 4
EPS = 1e-6
LOG2E = 1.4426950408889634
LN2 = 0.6931471805599453

LANES = 128
SUBLANES = 8
VMEM_LIMIT = 48 * 1024 * 1024

PROJ_TILE = 512
ATTN_TILE = 1024
MIX_TILE = 256
MIX_SUBTILES = 2
EXPERT_TILE = 512
SLOT_CHUNK = 2 * SUBLANES
SLOTS_PER_TILE = 1024
ZERO_CHUNKS = 64


def _dot(a, b):
    return jnp.dot(a, b, preferred_element_type=F32)


def _gelu(x):
    return 0.5 * x * (1.0 + jnp.tanh(0.7978845608028654 * (x + 0.044715 * (x * x * x))))


def _head_rms(t, gain, bd):
    ss = _dot((t * t).astype(BF16), bd)
    return t * lax.rsqrt(ss * (1.0 / HEAD_DIM) + EPS) * gain


def _proj_common(x_ref, g1_ref, win_ref, gains_ref, bd_ref, q_scale):
    x = x_ref[...]
    r = lax.rsqrt(jnp.mean(x * x, axis=-1, keepdims=True) + EPS)
    xn = (x * r * g1_ref[...]).astype(BF16)
    bd = bd_ref[...]
    q = _head_rms(_dot(xn, win_ref[:, 0:WIDTH]), gains_ref[0:1, :], bd) * q_scale
    k = _head_rms(_dot(xn, win_ref[:, WIDTH:2 * WIDTH]), gains_ref[1:2, :], bd)
    v = _dot(xn, win_ref[:, 2 * WIDTH:3 * WIDTH])
    u = _gelu(_dot(xn, win_ref[:, 3 * WIDTH:4 * WIDTH]))
    vg = _head_rms(_gelu(_dot(xn, win_ref[:, 4 * WIDTH:5 * WIDTH])), gains_ref[2:3, :], bd)
    return q, k, v, u, vg


def _proj_prompt_kernel(x_ref, g1_ref, win_ref, gains_ref, bd_ref, wcat_ref, bias_ref, *rest,
                        tiles_per_seq, first_kept_tile):
    stream_refs = rest[:3 * len(DILATIONS)]
    oa_ref, kl_ref, vl_ref, slab_a, slab_b = rest[3 * len(DILATIONS):]
    q, k, v, u, vg = _proj_common(x_ref, g1_ref, win_ref, gains_ref, bd_ref, HEAD_DIM ** -0.5 * LOG2E)
    tm = x_ref.shape[0]
    for a, z in enumerate((q, k, v)):
        cur, nxt = slab_a, slab_b
        for p in range(N_PAIRS):
            cur[p] = z[:, p * LANES:(p + 1) * LANES]
        for c, dil in enumerate(DILATIONS):
            out_ref = stream_refs[3 * c + a]
            if dil == 1:
                out_ref[...] = z.astype(BF16)
                continue
            prev = DILATIONS[c - 1]
            rows, rows_prev = tm // dil, tm // prev
            for r in range(dil):
                for p in range(N_PAIRS):
                    part = cur[p, pl.ds((r % prev) * rows_prev + r // prev, rows, stride=dil // prev), :]
                    out_ref[:, r * WIDTH + p * LANES:r * WIDTH + (p + 1) * LANES] = part.astype(BF16)
                    if c + 1 < len(DILATIONS):
                        nxt[p, r * rows:(r + 1) * rows, :] = part
            cur, nxt = nxt, cur

    @pl.when(pl.program_id(0) % tiles_per_seq >= first_kept_tile)
    def _():
        kl_ref[...] = k.T
        vl_ref[...] = v.T

    vgb = vg.astype(BF16)
    lane = lax.broadcasted_iota(I32, (CHUNK, LANES), 1)
    tm = x_ref.shape[0]
    for c in range(tm // CHUNK):
        rows = slice(c * CHUNK, (c + 1) * CHUNK)
        for p in range(N_PAIRS):
            cols = slice(p * LANES, (p + 1) * LANES)
            vp = vgb[rows, cols]
            zero = jnp.zeros_like(vp)
            rhs = jnp.concatenate([jnp.where(lane < HEAD_DIM, vp, zero),
                                   jnp.where(lane >= HEAD_DIM, vp, zero)], axis=0)
            mixed = _dot(wcat_ref[p], rhs) + bias_ref[p]
            oa_ref[rows, cols] = (u[rows, cols] * mixed).astype(BF16)


def _proj_sample_kernel(x_ref, g1_ref, win_ref, gains_ref, bd_ref, avec_ref, bvec_ref,
                        q_ref, k_ref, v_ref, vg_ref, oa_ref):
    q, k, v, u, vg = _proj_common(x_ref, g1_ref, win_ref, gains_ref, bd_ref, HEAD_DIM ** -0.5)
    q_ref[...] = q
    k_ref[...] = k
    v_ref[...] = v
    vg_ref[...] = vg
    oa_ref[...] = (u * (avec_ref[...] * vg + bvec_ref[...])).astype(BF16)


def _const_spec(shape):
    return pl.BlockSpec(shape, lambda *_: (0,) * len(shape))


def _proj_prompt(x2, g1, win, gains, bd, wcat, bias, *, seq):
    n = x2.shape[0]
    tm = PROJ_TILE
    tiles_per_seq = seq // tm
    kept = min(MAX_WINDOW, seq)
    first_kept_tile = tiles_per_seq - kept // tm

    def kept_map(i):
        return (i // tiles_per_seq, 0, jnp.maximum(i % tiles_per_seq - first_kept_tile, 0))

    row_spec = pl.BlockSpec((tm, WIDTH), lambda i: (i, 0))
    kept_spec = pl.BlockSpec((None, WIDTH, tm), kept_map)
    stream_specs = [pl.BlockSpec((tm // dil, dil * WIDTH), lambda i: (i, 0)) for dil in DILATIONS for _ in range(3)]
    stream_shapes = [jax.ShapeDtypeStruct((n // dil, dil * WIDTH), BF16) for dil in DILATIONS for _ in range(3)]
    outs = pl.pallas_call(
        functools.partial(_proj_prompt_kernel, tiles_per_seq=tiles_per_seq, first_kept_tile=first_kept_tile),
        grid=(n // tm,),
        in_specs=[pl.BlockSpec((tm, D_MODEL), lambda i: (i, 0)),
                  _const_spec((1, D_MODEL)), _const_spec((D_MODEL, 5 * WIDTH)), _const_spec((3, WIDTH)),
                  _const_spec((WIDTH, WIDTH)), _const_spec((N_PAIRS, CHUNK, 2 * CHUNK)),
                  _const_spec((N_PAIRS, CHUNK, LANES))],
        out_specs=stream_specs + [row_spec, kept_spec, kept_spec],
        out_shape=stream_shapes + [jax.ShapeDtypeStruct((n, WIDTH), BF16)] +
                  [jax.ShapeDtypeStruct((n // seq, WIDTH, kept), F32)] * 2,
        scratch_shapes=[pltpu.VMEM((N_PAIRS, tm, LANES), F32)] * 2,
        compiler_params=pltpu.CompilerParams(dimension_semantics=("arbitrary",), vmem_limit_bytes=VMEM_LIMIT),
        name="proj_prompt",
    )(x2, g1, win, gains, bd, wcat, bias)
    n_streams = 3 * len(DILATIONS)
    qkv = [outs[3 * c:3 * c + 3] for c in range(len(DILATIONS))]
    return (qkv, *outs[n_streams:])


def _proj_sample(x2, g1, win, gains, bd, avec, bvec):
    n = x2.shape[0]
    full = _const_spec((n, WIDTH))
    return pl.pallas_call(
        _proj_sample_kernel,
        grid=(1,),
        in_specs=[_const_spec((n, D_MODEL)), _const_spec((1, D_MODEL)), _const_spec((D_MODEL, 5 * WIDTH)),
                  _const_spec((3, WIDTH)), _const_spec((WIDTH, WIDTH)), _const_spec((1, WIDTH)),
                  _const_spec((1, WIDTH))],
        out_specs=[full] * 5,
        out_shape=[jax.ShapeDtypeStruct((n, WIDTH), F32)] * 4 + [jax.ShapeDtypeStruct((n, WIDTH), BF16)],
        compiler_params=pltpu.CompilerParams(dimension_semantics=("arbitrary",), vmem_limit_bytes=VMEM_LIMIT),
        name="proj_sample",
    )(x2, g1, win, gains, bd, avec, bvec)


def _band_attn_kernel(q_ref, k_ref, v_ref, o_ref, lse_ref, kprev, vprev, *, tq):
    j = pl.program_id(2)

    @pl.when(j == 0)
    def _():
        kprev[...] = jnp.zeros_like(kprev)
        vprev[...] = jnp.zeros_like(vprev)

    qi = lax.broadcasted_iota(I32, (2 * N_WIN, 2 * N_WIN), 0) % N_WIN
    ki = lax.broadcasted_iota(I32, (2 * N_WIN, 2 * N_WIN), 1)
    band = (ki >= qi) & (ki <= qi + N_WIN)
    lane = lax.broadcasted_iota(I32, (N_WIN, LANES), 1)
    low_half = lane < HEAD_DIM
    ones = jnp.ones((2 * N_WIN, LANES), BF16)

    for jb in range(tq // N_WIN):
        rows = slice(jb * N_WIN, (jb + 1) * N_WIN)
        window = slice((jb - 1) * N_WIN, (jb + 1) * N_WIN)
        valid = band & ((ki >= N_WIN) | (j > 0)) if jb == 0 else band
        lse_tile = jnp.zeros((N_WIN, LANES), F32)
        for p in range(N_PAIRS):
            cols = slice(p * LANES, (p + 1) * LANES)
            qp = q_ref[rows, cols]
            if jb == 0:
                kp = jnp.concatenate([kprev[:, cols], k_ref[rows, cols]], axis=0)
                vv = jnp.concatenate([vprev[:, cols], v_ref[rows, cols]], axis=0)
            else:
                kp = k_ref[window, cols]
                vv = v_ref[window, cols]
            zero = jnp.zeros_like(qp)
            q2 = jnp.concatenate([jnp.where(low_half, qp, zero), jnp.where(low_half, zero, qp)], axis=0)
            s = lax.dot_general(q2, kp, (((1,), (1,)), ((), ())), preferred_element_type=F32)
            s = jnp.where(valid, s, -jnp.inf)
            m = jnp.max(s, axis=-1, keepdims=True)
            e = jnp.exp2((s - m).astype(BF16))
            ov = _dot(e, jnp.concatenate([vv, ones], axis=1))
            den = ov[:, LANES:2 * LANES]
            o = ov[:, 0:LANES] / den
            lse = m * LN2 + jnp.log(den)
            lse_tile = jnp.where(lane == 2 * p, lse[0:N_WIN],
                                 jnp.where(lane == 2 * p + 1, lse[N_WIN:2 * N_WIN], lse_tile))
            o_ref[rows, cols] = jnp.where(low_half, o[0:N_WIN], o[N_WIN:2 * N_WIN]).astype(BF16)
        lse_ref[rows, :] = lse_tile

    kprev[...] = k_ref[tq - N_WIN:tq, :]
    vprev[...] = v_ref[tq - N_WIN:tq, :]


def _band_attention(q2, k2, v2, *, batch, seq, dil):
    length = seq // dil
    tq = min(ATTN_TILE, length)
    view = lambda a: a.reshape(batch, length, a.shape[-1])
    spec = pl.BlockSpec((None, tq, WIDTH), lambda b, r, j: (b, j, r))
    lse_spec = pl.BlockSpec((None, tq, LANES), lambda b, r, j: (b, j, r))
    o, lse = pl.pallas_call(
        functools.partial(_band_attn_kernel, tq=tq),
        grid=(batch, dil, length // tq),
        in_specs=[spec, spec, spec],
        out_specs=[spec, lse_spec],
        out_shape=[jax.ShapeDtypeStruct((batch, length, dil * WIDTH), BF16),
                   jax.ShapeDtypeStruct((batch, length, dil * LANES), F32)],
        scratch_shapes=[pltpu.VMEM((N_WIN, WIDTH), BF16), pltpu.VMEM((N_WIN, WIDTH), BF16)],
        compiler_params=pltpu.CompilerParams(dimension_semantics=("arbitrary", "arbitrary", "arbitrary"),
                                             vmem_limit_bytes=VMEM_LIMIT),
        name=f"band_attn_d{dil}",
    )(view(q2), view(k2), view(v2))
    return o.reshape(batch * length, dil * WIDTH), lse.reshape(batch * length, dil * LANES)


def _sample_attn_kernel(qt_ref, knt_ref, vnt_ref, kt_ref, vt_ref, ot_ref):
    qt = qt_ref[...]
    vnt = vnt_ref[...]
    s_new = jnp.sum(qt * knt_ref[...], axis=0, keepdims=True)
    wb = kt_ref.shape[1]
    heads = range(N_HEADS)
    head_rows = lambda h: slice(h * HEAD_DIM, (h + 1) * HEAD_DIM)
    s = jnp.concatenate([jnp.sum(kt_ref[head_rows(h), :] * qt[:, h:h + 1], axis=0, keepdims=True) for h in heads],
                        axis=0)
    s0 = jnp.concatenate([s_new[:, h:h + 1] for h in heads], axis=0)
    pos = lax.broadcasted_iota(I32, (N_HEADS, wb), 1)
    es, e0s, dens, lses = [], [], [], []
    for dil in DILATIONS:
        lo = wb - N_WIN * dil
        sc = s[:, lo:]
        valid = (pos[:, lo:] & (dil - 1)) == 0
        m = jnp.maximum(jnp.max(jnp.where(valid, sc, -jnp.inf), axis=-1, keepdims=True), s0)
        e = jnp.where(valid, jnp.exp(sc - m), 0.0)
        e0 = jnp.exp(s0 - m)
        den = jnp.sum(e, axis=-1, keepdims=True) + e0
        es.append(e)
        e0s.append(e0)
        dens.append(den)
        lses.append(m + jnp.log(den))
    mm = jnp.maximum(jnp.maximum(lses[0], lses[1]), lses[2])
    ws = [jnp.exp(l - mm) for l in lses]
    tot = ws[0] + ws[1] + ws[2]
    coef = [w / (tot * den) for w, den in zip(ws, dens)]
    w1, w4, w16 = [e * c for e, c in zip(es, coef)]
    w_new = coef[0] * e0s[0] + coef[1] * e0s[1] + coef[2] * e0s[2]
    n1, n4 = N_WIN * DILATIONS[0], N_WIN * DILATIONS[1]
    w_pos = jnp.concatenate([w16[:, :wb - n4], w16[:, wb - n4:wb - n1] + w4[:, :n4 - n1],
                             w16[:, wb - n1:] + w4[:, n4 - n1:] + w1], axis=1)
    head_lane = lax.broadcasted_iota(I32, (HEAD_DIM, N_HEADS), 1)
    ot = jnp.zeros((HEAD_DIM, N_HEADS), F32)
    for h in heads:
        col = (jnp.sum(vt_ref[head_rows(h), :] * w_pos[h:h + 1, :], axis=-1, keepdims=True) +
               w_new[h:h + 1, :] * vnt[:, h:h + 1])
        ot = jnp.where(head_lane == h, col, ot)
    ot_ref[...] = ot


def _sample_attention(qt, knt, vnt, kt, vt):
    nb, _, wb = kt.shape
    assert wb == MAX_WINDOW, "cache window must cover every dilated key"
    tok = pl.BlockSpec((None, HEAD_DIM, N_HEADS), lambda b: (b, 0, 0))
    cache = pl.BlockSpec((None, WIDTH, wb), lambda b: (b, 0, 0))
    return pl.pallas_call(
        _sample_attn_kernel,
        grid=(nb,),
        in_specs=[tok, tok, tok, cache, cache],
        out_specs=tok,
        out_shape=jax.ShapeDtypeStruct((nb, HEAD_DIM, N_HEADS), F32),
        compiler_params=pltpu.CompilerParams(dimension_semantics=("arbitrary",), vmem_limit_bytes=VMEM_LIMIT),
        name="sample_attn",
    )(qt, knt, vnt, kt, vt)


def _split_hi_lo(w):
    hi = w.astype(BF16)
    return jnp.concatenate([hi, (w - hi.astype(F32)).astype(BF16)], axis=-1)


def _token_order(ref, dil, width, slab, tm):
    if dil == 1:
        return ref[...].astype(F32)
    n_slabs = width // LANES
    for r in range(dil):
        for p in range(n_slabs):
            cols = slice(r * width + p * LANES, r * width + (p + 1) * LANES)
            slab[p, pl.ds(r, tm // dil, stride=dil), :] = ref[:, cols].astype(F32)
    return jnp.concatenate([slab[p] for p in range(n_slabs)], axis=1)


def _mix_kernel(*refs, dils, n_steps, n_sub, aliased):
    n_cfg = max(len(dils), 1)
    n_lse = len(dils)
    x_ref, oa_ref = refs[0], refs[1]
    o_refs = refs[2:2 + n_cfg]
    lse_refs = refs[2 + n_cfg:2 + n_cfg + n_lse]
    rest = refs[2 + n_cfg + n_lse:]
    consts = rest[:9]
    h_ref, xs_ref, route_ref, nchunk_ref, chunk0_ref, *slabs = rest[9 + int(aliased):]
    tm = x_ref.shape[0] // n_sub
    i = pl.program_id(0)
    rows_of = lambda ref, sub, n: ref.at[pl.ds(sub * n, n), :]
    tiles = [
        _mix_tile(rows_of(x_ref, sub, tm), rows_of(oa_ref, sub, tm),
                  [rows_of(r, sub, tm // dil) for r, dil in zip(o_refs, dils or (1,))],
                  [rows_of(r, sub, tm // dil) for r, dil in zip(lse_refs, dils)], *consts,
                  rows_of(h_ref, sub, tm), rows_of(xs_ref, sub, SLOTS_PER_TILE), rows_of(route_ref, sub, tm),
                  nchunk_ref, chunk0_ref, slabs[sub], dils=dils,
                  table_row=jnp.minimum(i, n_steps - 1) * n_sub + sub, live=i < n_steps)
        for sub in range(n_sub)]
    for _ in zip(*tiles):
        pass


def _mix_tile(x_ref, oa_ref, o_refs, lse_refs, ga_ref, gb_ref, wout_ref, g2_ref, wr_ref, br_ref, expand_ref,
              tri_ref, upper_ref, h_ref, xs_ref, route_ref, nchunk_ref, chunk0_ref, slab, *, dils, table_row,
              live):
    tm = x_ref.shape[0]

    if not dils:
        ob = o_refs[0][...].astype(F32)
    else:
        lses = [_token_order(r, dil, LANES, slab, tm) for r, dil in zip(lse_refs, dils)]
        mm = functools.reduce(jnp.maximum, lses)
        ws = [jnp.exp(l - mm) for l in lses]
        tot = functools.reduce(lambda a, b: a + b, ws)
        ob = jnp.zeros((tm, WIDTH), F32)
        for w, o_ref, dil in zip(ws, o_refs, dils):
            ob = ob + _dot(_split_hi_lo(w / tot), expand_ref[...]) * _token_order(o_ref, dil, WIDTH, slab, tm)

    oa = oa_ref[...].astype(F32)
    ya = oa * lax.rsqrt(jnp.mean(oa * oa, axis=-1, keepdims=True) + EPS) * ga_ref[...]
    yb = ob * lax.rsqrt(jnp.mean(ob * ob, axis=-1, keepdims=True) + EPS) * gb_ref[...]
    cat = jnp.concatenate([ya, yb], axis=-1).astype(BF16)
    yield
    h = x_ref[...] + _dot(cat, wout_ref[...])
    h_ref[...] = h
    hn = (h * lax.rsqrt(jnp.mean(h * h, axis=-1, keepdims=True) + EPS) * g2_ref[...]).astype(BF16)
    yield

    logits = _dot(hn, wr_ref[...]) + br_ref[...]
    lane = lax.broadcasted_iota(I32, (tm, LANES), 1)
    lane_f = lane.astype(F32)
    neg = -jnp.inf
    big = float(LANES)
    coarse = (lane >= N_EXPERTS) & (lane < N_EXPERTS + N_GROUPS)
    lg = jnp.where(coarse, logits, neg)
    mx = jnp.max(lg, axis=-1, keepdims=True)
    g_lane = jnp.min(jnp.where(lg == mx, lane_f, big), axis=-1, keepdims=True)
    p_star = 1.0 / jnp.sum(jnp.exp(lg - mx), axis=-1, keepdims=True)
    lo = (g_lane - float(N_EXPERTS)) * float(EXPERTS_PER_GROUP)
    lf = jnp.where((lane_f >= lo) & (lane_f < lo + float(EXPERTS_PER_GROUP)), logits, neg)
    v1 = jnp.max(lf, axis=-1, keepdims=True)
    i1 = jnp.min(jnp.where(lf == v1, lane_f, big), axis=-1, keepdims=True)
    lf2 = jnp.where(lane_f == i1, neg, lf)
    v2 = jnp.max(lf2, axis=-1, keepdims=True)
    i2 = jnp.min(jnp.where(lf2 == v2, lane_f, big), axis=-1, keepdims=True)
    e21 = jnp.exp(v2 - v1)
    w1 = p_star / (1.0 + e21)
    w2 = p_star * e21 / (1.0 + e21)
    yield

    sel1 = lane_f == i1
    sel2 = lane_f == i2
    onehot = jnp.where(sel1 | sel2, 1.0, 0.0)
    rank = _dot(tri_ref[...], onehot.astype(BF16))
    count = jnp.sum(onehot, axis=0, keepdims=True)
    chunks = jnp.floor((count + float(SLOT_CHUNK - 1)) * (1.0 / SLOT_CHUNK))
    chunk0 = _dot(jnp.broadcast_to(chunks, (SUBLANES, LANES)).astype(BF16), upper_ref[...])[0:1, :]
    slot = rank + chunk0 * float(SLOT_CHUNK)
    s1 = jnp.sum(jnp.where(sel1, slot, 0.0), axis=-1, keepdims=True)
    s2 = jnp.sum(jnp.where(sel2, slot, 0.0), axis=-1, keepdims=True)
    nchunk_ref[pl.ds(table_row, 1), :] = chunks.astype(I32)
    chunk0_ref[pl.ds(table_row, 1), :] = chunk0.astype(I32)
    yield

    s1_row = jnp.broadcast_to(s1, (tm, LANES)).T[0:1, :]
    s2_row = jnp.broadcast_to(s2, (tm, LANES)).T[0:1, :]
    slot_id = lax.broadcasted_iota(I32, (SLOTS_PER_TILE, tm), 0).astype(F32)
    perm = jnp.where((slot_id == s1_row) | (slot_id == s2_row), 1.0, 0.0).astype(BF16)
    xs_ref[...] = jnp.where(live, _dot(perm, hn), 0.0).astype(BF16)

    col = lax.broadcasted_iota(I32, (tm, SUBLANES), 1)
    route = jnp.zeros((tm, SUBLANES), F32)
    for idx, val in enumerate((s1, s2, w1, w2)):
        route = jnp.where(col == idx, val, route)
    route_ref[...] = route
    yield


def _mix(x2, oa, obs, lses, dils, ga, gb, wout, g2, wr, br, expand, tri, upper, *, sorted_buf=None,
         first_block=0, spare_blocks=0):
    n = x2.shape[0]
    tm = min(MIX_TILE, n)
    n_sub = MIX_SUBTILES if n % (tm * MIX_SUBTILES) == 0 else 1
    step = tm * n_sub
    n_steps = n // step
    n_tiles = n_steps * n_sub
    spare_steps = pl.cdiv(spare_blocks, n_sub)
    last = n_steps - 1
    assert first_block % n_sub == 0
    row = lambda w: pl.BlockSpec((step, w), lambda i: (jnp.minimum(i, last), 0))
    stream = lambda w, dil: pl.BlockSpec((step // dil, dil * w), lambda i: (jnp.minimum(i, last), 0))
    in_specs = ([row(D_MODEL), row(WIDTH)] + [stream(WIDTH, dil) for dil in (dils or (1,))] +
                [stream(LANES, dil) for dil in dils] +
                [_const_spec((1, WIDTH)), _const_spec((1, WIDTH)), _const_spec((D_MODEL, D_MODEL)),
                 _const_spec((1, D_MODEL)), _const_spec((D_MODEL, LANES)), _const_spec((1, LANES)),
                 _const_spec((2 * LANES, WIDTH)), _const_spec((tm, tm)), _const_spec((LANES, LANES))])
    args = [x2, oa, *obs, *lses, ga, gb, wout, g2, wr, br, expand, tri, upper]
    aliases = {}
    if sorted_buf is None:
        sorted_shape = ((n_steps + spare_steps) * n_sub * SLOTS_PER_TILE, D_MODEL)
    else:
        assert spare_blocks == 0
        sorted_shape = sorted_buf.shape
        aliases = {len(args): 1}
        in_specs.append(pl.BlockSpec(memory_space=pl.ANY))
        args.append(sorted_buf)
    return pl.pallas_call(
        functools.partial(_mix_kernel, dils=tuple(dils), n_steps=n_steps, n_sub=n_sub,
                          aliased=sorted_buf is not None),
        grid=(n_steps + spare_steps,),
        in_specs=in_specs,
        out_specs=[row(D_MODEL),
                   pl.BlockSpec((n_sub * SLOTS_PER_TILE, D_MODEL), lambda i: (first_block // n_sub + i, 0)),
                   row(SUBLANES), _const_spec((n_tiles, LANES)), _const_spec((n_tiles, LANES))],
        out_shape=[jax.ShapeDtypeStruct((n, D_MODEL), F32), jax.ShapeDtypeStruct(sorted_shape, BF16),
                   jax.ShapeDtypeStruct((n, SUBLANES), F32),
                   jax.ShapeDtypeStruct((n_tiles, LANES), I32), jax.ShapeDtypeStruct((n_tiles, LANES), I32)],
        input_output_aliases=aliases,
        scratch_shapes=[pltpu.VMEM((N_PAIRS, tm, LANES), F32)] * n_sub,
        compiler_params=pltpu.CompilerParams(dimension_semantics=("arbitrary",), vmem_limit_bytes=VMEM_LIMIT),
        name="mix",
    )(*args)


def _chunk(ref, row):
    return ref.at[pl.ds(pl.multiple_of(row, SLOT_CHUNK), SLOT_CHUNK), :]


def _grouped_expert_kernel(te_ref, nch_ref, rows_ref, used_ref, xs_hbm, wg_ref, wu_ref, wd_ref, ys_hbm,
                           xbuf, obuf, zbuf, wgu, wdn, gsem, ssem, zsem):
    g = pl.program_id(0)
    n_steps = pl.num_programs(0)
    cpt = EXPERT_TILE // SLOT_CHUNK
    blocks = used_ref.shape[0]

    def gather(step):
        half = step % 2

        @pl.when(nch_ref[step] > 0)
        def _():
            def one(c, carry):
                pltpu.make_async_copy(_chunk(xs_hbm, rows_ref[step * cpt + c]),
                                      _chunk(xbuf, (half * cpt + c) * SLOT_CHUNK), gsem.at[half]).start()
                return carry

            lax.fori_loop(0, cpt, one, 0, unroll=8)

    def wait_scatter(n):
        def one(_, carry):
            pltpu.make_async_copy(_chunk(obuf, 0), _chunk(ys_hbm, 0), ssem).wait()
            return carry

        lax.fori_loop(0, n, one, 0)

    def zero_fill(wait):
        used = used_ref[jnp.minimum(g, blocks - 1)]
        n_free = SLOTS_PER_TILE // SLOT_CHUNK - used
        bit = ZERO_CHUNKS
        while bit >= 1:
            row = g * SLOTS_PER_TILE + (used + (n_free & ~(2 * bit - 1))) * SLOT_CHUNK
            cp = pltpu.make_async_copy(
                zbuf.at[pl.ds(0, bit * SLOT_CHUNK), :],
                ys_hbm.at[pl.ds(pl.multiple_of(row, SLOT_CHUNK), bit * SLOT_CHUNK), :], zsem)
            pl.when((g < blocks) & ((n_free & bit) != 0))(cp.wait if wait else cp.start)
            bit //= 2

    @pl.when(g == 0)
    def _():
        zbuf[...] = jnp.zeros_like(zbuf)
        gather(0)

    pl.when(g + 1 < n_steps)(functools.partial(gather, g + 1))
    zero_fill(wait=False)

    @pl.when((g == 0) | (te_ref[g] != te_ref[jnp.maximum(g - 1, 0)]))
    def _():
        wgu[:, 0:D_EXPERT] = wg_ref[...].astype(BF16)
        wgu[:, D_EXPERT:2 * D_EXPERT] = wu_ref[...].astype(BF16)
        wdn[...] = wd_ref[...].astype(BF16)

    half = g % 2
    n_chunks = nch_ref[g]
    n_prev = jnp.where(g > 0, nch_ref[jnp.maximum(g - 1, 0)], 0)

    @pl.when(n_chunks > 0)
    def _():
        x_ref = xbuf.at[pl.ds(pl.multiple_of(half * EXPERT_TILE, EXPERT_TILE), EXPERT_TILE), :]
        pltpu.make_async_copy(xs_hbm.at[pl.ds(0, EXPERT_TILE), :], x_ref, gsem.at[half]).wait()
        ab = _dot(x_ref[...], wgu[...])
        a = ab[:, 0:D_EXPERT]
        hid = (a * jax.nn.sigmoid(a)) * ab[:, D_EXPERT:2 * D_EXPERT]
        out = _dot(hid.astype(BF16), wdn[...])
        wait_scatter(n_prev)
        obuf[...] = out.astype(BF16)

        def scatter(c, carry):
            pltpu.make_async_copy(_chunk(obuf, c * SLOT_CHUNK), _chunk(ys_hbm, rows_ref[g * cpt + c]), ssem).start()
            return carry

        lax.fori_loop(0, n_chunks, scatter, 0)

    pl.when(n_chunks == 0)(functools.partial(wait_scatter, n_prev))
    pl.when(g == n_steps - 1)(functools.partial(wait_scatter, n_chunks))
    zero_fill(wait=True)


def _grouped_experts(tile_expert, tile_chunks, rows, used, xs, wg, wu, wd):
    n_steps = tile_expert.shape[0]
    weights = lambda shape: pl.BlockSpec((None,) + shape, lambda g, te, *_: (te[g], 0, 0))
    any_spec = pl.BlockSpec(memory_space=pl.ANY)
    return pl.pallas_call(
        _grouped_expert_kernel,
        grid_spec=pltpu.PrefetchScalarGridSpec(
            num_scalar_prefetch=4, grid=(n_steps,),
            in_specs=[any_spec, weights((D_MODEL, D_EXPERT)), weights((D_MODEL, D_EXPERT)),
                      weights((D_EXPERT, D_MODEL))],
            out_specs=any_spec,
            scratch_shapes=[pltpu.VMEM((2 * EXPERT_TILE, D_MODEL), BF16), pltpu.VMEM((EXPERT_TILE, D_MODEL), BF16),
                            pltpu.VMEM((ZERO_CHUNKS * SLOT_CHUNK, D_MODEL), BF16),
                            pltpu.VMEM((D_MODEL, 2 * D_EXPERT), BF16), pltpu.VMEM((D_EXPERT, D_MODEL), BF16),
                            pltpu.SemaphoreType.DMA((2,)), pltpu.SemaphoreType.DMA, pltpu.SemaphoreType.DMA]),
        out_shape=jax.ShapeDtypeStruct(xs.shape, BF16),
        compiler_params=pltpu.CompilerParams(dimension_semantics=("arbitrary",), vmem_limit_bytes=VMEM_LIMIT),
        name="experts",
    )(tile_expert, tile_chunks, rows, used, xs, wg, wu, wd)


def _unsort_kernel(h_ref, route_ref, ys_ref, y_ref):
    tm = h_ref.shape[0]
    route = route_ref[...]
    slot_id = lax.broadcasted_iota(I32, (tm, SLOTS_PER_TILE), 1).astype(F32)
    sel = (jnp.where(slot_id == route[:, 0:1], route[:, 2:3], 0.0) +
           jnp.where(slot_id == route[:, 1:2], route[:, 3:4], 0.0))
    y_ref[...] = h_ref[...] + _dot(sel.astype(BF16), ys_ref[...])


def _unsort(h, route, ys, first_block):
    n = h.shape[0]
    tm = min(MIX_TILE, n)
    return pl.pallas_call(
        _unsort_kernel,
        grid=(n // tm,),
        in_specs=[pl.BlockSpec((tm, D_MODEL), lambda i: (i, 0)), pl.BlockSpec((tm, SUBLANES), lambda i: (i, 0)),
                  pl.BlockSpec((SLOTS_PER_TILE, D_MODEL), lambda i: (first_block + i, 0))],
        out_specs=pl.BlockSpec((tm, D_MODEL), lambda i: (i, 0)),
        out_shape=jax.ShapeDtypeStruct((n, D_MODEL), F32),
        compiler_params=pltpu.CompilerParams(dimension_semantics=("arbitrary",), vmem_limit_bytes=VMEM_LIMIT),
        name="combine",
    )(h, route, ys)


def _grouped_moe(parts, xs, wg, wu, wd):
    cpt = EXPERT_TILE // SLOT_CHUNK
    blocks = xs.shape[0] // SLOTS_PER_TILE
    pad = lambda t: jnp.pad(t[:, :N_EXPERTS], ((0, blocks - t.shape[0]), (0, 0)))
    nck = pad(jnp.concatenate([p[2] for p in parts]))
    ck0 = pad(jnp.concatenate([p[3] for p in parts]))
    cum_incl = jnp.cumsum(nck, axis=0)
    cum_excl = cum_incl - nck
    per_expert = cum_incl[blocks - 1]
    tiles_e = (per_expert + cpt - 1) // cpt
    ends = jnp.cumsum(tiles_e)
    n_steps = blocks * (SLOTS_PER_TILE // SLOT_CHUNK) // cpt + N_EXPERTS
    step = jnp.arange(n_steps, dtype=I32)
    tile_expert = jnp.minimum(jnp.sum((step[:, None] >= ends[None, :]).astype(I32), axis=1), N_EXPERTS - 1)
    first_step = (ends - tiles_e)[tile_expert]
    mine = per_expert[tile_expert]
    tile_chunks = jnp.clip(mine - (step - first_step) * cpt, 0, cpt)
    tile_chunks = jnp.where(step < ends[N_EXPERTS - 1], tile_chunks, 0).astype(I32)
    p = (step - first_step)[:, None] * cpt + jnp.arange(cpt, dtype=I32)[None, :]
    p = jnp.clip(p, 0, jnp.maximum(mine - 1, 0)[:, None])[:, :, None]
    lo = cum_excl.T[tile_expert][:, None, :]
    hi = cum_incl.T[tile_expert][:, None, :]
    local = ck0.T[tile_expert][:, None, :] + p - lo
    block_row = (jnp.arange(blocks, dtype=I32) * SLOTS_PER_TILE)[None, None, :]
    rows = jnp.sum(jnp.where((lo <= p) & (p < hi), block_row + local * SLOT_CHUNK, 0), axis=-1).astype(I32)
    used = (ck0[:, N_EXPERTS - 1] + nck[:, N_EXPERTS - 1]).astype(I32)
    ys = _grouped_experts(tile_expert, tile_chunks, rows.reshape(-1), used, xs, wg, wu, wd)
    outs, first = [], 0
    for h, route, nck_p, _ in parts:
        outs.append(_unsort(h, route, ys, first))
        first += nck_p.shape[0]
    return outs


def kernel(x_prompt, x_sample, cache_k, cache_v, norm1_g, w_in, q_gain, k_gain, v_gain, w_spatial, b_spatial,
           out_gain_a, out_gain_b, w_out, norm2_g, w_router1, b_router1, w_router2, b_router2, w_up, w_gate,
           w_down):
    depth = norm1_g.shape[0]
    assert depth == 1
    l = 0
    batch, seq, _ = x_prompt.shape
    nb, dec_seq, _ = x_sample.shape
    assert dec_seq == 1

    g1 = norm1_g[l].reshape(1, D_MODEL)
    win = w_in[l].astype(BF16)
    gains = jnp.stack([q_gain[l].reshape(WIDTH), k_gain[l].reshape(WIDTH), v_gain[l].reshape(WIDTH)])
    head_of_lane = jnp.arange(WIDTH) // HEAD_DIM
    bd = (head_of_lane[:, None] == head_of_lane[None, :]).astype(BF16)
    ws_tril = jnp.tril(w_spatial[l])
    wcat = jnp.concatenate([ws_tril[0::2], ws_tril[1::2]], axis=-1).astype(BF16)
    bs = b_spatial[l]
    bias = jnp.where(jnp.arange(LANES)[None, None, :] < HEAD_DIM, bs[0::2][:, :, None], bs[1::2][:, :, None])
    avec = jnp.repeat(w_spatial[l][:, 0, 0], HEAD_DIM).reshape(1, WIDTH)
    bvec = jnp.repeat(bs[:, 0], HEAD_DIM).reshape(1, WIDTH)
    ga = out_gain_a[l].reshape(1, WIDTH)
    gb = out_gain_b[l].reshape(1, WIDTH)
    wout = w_out[l].astype(BF16)
    g2 = norm2_g[l].reshape(1, D_MODEL)
    wr = jnp.zeros((D_MODEL, LANES), F32)
    wr = wr.at[:, :N_EXPERTS].set(jnp.transpose(w_router2[l], (1, 0, 2)).reshape(D_MODEL, N_EXPERTS))
    wr = wr.at[:, N_EXPERTS:N_EXPERTS + N_GROUPS].set(w_router1[l]).astype(BF16)
    br = jnp.zeros((1, LANES), F32)
    br = br.at[0, :N_EXPERTS].set(b_router2[l].reshape(N_EXPERTS))
    br = br.at[0, N_EXPERTS:N_EXPERTS + N_GROUPS].set(b_router1[l])
    lane_head = (jnp.arange(LANES)[:, None] == head_of_lane[None, :]).astype(BF16)
    expand = jnp.concatenate([lane_head, lane_head], axis=0)
    wg = w_gate[l].reshape(N_EXPERTS, D_MODEL, D_EXPERT)
    wu = w_up[l].reshape(N_EXPERTS, D_MODEL, D_EXPERT)
    wd = w_down[l].reshape(N_EXPERTS, D_EXPERT, D_MODEL)

    def tri(t):
        return (jnp.arange(t)[:, None] > jnp.arange(t)[None, :]).astype(BF16)

    xp = x_prompt.reshape(batch * seq, D_MODEL)
    qkv, oa, kt_last, vt_last = _proj_prompt(xp, g1, win, gains, bd, wcat, bias, seq=seq)
    obs, lses = [], []
    for (q, k, v), dil in zip(qkv, DILATIONS):
        o, lse = _band_attention(q, k, v, batch=batch, seq=seq, dil=dil)
        obs.append(o)
        lses.append(lse)
    tmix = min(MIX_TILE, batch * seq)
    upper = (jnp.arange(LANES)[:, None] < jnp.arange(LANES)[None, :]).astype(BF16)
    tmix_s = min(MIX_TILE, nb)
    h, sorted_rows, route, nck, ck0 = _mix(xp, oa, obs, lses, DILATIONS, ga, gb, wout, g2, wr, br, expand,
                                            tri(tmix), upper, spare_blocks=nb // tmix_s)
    kept = min(MAX_WINDOW, seq)
    to_cache = lambda t: jnp.transpose(t.reshape(1, batch, N_HEADS, HEAD_DIM, kept), (0, 1, 4, 2, 3))

    xs = x_sample.reshape(nb, D_MODEL)
    qs, ks, vs, vgs, oas = _proj_sample(xs, g1, win, gains, bd, avec, bvec)
    cols = lambda t: jnp.transpose(t.reshape(nb, N_HEADS, HEAD_DIM), (0, 2, 1))
    feature_major = lambda c: jnp.transpose(c, (0, 2, 3, 1)).reshape(nb, WIDTH, c.shape[1])
    ot = _sample_attention(cols(qs), cols(ks), cols(vs), feature_major(cache_k[l]), feature_major(cache_v[l]))
    obs_s = jnp.transpose(ot, (0, 2, 1)).reshape(nb, WIDTH)
    hs, sorted_rows, route_s, nck_s, ck0_s = _mix(xs, oas, [obs_s], [], (), ga, gb, wout, g2, wr, br, expand,
                                                  tri(tmix_s), upper, sorted_buf=sorted_rows,
                                                  first_block=batch * seq // tmix)

    y_prompt, y_sample = _grouped_moe([(h, route, nck, ck0), (hs, route_s, nck_s, ck0_s)], sorted_rows,
                                      wg, wu, wd)
    to5 = lambda t: t.reshape(1, nb, 1, N_HEADS, HEAD_DIM)
    return (y_prompt.reshape(batch, seq, D_MODEL), y_sample.reshape(nb, 1, D_MODEL),
            to_cache(kt_last), to_cache(vt_last), to5(ks), to5(vs), to5(vgs))
```

```python
import functools

import jax
import jax.numpy as jnp
from jax import lax
from jax.experimental import pallas as pl
from jax.experimental.pallas import tpu as pltpu

F32 = jnp.float32
BF16 = jnp.bfloat16
I32 = jnp.int32

D_MODEL = 1024
HEAD_DIM = 64
N_HEADS = 8
WIDTH = N_HEADS * HEAD_DIM
N_PAIRS = N_HEADS // 2
CHUNK = 128
N_WIN = 128
DILATIONS = (1, 4, 16)
MAX_WINDOW = 2048
N_GROUPS = 4
EXPERTS_PER_GROUP = 8
N_EXPERTS = N_GROUPS * EXPERTS_PER_GROUP
D_EXPERT = D_MODEL // 4
EPS = 1e-6
LOG2E = 1.4426950408889634
LN2 = 0.6931471805599453

LANES = 128
SUBLANES = 8
VMEM_LIMIT = 48 * 1024 * 1024

PROJ_TILE = 512
ATTN_TILE = 1024
MIX_TILE = 256
MIX_SUBTILES = 2
EXPERT_TILE = 512
SLOT_CHUNK = 2 * SUBLANES
SLOTS_PER_TILE = 1024
ZERO_CHUNKS = 64


def _dot(a, b):
    return jnp.dot(a, b, preferred_element_type=F32)


def _gelu(x):
    return 0.5 * x * (1.0 + jnp.tanh(0.7978845608028654 * (x + 0.044715 * (x * x * x))))


def _head_rms(t, gain, bd):
    ss = _dot((t * t).astype(BF16), bd)
    return t * lax.rsqrt(ss * (1.0 / HEAD_DIM) + EPS) * gain


def _proj_common(x_ref, g1_ref, win_ref, gains_ref, bd_ref, q_scale):
    x = x_ref[...]
    r = lax.rsqrt(jnp.mean(x * x, axis=-1, keepdims=True) + EPS)
    xn = (x * r * g1_ref[...]).astype(BF16)
    bd = bd_ref[...]
    q = _head_rms(_dot(xn, win_ref[:, 0:WIDTH]), gains_ref[0:1, :], bd) * q_scale
    k = _head_rms(_dot(xn, win_ref[:, WIDTH:2 * WIDTH]), gains_ref[1:2, :], bd)
    v = _dot(xn, win_ref[:, 2 * WIDTH:3 * WIDTH])
    u = _gelu(_dot(xn, win_ref[:, 3 * WIDTH:4 * WIDTH]))
    vg = _head_rms(_gelu(_dot(xn, win_ref[:, 4 * WIDTH:5 * WIDTH])), gains_ref[2:3, :], bd)
    return q, k, v, u, vg


def _proj_prompt_kernel(x_ref, g1_ref, win_ref, gains_ref, bd_ref, wcat_ref, bias_ref, *rest,
                        tiles_per_seq, first_kept_tile):
    stream_refs = rest[:3 * len(DILATIONS)]
    oa_ref, kl_ref, vl_ref, slab_a, slab_b = rest[3 * len(DILATIONS):]
    q, k, v, u, vg = _proj_common(x_ref, g1_ref, win_ref, gains_ref, bd_ref, HEAD_DIM ** -0.5 * LOG2E)
    tm = x_ref.shape[0]
    for a, z in enumerate((q, k, v)):
        cur, nxt = slab_a, slab_b
        for p in range(N_PAIRS):
            cur[p] = z[:, p * LANES:(p + 1) * LANES]
        for c, dil in enumerate(DILATIONS):
            out_ref = stream_refs[3 * c + a]
            if dil == 1:
                out_ref[...] = z.astype(BF16)
                continue
            prev = DILATIONS[c - 1]
            rows, rows_prev = tm // dil, tm // prev
            for r in range(dil):
                for p in range(N_PAIRS):
                    part = cur[p, pl.ds((r % prev) * rows_prev + r // prev, rows, stride=dil // prev), :]
                    out_ref[:, r * WIDTH + p * LANES:r * WIDTH + (p + 1) * LANES] = part.astype(BF16)
                    if c + 1 < len(DILATIONS):
                        nxt[p, r * rows:(r + 1) * rows, :] = part
            cur, nxt = nxt, cur

    @pl.when(pl.program_id(0) % tiles_per_seq >= first_kept_tile)
    def _():
        kl_ref[...] = k.T
        vl_ref[...] = v.T

    vgb = vg.astype(BF16)
    lane = lax.broadcasted_iota(I32, (CHUNK, LANES), 1)
    tm = x_ref.shape[0]
    for c in range(tm // CHUNK):
        rows = slice(c * CHUNK, (c + 1) * CHUNK)
        for p in range(N_PAIRS):
            cols = slice(p * LANES, (p + 1) * LANES)
            vp = vgb[rows, cols]
            zero = jnp.zeros_like(vp)
            rhs = jnp.concatenate([jnp.where(lane < HEAD_DIM, vp, zero),
                                   jnp.where(lane >= HEAD_DIM, vp, zero)], axis=0)
            mixed = _dot(wcat_ref[p], rhs) + bias_ref[p]
            oa_ref[rows, cols] = (u[rows, cols] * mixed).astype(BF16)


def _proj_sample_kernel(x_ref, g1_ref, win_ref, gains_ref, bd_ref, avec_ref, bvec_ref,
                        q_ref, k_ref, v_ref, vg_ref, oa_ref):
    q, k, v, u, vg = _proj_common(x_ref, g1_ref, win_ref, gains_ref, bd_ref, HEAD_DIM ** -0.5)
    q_ref[...] = q
    k_ref[...] = k
    v_ref[...] = v
    vg_ref[...] = vg
    oa_ref[...] = (u * (avec_ref[...] * vg + bvec_ref[...])).astype(BF16)


def _const_spec(shape):
    return pl.BlockSpec(shape, lambda *_: (0,) * len(shape))


def _proj_prompt(x2, g1, win, gains, bd, wcat, bias, *, seq):
    n = x2.shape[0]
    tm = PROJ_TILE
    tiles_per_seq = seq // tm
    kept = min(MAX_WINDOW, seq)
    first_kept_tile = tiles_per_seq - kept // tm

    def kept_map(i):
        return (i // tiles_per_seq, 0, jnp.maximum(i % tiles_per_seq - first_kept_tile, 0))

    row_spec = pl.BlockSpec((tm, WIDTH), lambda i: (i, 0))
    kept_spec = pl.BlockSpec((None, WIDTH, tm), kept_map)
    stream_specs = [pl.BlockSpec((tm // dil, dil * WIDTH), lambda i: (i, 0)) for dil in DILATIONS for _ in range(3)]
    stream_shapes = [jax.ShapeDtypeStruct((n // dil, dil * WIDTH), BF16) for dil in DILATIONS for _ in range(3)]
    outs = pl.pallas_call(
        functools.partial(_proj_prompt_kernel, tiles_per_seq=tiles_per_seq, first_kept_tile=first_kept_tile),
        grid=(n // tm,),
        in_specs=[pl.BlockSpec((tm, D_MODEL), lambda i: (i, 0)),
                  _const_spec((1, D_MODEL)), _const_spec((D_MODEL, 5 * WIDTH)), _const_spec((3, WIDTH)),
                  _const_spec((WIDTH, WIDTH)), _const_spec((N_PAIRS, CHUNK, 2 * CHUNK)),
                  _const_spec((N_PAIRS, CHUNK, LANES))],
        out_specs=stream_specs + [row_spec, kept_spec, kept_spec],
        out_shape=stream_shapes + [jax.ShapeDtypeStruct((n, WIDTH), BF16)] +
                  [jax.ShapeDtypeStruct((n // seq, WIDTH, kept), F32)] * 2,
        scratch_shapes=[pltpu.VMEM((N_PAIRS, tm, LANES), F32)] * 2,
        compiler_params=pltpu.CompilerParams(dimension_semantics=("arbitrary",), vmem_limit_bytes=VMEM_LIMIT),
        name="proj_prompt",
    )(x2, g1, win, gains, bd, wcat, bias)
    n_streams = 3 * len(DILATIONS)
    qkv = [outs[3 * c:3 * c + 3] for c in range(len(DILATIONS))]
    return (qkv, *outs[n_streams:])


def _proj_sample(x2, g1, win, gains, bd, avec, bvec):
    n = x2.shape[0]
    full = _const_spec((n, WIDTH))
    return pl.pallas_call(
        _proj_sample_kernel,
        grid=(1,),
        in_specs=[_const_spec((n, D_MODEL)), _const_spec((1, D_MODEL)), _const_spec((D_MODEL, 5 * WIDTH)),
                  _const_spec((3, WIDTH)), _const_spec((WIDTH, WIDTH)), _const_spec((1, WIDTH)),
                  _const_spec((1, WIDTH))],
        out_specs=[full] * 5,
        out_shape=[jax.ShapeDtypeStruct((n, WIDTH), F32)] * 4 + [jax.ShapeDtypeStruct((n, WIDTH), BF16)],
        compiler_params=pltpu.CompilerParams(dimension_semantics=("arbitrary",), vmem_limit_bytes=VMEM_LIMIT),
        name="proj_sample",
    )(x2, g1, win, gains, bd, avec, bvec)


def _band_attn_kernel(q_ref, k_ref, v_ref, o_ref, lse_ref, kprev, vprev, *, tq):
    j = pl.program_id(2)

    @pl.when(j == 0)
    def _():
        kprev[...] = jnp.zeros_like(kprev)
        vprev[...] = jnp.zeros_like(vprev)

    qi = lax.broadcasted_iota(I32, (2 * N_WIN, 2 * N_WIN), 0) % N_WIN
    ki = lax.broadcasted_iota(I32, (2 * N_WIN, 2 * N_WIN), 1)
    band = (ki >= qi) & (ki <= qi + N_WIN)
    lane = lax.broadcasted_iota(I32, (N_WIN, LANES), 1)
    low_half = lane < HEAD_DIM
    ones = jnp.ones((2 * N_WIN, LANES), BF16)

    for jb in range(tq // N_WIN):
        rows = slice(jb * N_WIN, (jb + 1) * N_WIN)
        window = slice((jb - 1) * N_WIN, (jb + 1) * N_WIN)
        valid = band & ((ki >= N_WIN) | (j > 0)) if jb == 0 else band
        lse_tile = jnp.zeros((N_WIN, LANES), F32)
        for p in range(N_PAIRS):
            cols = slice(p * LANES, (p + 1) * LANES)
            qp = q_ref[rows, cols]
            if jb == 0:
                kp = jnp.concatenate([kprev[:, cols], k_ref[rows, cols]], axis=0)
                vv = jnp.concatenate([vprev[:, cols], v_ref[rows, cols]], axis=0)
            else:
                kp = k_ref[window, cols]
                vv = v_ref[window, cols]
            zero = jnp.zeros_like(qp)
            q2 = jnp.concatenate([jnp.where(low_half, qp, zero), jnp.where(low_half, zero, qp)], axis=0)
            s = lax.dot_general(q2, kp, (((1,), (1,)), ((), ())), preferred_element_type=F32)
            s = jnp.where(valid, s, -jnp.inf)
            m = jnp.max(s, axis=-1, keepdims=True)
            e = jnp.exp2((s - m).astype(BF16))
            ov = _dot(e, jnp.concatenate([vv, ones], axis=1))
            den = ov[:, LANES:2 * LANES]
            o = ov[:, 0:LANES] / den
            lse = m * LN2 + jnp.log(den)
            lse_tile = jnp.where(lane == 2 * p, lse[0:N_WIN],
                                 jnp.where(lane == 2 * p + 1, lse[N_WIN:2 * N_WIN], lse_tile))
            o_ref[rows, cols] = jnp.where(low_half, o[0:N_WIN], o[N_WIN:2 * N_WIN]).astype(BF16)
        lse_ref[rows, :] = lse_tile

    kprev[...] = k_ref[tq - N_WIN:tq, :]
    vprev[...] = v_ref[tq - N_WIN:tq, :]


def _band_attention(q2, k2, v2, *, batch, seq, dil):
    length = seq // dil
    tq = min(ATTN_TILE, length)
    view = lambda a: a.reshape(batch, length, a.shape[-1])
    spec = pl.BlockSpec((None, tq, WIDTH), lambda b, r, j: (b, j, r))
    lse_spec = pl.BlockSpec((None, tq, LANES), lambda b, r, j: (b, j, r))
    o, lse = pl.pallas_call(
        functools.partial(_band_attn_kernel, tq=tq),
        grid=(batch, dil, length // tq),
        in_specs=[spec, spec, spec],
        out_specs=[spec, lse_spec],
        out_shape=[jax.ShapeDtypeStruct((batch, length, dil * WIDTH), BF16),
                   jax.ShapeDtypeStruct((batch, length, dil * LANES), F32)],
        scratch_shapes=[pltpu.VMEM((N_WIN, WIDTH), BF16), pltpu.VMEM((N_WIN, WIDTH), BF16)],
        compiler_params=pltpu.CompilerParams(dimension_semantics=("arbitrary", "arbitrary", "arbitrary"),
                                             vmem_limit_bytes=VMEM_LIMIT),
        name=f"band_attn_d{dil}",
    )(view(q2), view(k2), view(v2))
    return o.reshape(batch * length, dil * WIDTH), lse.reshape(batch * length, dil * LANES)


def _sample_attn_kernel(qt_ref, knt_ref, vnt_ref, kt_ref, vt_ref, ot_ref):
    qt = qt_ref[...]
    vnt = vnt_ref[...]
    s_new = jnp.sum(qt * knt_ref[...], axis=0, keepdims=True)
    wb = kt_ref.shape[1]
    heads = range(N_HEADS)
    head_rows = lambda h: slice(h * HEAD_DIM, (h + 1) * HEAD_DIM)
    s = jnp.concatenate([jnp.sum(kt_ref[head_rows(h), :] * qt[:, h:h + 1], axis=0, keepdims=True) for h in heads],
                        axis=0)
    s0 = jnp.concatenate([s_new[:, h:h + 1] for h in heads], axis=0)
    pos = lax.broadcasted_iota(I32, (N_HEADS, wb), 1)
    es, e0s, dens, lses = [], [], [], []
    for dil in DILATIONS:
        lo = wb - N_WIN * dil
        sc = s[:, lo:]
        valid = (pos[:, lo:] & (dil - 1)) == 0
        m = jnp.maximum(jnp.max(jnp.where(valid, sc, -jnp.inf), axis=-1, keepdims=True), s0)
        e = jnp.where(valid, jnp.exp(sc - m), 0.0)
        e0 = jnp.exp(s0 - m)
        den = jnp.sum(e, axis=-1, keepdims=True) + e0
        es.append(e)
        e0s.append(e0)
        dens.append(den)
        lses.append(m + jnp.log(den))
    mm = jnp.maximum(jnp.maximum(lses[0], lses[1]), lses[2])
    ws = [jnp.exp(l - mm) for l in lses]
    tot = ws[0] + ws[1] + ws[2]
    coef = [w / (tot * den) for w, den in zip(ws, dens)]
    w1, w4, w16 = [e * c for e, c in zip(es, coef)]
    w_new = coef[0] * e0s[0] + coef[1] * e0s[1] + coef[2] * e0s[2]
    n1, n4 = N_WIN * DILATIONS[0], N_WIN * DILATIONS[1]
    w_pos = jnp.concatenate([w16[:, :wb - n4], w16[:, wb - n4:wb - n1] + w4[:, :n4 - n1],
                             w16[:, wb - n1:] + w4[:, n4 - n1:] + w1], axis=1)
    head_lane = lax.broadcasted_iota(I32, (HEAD_DIM, N_HEADS), 1)
    ot = jnp.zeros((HEAD_DIM, N_HEADS), F32)
    for h in heads:
        col = (jnp.sum(vt_ref[head_rows(h), :] * w_pos[h:h + 1, :], axis=-1, keepdims=True) +
               w_new[h:h + 1, :] * vnt[:, h:h + 1])
        ot = jnp.where(head_lane == h, col, ot)
    ot_ref[...] = ot


def _sample_attention(qt, knt, vnt, kt, vt):
    nb, _, wb = kt.shape
    assert wb == MAX_WINDOW, "cache window must cover every dilated key"
    tok = pl.BlockSpec((None, HEAD_DIM, N_HEADS), lambda b: (b, 0, 0))
    cache = pl.BlockSpec((None, WIDTH, wb), lambda b: (b, 0, 0))
    return pl.pallas_call(
        _sample_attn_kernel,
        grid=(nb,),
        in_specs=[tok, tok, tok, cache, cache],
        out_specs=tok,
        out_shape=jax.ShapeDtypeStruct((nb, HEAD_DIM, N_HEADS), F32),
        compiler_params=pltpu.CompilerParams(dimension_semantics=("arbitrary",), vmem_limit_bytes=VMEM_LIMIT),
        name="sample_attn",
    )(qt, knt, vnt, kt, vt)


def _split_hi_lo(w):
    hi = w.astype(BF16)
    return jnp.concatenate([hi, (w - hi.astype(F32)).astype(BF16)], axis=-1)


def _token_order(ref, dil, width, slab, tm):
    if dil == 1:
        return ref[...].astype(F32)
    n_slabs = width // LANES
    for r in range(dil):
        for p in range(n_slabs):
            cols = slice(r * width + p * LANES, r * width + (p + 1) * LANES)
            slab[p, pl.ds(r, tm // dil, stride=dil), :] = ref[:, cols].astype(F32)
    return jnp.concatenate([slab[p] for p in range(n_slabs)], axis=1)


def _mix_kernel(*refs, dils, n_steps, n_sub, aliased):
    n_cfg = max(len(dils), 1)
    n_lse = len(dils)
    x_ref, oa_ref = refs[0], refs[1]
    o_refs = refs[2:2 + n_cfg]
    lse_refs = refs[2 + n_cfg:2 + n_cfg + n_lse]
    rest = refs[2 + n_cfg + n_lse:]
    consts = rest[:9]
    h_ref, xs_ref, route_ref, nchunk_ref, chunk0_ref, *slabs = rest[9 + int(aliased):]
    tm = x_ref.shape[0] // n_sub
    i = pl.program_id(0)
    rows_of = lambda ref, sub, n: ref.at[pl.ds(sub * n, n), :]
    tiles = [
        _mix_tile(rows_of(x_ref, sub, tm), rows_of(oa_ref, sub, tm),
                  [rows_of(r, sub, tm // dil) for r, dil in zip(o_refs, dils or (1,))],
                  [rows_of(r, sub, tm // dil) for r, dil in zip(lse_refs, dils)], *consts,
                  rows_of(h_ref, sub, tm), rows_of(xs_ref, sub, SLOTS_PER_TILE), rows_of(route_ref, sub, tm),
                  nchunk_ref, chunk0_ref, slabs[sub], dils=dils,
                  table_row=jnp.minimum(i, n_steps - 1) * n_sub + sub, live=i < n_steps)
        for sub in range(n_sub)]
    for _ in zip(*tiles):
        pass


def _mix_tile(x_ref, oa_ref, o_refs, lse_refs, ga_ref, gb_ref, wout_ref, g2_ref, wr_ref, br_ref, expand_ref,
              tri_ref, upper_ref, h_ref, xs_ref, route_ref, nchunk_ref, chunk0_ref, slab, *, dils, table_row,
              live):
    tm = x_ref.shape[0]

    if not dils:
        ob = o_refs[0][...].astype(F32)
    else:
        lses = [_token_order(r, dil, LANES, slab, tm) for r, dil in zip(lse_refs, dils)]
        mm = functools.reduce(jnp.maximum, lses)
        ws = [jnp.exp(l - mm) for l in lses]
        tot = functools.reduce(lambda a, b: a + b, ws)
        ob = jnp.zeros((tm, WIDTH), F32)
        for w, o_ref, dil in zip(ws, o_refs, dils):
            ob = ob + _dot(_split_hi_lo(w / tot), expand_ref[...]) * _token_order(o_ref, dil, WIDTH, slab, tm)

    oa = oa_ref[...].astype(F32)
    ya = oa * lax.rsqrt(jnp.mean(oa * oa, axis=-1, keepdims=True) + EPS) * ga_ref[...]
    yb = ob * lax.rsqrt(jnp.mean(ob * ob, axis=-1, keepdims=True) + EPS) * gb_ref[...]
    cat = jnp.concatenate([ya, yb], axis=-1).astype(BF16)
    yield
    h = x_ref[...] + _dot(cat, wout_ref[...])
    h_ref[...] = h
    hn = (h * lax.rsqrt(jnp.mean(h * h, axis=-1, keepdims=True) + EPS) * g2_ref[...]).astype(BF16)
    yield

    logits = _dot(hn, wr_ref[...]) + br_ref[...]
    lane = lax.broadcasted_iota(I32, (tm, LANES), 1)
    lane_f = lane.astype(F32)
    neg = -jnp.inf
    big = float(LANES)
    coarse = (lane >= N_EXPERTS) & (lane < N_EXPERTS + N_GROUPS)
    lg = jnp.where(coarse, logits, neg)
    mx = jnp.max(lg, axis=-1, keepdims=True)
    g_lane = jnp.min(jnp.where(lg == mx, lane_f, big), axis=-1, keepdims=True)
    p_star = 1.0 / jnp.sum(jnp.exp(lg - mx), axis=-1, keepdims=True)
    lo = (g_lane - float(N_EXPERTS)) * float(EXPERTS_PER_GROUP)
    lf = jnp.where((lane_f >= lo) & (lane_f < lo + float(EXPERTS_PER_GROUP)), logits, neg)
    v1 = jnp.max(lf, axis=-1, keepdims=True)
    i1 = jnp.min(jnp.where(lf == v1, lane_f, big), axis=-1, keepdims=True)
    lf2 = jnp.where(lane_f == i1, neg, lf)
    v2 = jnp.max(lf2, axis=-1, keepdims=True)
    i2 = jnp.min(jnp.where(lf2 == v2, lane_f, big), axis=-1, keepdims=True)
    e21 = jnp.exp(v2 - v1)
    w1 = p_star / (1.0 + e21)
    w2 = p_star * e21 / (1.0 + e21)
    yield

    sel1 = lane_f == i1
    sel2 = lane_f == i2
    onehot = jnp.where(sel1 | sel2, 1.0, 0.0)
    rank = _dot(tri_ref[...], onehot.astype(BF16))
    count = jnp.sum(onehot, axis=0, keepdims=True)
    chunks = jnp.floor((count + float(SLOT_CHUNK - 1)) * (1.0 / SLOT_CHUNK))
    chunk0 = _dot(jnp.broadcast_to(chunks, (SUBLANES, LANES)).astype(BF16), upper_ref[...])[0:1, :]
    slot = rank + chunk0 * float(SLOT_CHUNK)
    s1 = jnp.sum(jnp.where(sel1, slot, 0.0), axis=-1, keepdims=True)
    s2 = jnp.sum(jnp.where(sel2, slot, 0.0), axis=-1, keepdims=True)
    nchunk_ref[pl.ds(table_row, 1), :] = chunks.astype(I32)
    chunk0_ref[pl.ds(table_row, 1), :] = chunk0.astype(I32)
    yield

    s1_row = jnp.broadcast_to(s1, (tm, LANES)).T[0:1, :]
    s2_row = jnp.broadcast_to(s2, (tm, LANES)).T[0:1, :]
    slot_id = lax.broadcasted_iota(I32, (SLOTS_PER_TILE, tm), 0).astype(F32)
    perm = jnp.where((slot_id == s1_row) | (slot_id == s2_row), 1.0, 0.0).astype(BF16)
    xs_ref[...] = jnp.where(live, _dot(perm, hn), 0.0).astype(BF16)

    col = lax.broadcasted_iota(I32, (tm, SUBLANES), 1)
    route = jnp.zeros((tm, SUBLANES), F32)
    for idx, val in enumerate((s1, s2, w1, w2)):
        route = jnp.where(col == idx, val, route)
    route_ref[...] = route
    yield


def _mix(x2, oa, obs, lses, dils, ga, gb, wout, g2, wr, br, expand, tri, upper, *, sorted_buf=None,
         first_block=0, spare_blocks=0):
    n = x2.shape[0]
    tm = min(MIX_TILE, n)
    n_sub = MIX_SUBTILES if n % (tm * MIX_SUBTILES) == 0 else 1
    step = tm * n_sub
    n_steps = n // step
    n_tiles = n_steps * n_sub
    spare_steps = pl.cdiv(spare_blocks, n_sub)
    last = n_steps - 1
    assert first_block % n_sub == 0
    row = lambda w: pl.BlockSpec((step, w), lambda i: (jnp.minimum(i, last), 0))
    stream = lambda w, dil: pl.BlockSpec((step // dil, dil * w), lambda i: (jnp.minimum(i, last), 0))
    in_specs = ([row(D_MODEL), row(WIDTH)] + [stream(WIDTH, dil) for dil in (dils or (1,))] +
                [stream(LANES, dil) for dil in dils] +
                [_const_spec((1, WIDTH)), _const_spec((1, WIDTH)), _const_spec((D_MODEL, D_MODEL)),
                 _const_spec((1, D_MODEL)), _const_spec((D_MODEL, LANES)), _const_spec((1, LANES)),
                 _const_spec((2 * LANES, WIDTH)), _const_spec((tm, tm)), _const_spec((LANES, LANES))])
    args = [x2, oa, *obs, *lses, ga, gb, wout, g2, wr, br, expand, tri, upper]
    aliases = {}
    if sorted_buf is None:
        sorted_shape = ((n_steps + spare_steps) * n_sub * SLOTS_PER_TILE, D_MODEL)
    else:
        assert spare_blocks == 0
        sorted_shape = sorted_buf.shape
        aliases = {len(args): 1}
        in_specs.append(pl.BlockSpec(memory_space=pl.ANY))
        args.append(sorted_buf)
    return pl.pallas_call(
        functools.partial(_mix_kernel, dils=tuple(dils), n_steps=n_steps, n_sub=n_sub,
                          aliased=sorted_buf is not None),
        grid=(n_steps + spare_steps,),
        in_specs=in_specs,
        out_specs=[row(D_MODEL),
                   pl.BlockSpec((n_sub * SLOTS_PER_TILE, D_MODEL), lambda i: (first_block // n_sub + i, 0)),
                   row(SUBLANES), _const_spec((n_tiles, LANES)), _const_spec((n_tiles, LANES))],
        out_shape=[jax.ShapeDtypeStruct((n, D_MODEL), F32), jax.ShapeDtypeStruct(sorted_shape, BF16),
                   jax.ShapeDtypeStruct((n, SUBLANES), F32),
                   jax.ShapeDtypeStruct((n_tiles, LANES), I32), jax.ShapeDtypeStruct((n_tiles, LANES), I32)],
        input_output_aliases=aliases,
        scratch_shapes=[pltpu.VMEM((N_PAIRS, tm, LANES), F32)] * n_sub,
        compiler_params=pltpu.CompilerParams(dimension_semantics=("arbitrary",), vmem_limit_bytes=VMEM_LIMIT),
        name="mix",
    )(*args)


def _chunk(ref, row):
    return ref.at[pl.ds(pl.multiple_of(row, SLOT_CHUNK), SLOT_CHUNK), :]


def _grouped_expert_kernel(te_ref, nch_ref, rows_ref, used_ref, xs_hbm, wg_ref, wu_ref, wd_ref, ys_hbm,
                           xbuf, obuf, zbuf, wgu, wdn, gsem, ssem, zsem):
    g = pl.program_id(0)
    n_steps = pl.num_programs(0)
    cpt = EXPERT_TILE // SLOT_CHUNK
    blocks = used_ref.shape[0]

    def gather(step):
        half = step % 2

        @pl.when(nch_ref[step] > 0)
        def _():
            def one(c, carry):
                pltpu.make_async_copy(_chunk(xs_hbm, rows_ref[step * cpt + c]),
                                      _chunk(xbuf, (half * cpt + c) * SLOT_CHUNK), gsem.at[half]).start()
                return carry

            lax.fori_loop(0, cpt, one, 0, unroll=8)

    def wait_scatter(n):
        def one(_, carry):
            pltpu.make_async_copy(_chunk(obuf, 0), _chunk(ys_hbm, 0), ssem).wait()
            return carry

        lax.fori_loop(0, n, one, 0)

    def zero_fill(wait):
        used = used_ref[jnp.minimum(g, blocks - 1)]
        n_free = SLOTS_PER_TILE // SLOT_CHUNK - used
        bit = ZERO_CHUNKS
        while bit >= 1:
            row = g * SLOTS_PER_TILE + (used + (n_free & ~(2 * bit - 1))) * SLOT_CHUNK
            cp = pltpu.make_async_copy(
                zbuf.at[pl.ds(0, bit * SLOT_CHUNK), :],
                ys_hbm.at[pl.ds(pl.multiple_of(row, SLOT_CHUNK), bit * SLOT_CHUNK), :], zsem)
            pl.when((g < blocks) & ((n_free & bit) != 0))(cp.wait if wait else cp.start)
            bit //= 2

    @pl.when(g == 0)
    def _():
        zbuf[...] = jnp.zeros_like(zbuf)
        gather(0)

    pl.when(g + 1 < n_steps)(functools.partial(gather, g + 1))
    zero_fill(wait=False)

    @pl.when((g == 0) | (te_ref[g] != te_ref[jnp.maximum(g - 1, 0)]))
    def _():
        wgu[:, 0:D_EXPERT] = wg_ref[...].astype(BF16)
        wgu[:, D_EXPERT:2 * D_EXPERT] = wu_ref[...].astype(BF16)
        wdn[...] = wd_ref[...].astype(BF16)

    half = g % 2
    n_chunks = nch_ref[g]
    n_prev = jnp.where(g > 0, nch_ref[jnp.maximum(g - 1, 0)], 0)

    @pl.when(n_chunks > 0)
    def _():
        x_ref = xbuf.at[pl.ds(pl.multiple_of(half * EXPERT_TILE, EXPERT_TILE), EXPERT_TILE), :]
        pltpu.make_async_copy(xs_hbm.at[pl.ds(0, EXPERT_TILE), :], x_ref, gsem.at[half]).wait()
        ab = _dot(x_ref[...], wgu[...])
        a = ab[:, 0:D_EXPERT]
        hid = (a * jax.nn.sigmoid(a)) * ab[:, D_EXPERT:2 * D_EXPERT]
        out = _dot(hid.astype(BF16), wdn[...])
        wait_scatter(n_prev)
        obuf[...] = out.astype(BF16)

        def scatter(c, carry):
            pltpu.make_async_copy(_chunk(obuf, c * SLOT_CHUNK), _chunk(ys_hbm, rows_ref[g * cpt + c]), ssem).start()
            return carry

        lax.fori_loop(0, n_chunks, scatter, 0)

    pl.when(n_chunks == 0)(functools.partial(wait_scatter, n_prev))
    pl.when(g == n_steps - 1)(functools.partial(wait_scatter, n_chunks))
    zero_fill(wait=True)


def _grouped_experts(tile_expert, tile_chunks, rows, used, xs, wg, wu, wd):
    n_steps = tile_expert.shape[0]
    weights = lambda shape: pl.BlockSpec((None,) + shape, lambda g, te, *_: (te[g], 0, 0))
    any_spec = pl.BlockSpec(memory_space=pl.ANY)
    return pl.pallas_call(
        _grouped_expert_kernel,
        grid_spec=pltpu.PrefetchScalarGridSpec(
            num_scalar_prefetch=4, grid=(n_steps,),
            in_specs=[any_spec, weights((D_MODEL, D_EXPERT)), weights((D_MODEL, D_EXPERT)),
                      weights((D_EXPERT, D_MODEL))],
            out_specs=any_spec,
            scratch_shapes=[pltpu.VMEM((2 * EXPERT_TILE, D_MODEL), BF16), pltpu.VMEM((EXPERT_TILE, D_MODEL), BF16),
                            pltpu.VMEM((ZERO_CHUNKS * SLOT_CHUNK, D_MODEL), BF16),
                            pltpu.VMEM((D_MODEL, 2 * D_EXPERT), BF16), pltpu.VMEM((D_EXPERT, D_MODEL), BF16),
                            pltpu.SemaphoreType.DMA((2,)), pltpu.SemaphoreType.DMA, pltpu.SemaphoreType.DMA]),
        out_shape=jax.ShapeDtypeStruct(xs.shape, BF16),
        compiler_params=pltpu.CompilerParams(dimension_semantics=("arbitrary",), vmem_limit_bytes=VMEM_LIMIT),
        name="experts",
    )(tile_expert, tile_chunks, rows, used, xs, wg, wu, wd)


def _unsort_kernel(h_ref, route_ref, ys_ref, y_ref, *, n_sub):
    tm = h_ref.shape[0] // n_sub
    slot_id = lax.broadcasted_iota(I32, (tm, SLOTS_PER_TILE), 1).astype(F32)
    sels = []
    for sub in range(n_sub):
        route = route_ref[sub * tm:(sub + 1) * tm, :]
        sels.append((jnp.where(slot_id == route[:, 0:1], route[:, 2:3], 0.0) +
                     jnp.where(slot_id == route[:, 1:2], route[:, 3:4], 0.0)).astype(BF16))
    for sub, sel in enumerate(sels):
        rows = slice(sub * tm, (sub + 1) * tm)
        y_ref[rows, :] = h_ref[rows, :] + _dot(sel, ys_ref[sub * SLOTS_PER_TILE:(sub + 1) * SLOTS_PER_TILE, :])


def _unsort(h, route, ys, first_block):
    n = h.shape[0]
    tile = min(MIX_TILE, n)
    n_sub = MIX_SUBTILES if n % (tile * MIX_SUBTILES) == 0 else 1
    tm = tile * n_sub
    assert first_block % n_sub == 0
    return pl.pallas_call(
        functools.partial(_unsort_kernel, n_sub=n_sub),
        grid=(n // tm,),
        in_specs=[pl.BlockSpec((tm, D_MODEL), lambda i: (i, 0)), pl.BlockSpec((tm, SUBLANES), lambda i: (i, 0)),
                  pl.BlockSpec((n_sub * SLOTS_PER_TILE, D_MODEL), lambda i: (first_block // n_sub + i, 0))],
        out_specs=pl.BlockSpec((tm, D_MODEL), lambda i: (i, 0)),
        out_shape=jax.ShapeDtypeStruct((n, D_MODEL), F32),
        compiler_params=pltpu.CompilerParams(dimension_semantics=("arbitrary",), vmem_limit_bytes=VMEM_LIMIT),
        name="combine",
    )(h, route, ys)


def _grouped_moe(parts, xs, wg, wu, wd):
    cpt = EXPERT_TILE // SLOT_CHUNK
    blocks = xs.shape[0] // SLOTS_PER_TILE
    pad = lambda t: jnp.pad(t[:, :N_EXPERTS], ((0, blocks - t.shape[0]), (0, 0)))
    nck = pad(jnp.concatenate([p[2] for p in parts]))
    ck0 = pad(jnp.concatenate([p[3] for p in parts]))
    cum_incl = jnp.cumsum(nck, axis=0)
    cum_excl = cum_incl - nck
    per_expert = cum_incl[blocks - 1]
    tiles_e = (per_expert + cpt - 1) // cpt
    ends = jnp.cumsum(tiles_e)
    n_steps = blocks * (SLOTS_PER_TILE // SLOT_CHUNK) // cpt + N_EXPERTS
    step = jnp.arange(n_steps, dtype=I32)
    tile_expert = jnp.minimum(jnp.sum((step[:, None] >= ends[None, :]).astype(I32), axis=1), N_EXPERTS - 1)
    first_step = (ends - tiles_e)[tile_expert]
    mine = per_expert[tile_expert]
    tile_chunks = jnp.clip(mine - (step - first_step) * cpt, 0, cpt)
    tile_chunks = jnp.where(step < ends[N_EXPERTS - 1], tile_chunks, 0).astype(I32)
    p = (step - first_step)[:, None] * cpt + jnp.arange(cpt, dtype=I32)[None, :]
    p = jnp.clip(p, 0, jnp.maximum(mine - 1, 0)[:, None])[None, :, :]
    lo = cum_excl[:, tile_expert][:, :, None]
    hi = cum_incl[:, tile_expert][:, :, None]
    local = ck0[:, tile_expert][:, :, None] + p - lo
    block_row = (jnp.arange(blocks, dtype=I32) * SLOTS_PER_TILE)[:, None, None]
    rows = jnp.sum(jnp.where((lo <= p) & (p < hi), block_row + local * SLOT_CHUNK, 0), axis=0).astype(I32)
    used = (ck0[:, N_EXPERTS - 1] + nck[:, N_EXPERTS - 1]).astype(I32)
    ys = _grouped_experts(tile_expert, tile_chunks, rows.reshape(-1), used, xs, wg, wu, wd)
    outs, first = [], 0
    for h, route, nck_p, _ in parts:
        outs.append(_unsort(h, route, ys, first))
        first += nck_p.shape[0]
    return outs


def kernel(x_prompt, x_sample, cache_k, cache_v, norm1_g, w_in, q_gain, k_gain, v_gain, w_spatial, b_spatial,
           out_gain_a, out_gain_b, w_out, norm2_g, w_router1, b_router1, w_router2, b_router2, w_up, w_gate,
           w_down):
    depth = norm1_g.shape[0]
    assert depth == 1
    l = 0
    batch, seq, _ = x_prompt.shape
    nb, dec_seq, _ = x_sample.shape
    assert dec_seq == 1

    g1 = norm1_g[l].reshape(1, D_MODEL)
    win = w_in[l].astype(BF16)
    gains = jnp.stack([q_gain[l].reshape(WIDTH), k_gain[l].reshape(WIDTH), v_gain[l].reshape(WIDTH)])
    head_of_lane = jnp.arange(WIDTH) // HEAD_DIM
    bd = (head_of_lane[:, None] == head_of_lane[None, :]).astype(BF16)
    ws_tril = jnp.tril(w_spatial[l])
    wcat = jnp.concatenate([ws_tril[0::2], ws_tril[1::2]], axis=-1).astype(BF16)
    bs = b_spatial[l]
    bias = jnp.where(jnp.arange(LANES)[None, None, :] < HEAD_DIM, bs[0::2][:, :, None], bs[1::2][:, :, None])
    avec = jnp.repeat(w_spatial[l][:, 0, 0], HEAD_DIM).reshape(1, WIDTH)
    bvec = jnp.repeat(bs[:, 0], HEAD_DIM).reshape(1, WIDTH)
    ga = out_gain_a[l].reshape(1, WIDTH)
    gb = out_gain_b[l].reshape(1, WIDTH)
    wout = w_out[l].astype(BF16)
    g2 = norm2_g[l].reshape(1, D_MODEL)
    wr = jnp.zeros((D_MODEL, LANES), F32)
    wr = wr.at[:, :N_EXPERTS].set(jnp.transpose(w_router2[l], (1, 0, 2)).reshape(D_MODEL, N_EXPERTS))
    wr = wr.at[:, N_EXPERTS:N_EXPERTS + N_GROUPS].set(w_router1[l]).astype(BF16)
    br = jnp.zeros((1, LANES), F32)
    br = br.at[0, :N_EXPERTS].set(b_router2[l].reshape(N_EXPERTS))
    br = br.at[0, N_EXPERTS:N_EXPERTS + N_GROUPS].set(b_router1[l])
    lane_head = (jnp.arange(LANES)[:, None] == head_of_lane[None, :]).astype(BF16)
    expand = jnp.concatenate([lane_head, lane_head], axis=0)
    wg = w_gate[l].reshape(N_EXPERTS, D_MODEL, D_EXPERT)
    wu = w_up[l].reshape(N_EXPERTS, D_MODEL, D_EXPERT)
    wd = w_down[l].reshape(N_EXPERTS, D_EXPERT, D_MODEL)

    def tri(t):
        return (jnp.arange(t)[:, None] > jnp.arange(t)[None, :]).astype(BF16)

    xp = x_prompt.reshape(batch * seq, D_MODEL)
    qkv, oa, kt_last, vt_last = _proj_prompt(xp, g1, win, gains, bd, wcat, bias, seq=seq)
    obs, lses = [], []
    for (q, k, v), dil in zip(qkv, DILATIONS):
        o, lse = _band_attention(q, k, v, batch=batch, seq=seq, dil=dil)
        obs.append(o)
        lses.append(lse)
    tmix = min(MIX_TILE, batch * seq)
    upper = (jnp.arange(LANES)[:, None] < jnp.arange(LANES)[None, :]).astype(BF16)
    tmix_s = min(MIX_TILE, nb)
    h, sorted_rows, route, nck, ck0 = _mix(xp, oa, obs, lses, DILATIONS, ga, gb, wout, g2, wr, br, expand,
                                            tri(tmix), upper, spare_blocks=nb // tmix_s)
    kept = min(MAX_WINDOW, seq)
    to_cache = lambda t: jnp.transpose(t.reshape(1, batch, N_HEADS, HEAD_DIM, kept), (0, 1, 4, 2, 3))

    xs = x_sample.reshape(nb, D_MODEL)
    qs, ks, vs, vgs, oas = _proj_sample(xs, g1, win, gains, bd, avec, bvec)
    cols = lambda t: jnp.transpose(t.reshape(nb, N_HEADS, HEAD_DIM), (0, 2, 1))
    feature_major = lambda c: jnp.transpose(c, (0, 2, 3, 1)).reshape(nb, WIDTH, c.shape[1])
    ot = _sample_attention(cols(qs), cols(ks), cols(vs), feature_major(cache_k[l]), feature_major(cache_v[l]))
    obs_s = jnp.transpose(ot, (0, 2, 1)).reshape(nb, WIDTH)
    hs, sorted_rows, route_s, nck_s, ck0_s = _mix(xs, oas, [obs_s], [], (), ga, gb, wout, g2, wr, br, expand,
                                                  tri(tmix_s), upper, sorted_buf=sorted_rows,
                                                  first_block=batch * seq // tmix)

    y_prompt, y_sample = _grouped_moe([(h, route, nck, ck0), (hs, route_s, nck_s, ck0_s)], sorted_rows,
                                      wg, wu, wd)
    to5 = lambda t: t.reshape(1, nb, 1, N_HEADS, HEAD_DIM)
    return (y_prompt.reshape(batch, seq, D_MODEL), y_sample.reshape(nb, 1, D_MODEL),
            to_cache(kt_last), to_cache(vt_last), to5(ks), to5(vs), to5(vgs))
```

```python
import functools

import jax
import jax.numpy as jnp
from jax import lax
from jax.experimental import pallas as pl
from jax.experimental.pallas import tpu as pltpu

F32 = jnp.float32
BF16 = jnp.bfloat16
I32 = jnp.int32

D_MODEL = 1024
HEAD_DIM = 64
N_HEADS = 8
WIDTH = N_HEADS * HEAD_DIM
N_PAIRS = N_HEADS // 2
CHUNK = 128
N_WIN = 128
DILATIONS = (1, 4, 16)
MAX_WINDOW = 2048
N_GROUPS = 4
EXPERTS_PER_GROUP = 8
N_EXPERTS = N_GROUPS * EXPERTS_PER_GROUP
D_EXPERT = D_MODEL // 4
EPS = 1e-6
LOG2E = 1.4426950408889634
LN2 = 0.6931471805599453

LANES = 128
SUBLANES = 8
VMEM_LIMIT = 56 * 1024 * 1024

PROJ_TILE = 512
ATTN_TILE = 1024
MIX_TILE = 256
MIX_SUBTILES = 4
EXPERT_TILE = 512
SLOT_CHUNK = 2 * SUBLANES
MAX_TILE_CHUNKS = (2 * MIX_TILE + N_EXPERTS * (SLOT_CHUNK - 1)) // SLOT_CHUNK
SLOTS_PER_TILE = pl.cdiv(MAX_TILE_CHUNKS * SLOT_CHUNK, LANES) * LANES
ZERO_CHUNKS = 1 << ((SLOTS_PER_TILE // SLOT_CHUNK).bit_length() - 1)


def _dot(a, b):
    return jnp.dot(a, b, preferred_element_type=F32)


def _gelu(x):
    return 0.5 * x * (1.0 + jnp.tanh(0.7978845608028654 * (x + 0.044715 * (x * x * x))))


def _head_rms(t, gain, bd):
    ss = _dot((t * t).astype(BF16), bd)
    return t * lax.rsqrt(ss * (1.0 / HEAD_DIM) + EPS) * gain


def _proj_common(x_ref, g1_ref, win_ref, gains_ref, bd_ref, q_scale):
    x = x_ref[...]
    r = lax.rsqrt(jnp.mean(x * x, axis=-1, keepdims=True) + EPS)
    xn = (x * r * g1_ref[...]).astype(BF16)
    bd = bd_ref[...]
    q = _head_rms(_dot(xn, win_ref[:, 0:WIDTH]), gains_ref[0:1, :], bd) * q_scale
    k = _head_rms(_dot(xn, win_ref[:, WIDTH:2 * WIDTH]), gains_ref[1:2, :], bd)
    v = _dot(xn, win_ref[:, 2 * WIDTH:3 * WIDTH])
    u = _gelu(_dot(xn, win_ref[:, 3 * WIDTH:4 * WIDTH]))
    vg = _head_rms(_gelu(_dot(xn, win_ref[:, 4 * WIDTH:5 * WIDTH])), gains_ref[2:3, :], bd)
    return q, k, v, u, vg


def _proj_prompt_kernel(x_ref, g1_ref, win_ref, gains_ref, bd_ref, wcat_ref, bias_ref, *rest,
                        tiles_per_seq, first_kept_tile):
    stream_refs = rest[:3 * len(DILATIONS)]
    oa_ref, kl_ref, vl_ref, slab_a, slab_b = rest[3 * len(DILATIONS):]
    q, k, v, u, vg = _proj_common(x_ref, g1_ref, win_ref, gains_ref, bd_ref, HEAD_DIM ** -0.5 * LOG2E)
    tm = x_ref.shape[0]
    for a, z in enumerate((q, k, v)):
        cur, nxt = slab_a, slab_b
        for p in range(N_PAIRS):
            cur[p] = z[:, p * LANES:(p + 1) * LANES]
        for c, dil in enumerate(DILATIONS):
            out_ref = stream_refs[3 * c + a]
            if dil == 1:
                out_ref[...] = z.astype(BF16)
                continue
            prev = DILATIONS[c - 1]
            rows, rows_prev = tm // dil, tm // prev
            for r in range(dil):
                for p in range(N_PAIRS):
                    part = cur[p, pl.ds((r % prev) * rows_prev + r // prev, rows, stride=dil // prev), :]
                    out_ref[:, r * WIDTH + p * LANES:r * WIDTH + (p + 1) * LANES] = part.astype(BF16)
                    if c + 1 < len(DILATIONS):
                        nxt[p, r * rows:(r + 1) * rows, :] = part
            cur, nxt = nxt, cur

    @pl.when(pl.program_id(0) % tiles_per_seq >= first_kept_tile)
    def _():
        kl_ref[...] = k.T
        vl_ref[...] = v.T

    vgb = vg.astype(BF16)
    lane = lax.broadcasted_iota(I32, (CHUNK, LANES), 1)
    tm = x_ref.shape[0]
    for c in range(tm // CHUNK):
        rows = slice(c * CHUNK, (c + 1) * CHUNK)
        for p in range(N_PAIRS):
            cols = slice(p * LANES, (p + 1) * LANES)
            vp = vgb[rows, cols]
            zero = jnp.zeros_like(vp)
            rhs = jnp.concatenate([jnp.where(lane < HEAD_DIM, vp, zero),
                                   jnp.where(lane >= HEAD_DIM, vp, zero)], axis=0)
            mixed = _dot(wcat_ref[p], rhs) + bias_ref[p]
            oa_ref[rows, cols] = (u[rows, cols] * mixed).astype(BF16)


def _proj_sample_kernel(x_ref, g1_ref, win_ref, gains_ref, bd_ref, avec_ref, bvec_ref,
                        q_ref, k_ref, v_ref, vg_ref, oa_ref):
    q, k, v, u, vg = _proj_common(x_ref, g1_ref, win_ref, gains_ref, bd_ref, HEAD_DIM ** -0.5)
    q_ref[...] = q
    k_ref[...] = k
    v_ref[...] = v
    vg_ref[...] = vg
    oa_ref[...] = (u * (avec_ref[...] * vg + bvec_ref[...])).astype(BF16)


def _const_spec(shape):
    return pl.BlockSpec(shape, lambda *_: (0,) * len(shape))


def _proj_prompt(x2, g1, win, gains, bd, wcat, bias, *, seq):
    n = x2.shape[0]
    tm = PROJ_TILE
    tiles_per_seq = seq // tm
    kept = min(MAX_WINDOW, seq)
    first_kept_tile = tiles_per_seq - kept // tm

    def kept_map(i):
        return (i // tiles_per_seq, 0, jnp.maximum(i % tiles_per_seq - first_kept_tile, 0))

    row_spec = pl.BlockSpec((tm, WIDTH), lambda i: (i, 0))
    kept_spec = pl.BlockSpec((None, WIDTH, tm), kept_map)
    stream_specs = [pl.BlockSpec((tm // dil, dil * WIDTH), lambda i: (i, 0)) for dil in DILATIONS for _ in range(3)]
    stream_shapes = [jax.ShapeDtypeStruct((n // dil, dil * WIDTH), BF16) for dil in DILATIONS for _ in range(3)]
    outs = pl.pallas_call(
        functools.partial(_proj_prompt_kernel, tiles_per_seq=tiles_per_seq, first_kept_tile=first_kept_tile),
        grid=(n // tm,),
        in_specs=[pl.BlockSpec((tm, D_MODEL), lambda i: (i, 0)),
                  _const_spec((1, D_MODEL)), _const_spec((D_MODEL, 5 * WIDTH)), _const_spec((3, WIDTH)),
                  _const_spec((WIDTH, WIDTH)), _const_spec((N_PAIRS, CHUNK, 2 * CHUNK)),
                  _const_spec((N_PAIRS, CHUNK, LANES))],
        out_specs=stream_specs + [row_spec, kept_spec, kept_spec],
        out_shape=stream_shapes + [jax.ShapeDtypeStruct((n, WIDTH), BF16)] +
                  [jax.ShapeDtypeStruct((n // seq, WIDTH, kept), F32)] * 2,
        scratch_shapes=[pltpu.VMEM((N_PAIRS, tm, LANES), F32)] * 2,
        compiler_params=pltpu.CompilerParams(dimension_semantics=("arbitrary",), vmem_limit_bytes=VMEM_LIMIT),
        name="proj_prompt",
    )(x2, g1, win, gains, bd, wcat, bias)
    n_streams = 3 * len(DILATIONS)
    qkv = [outs[3 * c:3 * c + 3] for c in range(len(DILATIONS))]
    return (qkv, *outs[n_streams:])


def _proj_sample(x2, g1, win, gains, bd, avec, bvec):
    n = x2.shape[0]
    full = _const_spec((n, WIDTH))
    return pl.pallas_call(
        _proj_sample_kernel,
        grid=(1,),
        in_specs=[_const_spec((n, D_MODEL)), _const_spec((1, D_MODEL)), _const_spec((D_MODEL, 5 * WIDTH)),
                  _const_spec((3, WIDTH)), _const_spec((WIDTH, WIDTH)), _const_spec((1, WIDTH)),
                  _const_spec((1, WIDTH))],
        out_specs=[full] * 5,
        out_shape=[jax.ShapeDtypeStruct((n, WIDTH), F32)] * 4 + [jax.ShapeDtypeStruct((n, WIDTH), BF16)],
        compiler_params=pltpu.CompilerParams(dimension_semantics=("arbitrary",), vmem_limit_bytes=VMEM_LIMIT),
        name="proj_sample",
    )(x2, g1, win, gains, bd, avec, bvec)


def _band_attn_kernel(q_ref, k_ref, v_ref, o_ref, lse_ref, kprev, vprev, *, tq):
    j = pl.program_id(2)

    @pl.when(j == 0)
    def _():
        kprev[...] = jnp.zeros_like(kprev)
        vprev[...] = jnp.zeros_like(vprev)

    qi = lax.broadcasted_iota(I32, (2 * N_WIN, 2 * N_WIN), 0) % N_WIN
    ki = lax.broadcasted_iota(I32, (2 * N_WIN, 2 * N_WIN), 1)
    band = (ki >= qi) & (ki <= qi + N_WIN)
    lane = lax.broadcasted_iota(I32, (N_WIN, LANES), 1)
    low_half = lane < HEAD_DIM
    ones = jnp.ones((2 * N_WIN, LANES), BF16)

    for jb in range(tq // N_WIN):
        rows = slice(jb * N_WIN, (jb + 1) * N_WIN)
        window = slice((jb - 1) * N_WIN, (jb + 1) * N_WIN)
        valid = band & ((ki >= N_WIN) | (j > 0)) if jb == 0 else band
        lse_tile = jnp.zeros((N_WIN, LANES), F32)
        for p in range(N_PAIRS):
            cols = slice(p * LANES, (p + 1) * LANES)
            qp = q_ref[rows, cols]
            if jb == 0:
                kp = jnp.concatenate([kprev[:, cols], k_ref[rows, cols]], axis=0)
                vv = jnp.concatenate([vprev[:, cols], v_ref[rows, cols]], axis=0)
            else:
                kp = k_ref[window, cols]
                vv = v_ref[window, cols]
            zero = jnp.zeros_like(qp)
            q2 = jnp.concatenate([jnp.where(low_half, qp, zero), jnp.where(low_half, zero, qp)], axis=0)
            s = lax.dot_general(q2, kp, (((1,), (1,)), ((), ())), preferred_element_type=F32)
            s = jnp.where(valid, s, -jnp.inf)
            m = jnp.max(s, axis=-1, keepdims=True)
            e = jnp.exp2((s - m).astype(BF16))
            ov = _dot(e, jnp.concatenate([vv, ones], axis=1))
            den = ov[:, LANES:2 * LANES]
            o = ov[:, 0:LANES] / den
            lse = m * LN2 + jnp.log(den)
            lse_tile = jnp.where(lane == 2 * p, lse[0:N_WIN],
                                 jnp.where(lane == 2 * p + 1, lse[N_WIN:2 * N_WIN], lse_tile))
            o_ref[rows, cols] = jnp.where(low_half, o[0:N_WIN], o[N_WIN:2 * N_WIN]).astype(BF16)
        lse_ref[rows, :] = lse_tile

    kprev[...] = k_ref[tq - N_WIN:tq, :]
    vprev[...] = v_ref[tq - N_WIN:tq, :]


def _band_attention(q2, k2, v2, *, batch, seq, dil):
    length = seq // dil
    tq = min(ATTN_TILE, length)
    view = lambda a: a.reshape(batch, length, a.shape[-1])
    spec = pl.BlockSpec((None, tq, WIDTH), lambda b, r, j: (b, j, r))
    lse_spec = pl.BlockSpec((None, tq, LANES), lambda b, r, j: (b, j, r))
    o, lse = pl.pallas_call(
        functools.partial(_band_attn_kernel, tq=tq),
        grid=(batch, dil, length // tq),
        in_specs=[spec, spec, spec],
        out_specs=[spec, lse_spec],
        out_shape=[jax.ShapeDtypeStruct((batch, length, dil * WIDTH), BF16),
                   jax.ShapeDtypeStruct((batch, length, dil * LANES), F32)],
        scratch_shapes=[pltpu.VMEM((N_WIN, WIDTH), BF16), pltpu.VMEM((N_WIN, WIDTH), BF16)],
        compiler_params=pltpu.CompilerParams(dimension_semantics=("arbitrary", "arbitrary", "arbitrary"),
                                             vmem_limit_bytes=VMEM_LIMIT),
        name=f"band_attn_d{dil}",
    )(view(q2), view(k2), view(v2))
    return o.reshape(batch * length, dil * WIDTH), lse.reshape(batch * length, dil * LANES)


def _sample_attn_kernel(qt_ref, knt_ref, vnt_ref, kt_ref, vt_ref, ot_ref):
    qt = qt_ref[...]
    vnt = vnt_ref[...]
    s_new = jnp.sum(qt * knt_ref[...], axis=0, keepdims=True)
    wb = kt_ref.shape[1]
    heads = range(N_HEADS)
    head_rows = lambda h: slice(h * HEAD_DIM, (h + 1) * HEAD_DIM)
    s = jnp.concatenate([jnp.sum(kt_ref[head_rows(h), :] * qt[:, h:h + 1], axis=0, keepdims=True) for h in heads],
                        axis=0)
    s0 = jnp.concatenate([s_new[:, h:h + 1] for h in heads], axis=0)
    pos = lax.broadcasted_iota(I32, (N_HEADS, wb), 1)
    es, e0s, dens, lses = [], [], [], []
    for dil in DILATIONS:
        lo = wb - N_WIN * dil
        sc = s[:, lo:]
        valid = (pos[:, lo:] & (dil - 1)) == 0
        m = jnp.maximum(jnp.max(jnp.where(valid, sc, -jnp.inf), axis=-1, keepdims=True), s0)
        e = jnp.where(valid, jnp.exp(sc - m), 0.0)
        e0 = jnp.exp(s0 - m)
        den = jnp.sum(e, axis=-1, keepdims=True) + e0
        es.append(e)
        e0s.append(e0)
        dens.append(den)
        lses.append(m + jnp.log(den))
    mm = jnp.maximum(jnp.maximum(lses[0], lses[1]), lses[2])
    ws = [jnp.exp(l - mm) for l in lses]
    tot = ws[0] + ws[1] + ws[2]
    coef = [w / (tot * den) for w, den in zip(ws, dens)]
    w1, w4, w16 = [e * c for e, c in zip(es, coef)]
    w_new = coef[0] * e0s[0] + coef[1] * e0s[1] + coef[2] * e0s[2]
    n1, n4 = N_WIN * DILATIONS[0], N_WIN * DILATIONS[1]
    w_pos = jnp.concatenate([w16[:, :wb - n4], w16[:, wb - n4:wb - n1] + w4[:, :n4 - n1],
                             w16[:, wb - n1:] + w4[:, n4 - n1:] + w1], axis=1)
    head_lane = lax.broadcasted_iota(I32, (HEAD_DIM, N_HEADS), 1)
    ot = jnp.zeros((HEAD_DIM, N_HEADS), F32)
    for h in heads:
        col = (jnp.sum(vt_ref[head_rows(h), :] * w_pos[h:h + 1, :], axis=-1, keepdims=True) +
               w_new[h:h + 1, :] * vnt[:, h:h + 1])
        ot = jnp.where(head_lane == h, col, ot)
    ot_ref[...] = ot


def _sample_attention(qt, knt, vnt, kt, vt):
    nb, _, wb = kt.shape
    assert wb == MAX_WINDOW, "cache window must cover every dilated key"
    tok = pl.BlockSpec((None, HEAD_DIM, N_HEADS), lambda b: (b, 0, 0))
    cache = pl.BlockSpec((None, WIDTH, wb), lambda b: (b, 0, 0))
    return pl.pallas_call(
        _sample_attn_kernel,
        grid=(nb,),
        in_specs=[tok, tok, tok, cache, cache],
        out_specs=tok,
        out_shape=jax.ShapeDtypeStruct((nb, HEAD_DIM, N_HEADS), F32),
        compiler_params=pltpu.CompilerParams(dimension_semantics=("arbitrary",), vmem_limit_bytes=VMEM_LIMIT),
        name="sample_attn",
    )(qt, knt, vnt, kt, vt)


def _split_hi_lo(w):
    hi = w.astype(BF16)
    return jnp.concatenate([hi, (w - hi.astype(F32)).astype(BF16)], axis=-1)


def _token_order(ref, dil, width, slab, tm):
    if dil == 1:
        return ref[...].astype(F32)
    n_slabs = width // LANES
    for r in range(dil):
        for p in range(n_slabs):
            cols = slice(r * width + p * LANES, r * width + (p + 1) * LANES)
            slab[p, pl.ds(r, tm // dil, stride=dil), :] = ref[:, cols].astype(F32)
    return jnp.concatenate([slab[p] for p in range(n_slabs)], axis=1)


def _mix_kernel(*refs, dils, n_steps, n_sub, aliased):
    n_cfg = max(len(dils), 1)
    n_lse = len(dils)
    x_ref, oa_ref = refs[0], refs[1]
    o_refs = refs[2:2 + n_cfg]
    lse_refs = refs[2 + n_cfg:2 + n_cfg + n_lse]
    rest = refs[2 + n_cfg + n_lse:]
    consts = rest[:9]
    h_ref, xs_ref, route_ref, nchunk_ref, chunk0_ref, *slabs = rest[9 + int(aliased):]
    tm = x_ref.shape[0] // n_sub
    i = pl.program_id(0)
    rows_of = lambda ref, sub, n: ref.at[pl.ds(sub * n, n), :]
    tiles = [
        _mix_tile(rows_of(x_ref, sub, tm), rows_of(oa_ref, sub, tm),
                  [rows_of(r, sub, tm // dil) for r, dil in zip(o_refs, dils or (1,))],
                  [rows_of(r, sub, tm // dil) for r, dil in zip(lse_refs, dils)], *consts,
                  rows_of(h_ref, sub, tm), rows_of(xs_ref, sub, SLOTS_PER_TILE), rows_of(route_ref, sub, tm),
                  nchunk_ref, chunk0_ref, slabs[sub], dils=dils,
                  table_row=jnp.minimum(i, n_steps - 1) * n_sub + sub, live=i < n_steps)
        for sub in range(n_sub)]
    for _ in zip(*tiles):
        pass


def _mix_tile(x_ref, oa_ref, o_refs, lse_refs, ga_ref, gb_ref, wout_ref, g2_ref, wr_ref, br_ref, expand_ref,
              tri_ref, upper_ref, h_ref, xs_ref, route_ref, nchunk_ref, chunk0_ref, slab, *, dils, table_row,
              live):
    tm = x_ref.shape[0]

    if not dils:
        ob = o_refs[0][...].astype(F32)
    else:
        lses = [_token_order(r, dil, LANES, slab, tm) for r, dil in zip(lse_refs, dils)]
        mm = functools.reduce(jnp.maximum, lses)
        ws = [jnp.exp(l - mm) for l in lses]
        tot = functools.reduce(lambda a, b: a + b, ws)
        ob = jnp.zeros((tm, WIDTH), F32)
        for w, o_ref, dil in zip(ws, o_refs, dils):
            ob = ob + _dot(_split_hi_lo(w / tot), expand_ref[...]) * _token_order(o_ref, dil, WIDTH, slab, tm)

    oa = oa_ref[...].astype(F32)
    ya = oa * lax.rsqrt(jnp.mean(oa * oa, axis=-1, keepdims=True) + EPS) * ga_ref[...]
    yb = ob * lax.rsqrt(jnp.mean(ob * ob, axis=-1, keepdims=True) + EPS) * gb_ref[...]
    cat = jnp.concatenate([ya, yb], axis=-1).astype(BF16)
    yield
    h = x_ref[...] + _dot(cat, wout_ref[...])
    h_ref[...] = h
    hn = (h * lax.rsqrt(jnp.mean(h * h, axis=-1, keepdims=True) + EPS) * g2_ref[...]).astype(BF16)
    yield

    logits = _dot(hn, wr_ref[...]) + br_ref[...]
    lane = lax.broadcasted_iota(I32, (tm, LANES), 1)
    lane_f = lane.astype(F32)
    neg = -jnp.inf
    big = float(LANES)
    coarse = (lane >= N_EXPERTS) & (lane < N_EXPERTS + N_GROUPS)
    lg = jnp.where(coarse, logits, neg)
    mx = jnp.max(lg, axis=-1, keepdims=True)
    g_lane = jnp.min(jnp.where(lg == mx, lane_f, big), axis=-1, keepdims=True)
    p_star = 1.0 / jnp.sum(jnp.exp(lg - mx), axis=-1, keepdims=True)
    lo = (g_lane - float(N_EXPERTS)) * float(EXPERTS_PER_GROUP)
    lf = jnp.where((lane_f >= lo) & (lane_f < lo + float(EXPERTS_PER_GROUP)), logits, neg)
    v1 = jnp.max(lf, axis=-1, keepdims=True)
    i1 = jnp.min(jnp.where(lf == v1, lane_f, big), axis=-1, keepdims=True)
    lf2 = jnp.where(lane_f == i1, neg, lf)
    v2 = jnp.max(lf2, axis=-1, keepdims=True)
    i2 = jnp.min(jnp.where(lf2 == v2, lane_f, big), axis=-1, keepdims=True)
    e21 = jnp.exp(v2 - v1)
    w1 = p_star / (1.0 + e21)
    w2 = p_star * e21 / (1.0 + e21)
    yield

    sel1 = lane_f == i1
    sel2 = lane_f == i2
    onehot = jnp.where(sel1 | sel2, 1.0, 0.0)
    rank = _dot(tri_ref[...], onehot.astype(BF16))
    count = jnp.sum(onehot, axis=0, keepdims=True)
    chunks = jnp.floor((count + float(SLOT_CHUNK - 1)) * (1.0 / SLOT_CHUNK))
    chunk0 = _dot(jnp.broadcast_to(chunks, (SUBLANES, LANES)).astype(BF16), upper_ref[...])[0:1, :]
    slot = rank + chunk0 * float(SLOT_CHUNK)
    s1 = jnp.sum(jnp.where(sel1, slot, 0.0), axis=-1, keepdims=True)
    s2 = jnp.sum(jnp.where(sel2, slot, 0.0), axis=-1, keepdims=True)
    nchunk_ref[pl.ds(table_row, 1), :] = chunks.astype(I32)
    chunk0_ref[pl.ds(table_row, 1), :] = chunk0.astype(I32)
    yield

    s1_row = jnp.broadcast_to(s1, (tm, LANES)).T[0:1, :]
    s2_row = jnp.broadcast_to(s2, (tm, LANES)).T[0:1, :]
    slot_id = lax.broadcasted_iota(I32, (SLOTS_PER_TILE, tm), 0).astype(F32)
    perm = jnp.where((slot_id == s1_row) | (slot_id == s2_row), 1.0, 0.0).astype(BF16)
    xs_ref[...] = jnp.where(live, _dot(perm, hn), 0.0).astype(BF16)

    col = lax.broadcasted_iota(I32, (tm, SUBLANES), 1)
    route = jnp.zeros((tm, SUBLANES), F32)
    for idx, val in enumerate((s1, s2, w1, w2)):
        route = jnp.where(col == idx, val, route)
    route_ref[...] = route
    yield


def _mix(x2, oa, obs, lses, dils, ga, gb, wout, g2, wr, br, expand, tri, upper, *, sorted_buf=None,
         first_block=0, spare_blocks=0):
    n = x2.shape[0]
    tm = min(MIX_TILE, n)
    n_sub = MIX_SUBTILES if n % (tm * MIX_SUBTILES) == 0 else 1
    step = tm * n_sub
    n_steps = n // step
    n_tiles = n_steps * n_sub
    spare_steps = pl.cdiv(spare_blocks, n_sub)
    last = n_steps - 1
    assert first_block % n_sub == 0
    row = lambda w: pl.BlockSpec((step, w), lambda i: (jnp.minimum(i, last), 0))
    stream = lambda w, dil: pl.BlockSpec((step // dil, dil * w), lambda i: (jnp.minimum(i, last), 0))
    in_specs = ([row(D_MODEL), row(WIDTH)] + [stream(WIDTH, dil) for dil in (dils or (1,))] +
                [stream(LANES, dil) for dil in dils] +
                [_const_spec((1, WIDTH)), _const_spec((1, WIDTH)), _const_spec((D_MODEL, D_MODEL)),
                 _const_spec((1, D_MODEL)), _const_spec((D_MODEL, LANES)), _const_spec((1, LANES)),
                 _const_spec((2 * LANES, WIDTH)), _const_spec((tm, tm)), _const_spec((LANES, LANES))])
    args = [x2, oa, *obs, *lses, ga, gb, wout, g2, wr, br, expand, tri, upper]
    aliases = {}
    if sorted_buf is None:
        sorted_shape = ((n_steps + spare_steps) * n_sub * SLOTS_PER_TILE, D_MODEL)
    else:
        assert spare_blocks == 0
        sorted_shape = sorted_buf.shape
        aliases = {len(args): 1}
        in_specs.append(pl.BlockSpec(memory_space=pl.ANY))
        args.append(sorted_buf)
    return pl.pallas_call(
        functools.partial(_mix_kernel, dils=tuple(dils), n_steps=n_steps, n_sub=n_sub,
                          aliased=sorted_buf is not None),
        grid=(n_steps + spare_steps,),
        in_specs=in_specs,
        out_specs=[row(D_MODEL),
                   pl.BlockSpec((n_sub * SLOTS_PER_TILE, D_MODEL), lambda i: (first_block // n_sub + i, 0)),
                   row(SUBLANES), _const_spec((n_tiles, LANES)), _const_spec((n_tiles, LANES))],
        out_shape=[jax.ShapeDtypeStruct((n, D_MODEL), F32), jax.ShapeDtypeStruct(sorted_shape, BF16),
                   jax.ShapeDtypeStruct((n, SUBLANES), F32),
                   jax.ShapeDtypeStruct((n_tiles, LANES), I32), jax.ShapeDtypeStruct((n_tiles, LANES), I32)],
        input_output_aliases=aliases,
        scratch_shapes=[pltpu.VMEM((N_PAIRS, tm, LANES), F32)] * n_sub,
        compiler_params=pltpu.CompilerParams(dimension_semantics=("arbitrary",), vmem_limit_bytes=VMEM_LIMIT),
        name="mix",
    )(*args)


def _chunk(ref, row):
    return ref.at[pl.ds(pl.multiple_of(row, SLOT_CHUNK), SLOT_CHUNK), :]


def _grouped_expert_kernel(te_ref, nch_ref, rows_ref, used_ref, xs_hbm, wg_ref, wu_ref, wd_ref, ys_hbm,
                           xbuf, obuf, zbuf, wgu, wdn, gsem, ssem, zsem):
    g = pl.program_id(0)
    n_steps = pl.num_programs(0)
    cpt = EXPERT_TILE // SLOT_CHUNK
    blocks = used_ref.shape[0]

    def gather(step):
        half = step % 2

        @pl.when(nch_ref[step] > 0)
        def _():
            def one(c, carry):
                pltpu.make_async_copy(_chunk(xs_hbm, rows_ref[step * cpt + c]),
                                      _chunk(xbuf, (half * cpt + c) * SLOT_CHUNK), gsem.at[half]).start()
                return carry

            lax.fori_loop(0, cpt, one, 0, unroll=8)

    def wait_scatter(n):
        def one(_, carry):
            pltpu.make_async_copy(_chunk(obuf, 0), _chunk(ys_hbm, 0), ssem).wait()
            return carry

        lax.fori_loop(0, n, one, 0)

    def zero_fill(wait):
        used = used_ref[jnp.minimum(g, blocks - 1)]
        n_free = SLOTS_PER_TILE // SLOT_CHUNK - used
        bit = ZERO_CHUNKS
        while bit >= 1:
            row = g * SLOTS_PER_TILE + (used + (n_free & ~(2 * bit - 1))) * SLOT_CHUNK
            cp = pltpu.make_async_copy(
                zbuf.at[pl.ds(0, bit * SLOT_CHUNK), :],
                ys_hbm.at[pl.ds(pl.multiple_of(row, SLOT_CHUNK), bit * SLOT_CHUNK), :], zsem)
            pl.when((g < blocks) & ((n_free & bit) != 0))(cp.wait if wait else cp.start)
            bit //= 2

    @pl.when(g == 0)
    def _():
        zbuf[...] = jnp.zeros_like(zbuf)
        gather(0)

    pl.when(g + 1 < n_steps)(functools.partial(gather, g + 1))
    zero_fill(wait=False)

    @pl.when((g == 0) | (te_ref[g] != te_ref[jnp.maximum(g - 1, 0)]))
    def _():
        wgu[:, 0:D_EXPERT] = wg_ref[...].astype(BF16)
        wgu[:, D_EXPERT:2 * D_EXPERT] = wu_ref[...].astype(BF16)
        wdn[...] = wd_ref[...].astype(BF16)

    half = g % 2
    n_chunks = nch_ref[g]
    n_prev = jnp.where(g > 0, nch_ref[jnp.maximum(g - 1, 0)], 0)

    @pl.when(n_chunks > 0)
    def _():
        x_ref = xbuf.at[pl.ds(pl.multiple_of(half * EXPERT_TILE, EXPERT_TILE), EXPERT_TILE), :]
        pltpu.make_async_copy(xs_hbm.at[pl.ds(0, EXPERT_TILE), :], x_ref, gsem.at[half]).wait()
        ab = _dot(x_ref[...], wgu[...])
        a = ab[:, 0:D_EXPERT]
        hid = (a * jax.nn.sigmoid(a)) * ab[:, D_EXPERT:2 * D_EXPERT]
        out = _dot(hid.astype(BF16), wdn[...])
        wait_scatter(n_prev)
        obuf[...] = out.astype(BF16)

        def scatter(c, carry):
            pltpu.make_async_copy(_chunk(obuf, c * SLOT_CHUNK), _chunk(ys_hbm, rows_ref[g * cpt + c]), ssem).start()
            return carry

        lax.fori_loop(0, n_chunks, scatter, 0)

    pl.when(n_chunks == 0)(functools.partial(wait_scatter, n_prev))
    pl.when(g == n_steps - 1)(functools.partial(wait_scatter, n_chunks))
    zero_fill(wait=True)


def _grouped_experts(tile_expert, tile_chunks, rows, used, xs, wg, wu, wd):
    n_steps = tile_expert.shape[0]
    weights = lambda shape: pl.BlockSpec((None,) + shape, lambda g, te, *_: (te[g], 0, 0))
    any_spec = pl.BlockSpec(memory_space=pl.ANY)
    return pl.pallas_call(
        _grouped_expert_kernel,
        grid_spec=pltpu.PrefetchScalarGridSpec(
            num_scalar_prefetch=4, grid=(n_steps,),
            in_specs=[any_spec, weights((D_MODEL, D_EXPERT)), weights((D_MODEL, D_EXPERT)),
                      weights((D_EXPERT, D_MODEL))],
            out_specs=any_spec,
            scratch_shapes=[pltpu.VMEM((2 * EXPERT_TILE, D_MODEL), BF16), pltpu.VMEM((EXPERT_TILE, D_MODEL), BF16),
                            pltpu.VMEM((ZERO_CHUNKS * SLOT_CHUNK, D_MODEL), BF16),
                            pltpu.VMEM((D_MODEL, 2 * D_EXPERT), BF16), pltpu.VMEM((D_EXPERT, D_MODEL), BF16),
                            pltpu.SemaphoreType.DMA((2,)), pltpu.SemaphoreType.DMA, pltpu.SemaphoreType.DMA]),
        out_shape=jax.ShapeDtypeStruct(xs.shape, BF16),
        compiler_params=pltpu.CompilerParams(dimension_semantics=("arbitrary",), vmem_limit_bytes=VMEM_LIMIT),
        name="experts",
    )(tile_expert, tile_chunks, rows, used, xs, wg, wu, wd)


def _unsort_kernel(h_ref, route_ref, ys_ref, y_ref, *, n_sub):
    tm = h_ref.shape[0] // n_sub
    slot_id = lax.broadcasted_iota(I32, (tm, SLOTS_PER_TILE), 1).astype(F32)
    sels = []
    for sub in range(n_sub):
        route = route_ref[sub * tm:(sub + 1) * tm, :]
        sels.append((jnp.where(slot_id == route[:, 0:1], route[:, 2:3], 0.0) +
                     jnp.where(slot_id == route[:, 1:2], route[:, 3:4], 0.0)).astype(BF16))
    for sub, sel in enumerate(sels):
        rows = slice(sub * tm, (sub + 1) * tm)
        y_ref[rows, :] = h_ref[rows, :] + _dot(sel, ys_ref[sub * SLOTS_PER_TILE:(sub + 1) * SLOTS_PER_TILE, :])


def _unsort(h, route, ys, first_block):
    n = h.shape[0]
    tile = min(MIX_TILE, n)
    n_sub = MIX_SUBTILES if n % (tile * MIX_SUBTILES) == 0 else 1
    tm = tile * n_sub
    assert first_block % n_sub == 0
    return pl.pallas_call(
        functools.partial(_unsort_kernel, n_sub=n_sub),
        grid=(n // tm,),
        in_specs=[pl.BlockSpec((tm, D_MODEL), lambda i: (i, 0)), pl.BlockSpec((tm, SUBLANES), lambda i: (i, 0)),
                  pl.BlockSpec((n_sub * SLOTS_PER_TILE, D_MODEL), lambda i: (first_block // n_sub + i, 0))],
        out_specs=pl.BlockSpec((tm, D_MODEL), lambda i: (i, 0)),
        out_shape=jax.ShapeDtypeStruct((n, D_MODEL), F32),
        compiler_params=pltpu.CompilerParams(dimension_semantics=("arbitrary",), vmem_limit_bytes=VMEM_LIMIT),
        name="combine",
    )(h, route, ys)


def _grouped_moe(parts, xs, wg, wu, wd):
    cpt = EXPERT_TILE // SLOT_CHUNK
    blocks = xs.shape[0] // SLOTS_PER_TILE
    pad = lambda t: jnp.pad(t[:, :N_EXPERTS], ((0, blocks - t.shape[0]), (0, 0)))
    nck = pad(jnp.concatenate([p[2] for p in parts]))
    ck0 = pad(jnp.concatenate([p[3] for p in parts]))
    cum_incl = jnp.cumsum(nck, axis=0)
    cum_excl = cum_incl - nck
    per_expert = cum_incl[blocks - 1]
    tiles_e = (per_expert + cpt - 1) // cpt
    ends = jnp.cumsum(tiles_e)
    n_steps = blocks * MAX_TILE_CHUNKS // cpt + N_EXPERTS
    step = jnp.arange(n_steps, dtype=I32)
    tile_expert = jnp.minimum(jnp.sum((step[:, None] >= ends[None, :]).astype(I32), axis=1), N_EXPERTS - 1)
    is_expert = tile_expert[:, None] == jnp.arange(N_EXPERTS, dtype=I32)[None, :]
    pick = lambda table: jnp.sum(jnp.where(is_expert, table[None, :], 0), axis=1)
    first_step = pick(ends - tiles_e)
    mine = pick(per_expert)
    tile_chunks = jnp.clip(mine - (step - first_step) * cpt, 0, cpt)
    tile_chunks = jnp.where(step < ends[N_EXPERTS - 1], tile_chunks, 0).astype(I32)
    p = (step - first_step)[:, None] * cpt + jnp.arange(cpt, dtype=I32)[None, :]
    p = jnp.clip(p, 0, jnp.maximum(mine - 1, 0)[:, None])[None, :, :]
    lo = cum_excl[:, tile_expert][:, :, None]
    hi = cum_incl[:, tile_expert][:, :, None]
    local = ck0[:, tile_expert][:, :, None] + p - lo
    block_row = (jnp.arange(blocks, dtype=I32) * SLOTS_PER_TILE)[:, None, None]
    rows = jnp.sum(jnp.where((lo <= p) & (p < hi), block_row + local * SLOT_CHUNK, 0), axis=0).astype(I32)
    used = (ck0[:, N_EXPERTS - 1] + nck[:, N_EXPERTS - 1]).astype(I32)
    ys = _grouped_experts(tile_expert, tile_chunks, rows.reshape(-1), used, xs, wg, wu, wd)
    outs, first = [], 0
    for h, route, nck_p, _ in parts:
        outs.append(_unsort(h, route, ys, first))
        first += nck_p.shape[0]
    return outs


def kernel(x_prompt, x_sample, cache_k, cache_v, norm1_g, w_in, q_gain, k_gain, v_gain, w_spatial, b_spatial,
           out_gain_a, out_gain_b, w_out, norm2_g, w_router1, b_router1, w_router2, b_router2, w_up, w_gate,
           w_down):
    depth = norm1_g.shape[0]
    assert depth == 1
    l = 0
    batch, seq, _ = x_prompt.shape
    nb, dec_seq, _ = x_sample.shape
    assert dec_seq == 1

    g1 = norm1_g[l].reshape(1, D_MODEL)
    win = w_in[l].astype(BF16)
    gains = jnp.stack([q_gain[l].reshape(WIDTH), k_gain[l].reshape(WIDTH), v_gain[l].reshape(WIDTH)])
    head_of_lane = jnp.arange(WIDTH) // HEAD_DIM
    bd = (head_of_lane[:, None] == head_of_lane[None, :]).astype(BF16)
    ws_tril = jnp.tril(w_spatial[l])
    wcat = jnp.concatenate([ws_tril[0::2], ws_tril[1::2]], axis=-1).astype(BF16)
    bs = b_spatial[l]
    bias = jnp.where(jnp.arange(LANES)[None, None, :] < HEAD_DIM, bs[0::2][:, :, None], bs[1::2][:, :, None])
    avec = jnp.repeat(w_spatial[l][:, 0, 0], HEAD_DIM).reshape(1, WIDTH)
    bvec = jnp.repeat(bs[:, 0], HEAD_DIM).reshape(1, WIDTH)
    ga = out_gain_a[l].reshape(1, WIDTH)
    gb = out_gain_b[l].reshape(1, WIDTH)
    wout = w_out[l].astype(BF16)
    g2 = norm2_g[l].reshape(1, D_MODEL)
    wr = jnp.zeros((D_MODEL, LANES), F32)
    wr = wr.at[:, :N_EXPERTS].set(jnp.transpose(w_router2[l], (1, 0, 2)).reshape(D_MODEL, N_EXPERTS))
    wr = wr.at[:, N_EXPERTS:N_EXPERTS + N_GROUPS].set(w_router1[l]).astype(BF16)
    br = jnp.zeros((1, LANES), F32)
    br = br.at[0, :N_EXPERTS].set(b_router2[l].reshape(N_EXPERTS))
    br = br.at[0, N_EXPERTS:N_EXPERTS + N_GROUPS].set(b_router1[l])
    lane_head = (jnp.arange(LANES)[:, None] == head_of_lane[None, :]).astype(BF16)
    expand = jnp.concatenate([lane_head, lane_head], axis=0)
    wg = w_gate[l].reshape(N_EXPERTS, D_MODEL, D_EXPERT)
    wu = w_up[l].reshape(N_EXPERTS, D_MODEL, D_EXPERT)
    wd = w_down[l].reshape(N_EXPERTS, D_EXPERT, D_MODEL)

    def tri(t):
        return (jnp.arange(t)[:, None] > jnp.arange(t)[None, :]).astype(BF16)

    xp = x_prompt.reshape(batch * seq, D_MODEL)
    qkv, oa, kt_last, vt_last = _proj_prompt(xp, g1, win, gains, bd, wcat, bias, seq=seq)
    obs, lses = [], []
    for (q, k, v), dil in zip(qkv, DILATIONS):
        o, lse = _band_attention(q, k, v, batch=batch, seq=seq, dil=dil)
        obs.append(o)
        lses.append(lse)
    tmix = min(MIX_TILE, batch * seq)
    upper = (jnp.arange(LANES)[:, None] < jnp.arange(LANES)[None, :]).astype(BF16)
    tmix_s = min(MIX_TILE, nb)
    h, sorted_rows, route, nck, ck0 = _mix(xp, oa, obs, lses, DILATIONS, ga, gb, wout, g2, wr, br, expand,
                                            tri(tmix), upper, spare_blocks=nb // tmix_s)
    kept = min(MAX_WINDOW, seq)
    to_cache = lambda t: jnp.transpose(t.reshape(1, batch, N_HEADS, HEAD_DIM, kept), (0, 1, 4, 2, 3))

    xs = x_sample.reshape(nb, D_MODEL)
    qs, ks, vs, vgs, oas = _proj_sample(xs, g1, win, gains, bd, avec, bvec)
    cols = lambda t: jnp.transpose(t.reshape(nb, N_HEADS, HEAD_DIM), (0, 2, 1))
    feature_major = lambda c: jnp.transpose(c, (0, 2, 3, 1)).reshape(nb, WIDTH, c.shape[1])
    ot = _sample_attention(cols(qs), cols(ks), cols(vs), feature_major(cache_k[l]), feature_major(cache_v[l]))
    obs_s = jnp.transpose(ot, (0, 2, 1)).reshape(nb, WIDTH)
    hs, sorted_rows, route_s, nck_s, ck0_s = _mix(xs, oas, [obs_s], [], (), ga, gb, wout, g2, wr, br, expand,
                                                  tri(tmix_s), upper, sorted_buf=sorted_rows,
                                                  first_block=batch * seq // tmix)

    y_prompt, y_sample = _grouped_moe([(h, route, nck, ck0), (hs, route_s, nck_s, ck0_s)], sorted_rows,
                                      wg, wu, wd)
    to5 = lambda t: t.reshape(1, nb, 1, N_HEADS, HEAD_DIM)
    return (y_prompt.reshape(batch, seq, D_MODEL), y_sample.reshape(nb, 1, D_MODEL),
            to_cache(kt_last), to_cache(vt_last), to5(ks), to5(vs), to5(vgs))
```
